```python
import math
import jax
import jax.numpy as jnp
from jax import lax
import numpy as np

D_MODEL = 1024
BATCH = 2
SEQ = 16384
DEPTH = 4

HEAD_DIM = 64
N_GROUPS_MIX = 4
GROUP_WIDTH = D_MODEL // N_GROUPS_MIX
N_SB_HEADS = GROUP_WIDTH // HEAD_DIM
N_SWA_HEADS = GROUP_WIDTH // HEAD_DIM
N_SWA_KV_HEADS = 2
N_FOX_HEADS = GROUP_WIDTH // HEAD_DIM
SSM_CH_PER_GROUP = 16
N_SSM_GROUPS = GROUP_WIDTH // SSM_CH_PER_GROUP
SSM_STATE = 64
WINDOW = 128
Q_BLOCK = 128
REL_BUCKETS = 32
REL_MAX_DIST = 128
D_FF = 4 * D_MODEL
NORM_EPS = 1e-6
DT_MIN = 1e-3
DT_MAX = 1e-1
N_ADA = 6

IN_SIZES = (GROUP_WIDTH, GROUP_WIDTH, GROUP_WIDTH,
            N_SWA_HEADS * HEAD_DIM, N_SWA_KV_HEADS * HEAD_DIM, N_SWA_KV_HEADS * HEAD_DIM,
            GROUP_WIDTH, GROUP_WIDTH, GROUP_WIDTH, N_FOX_HEADS,
            GROUP_WIDTH)
IN_WIDTH = sum(IN_SIZES)

STRICT_LOWER = np.tril(np.ones((Q_BLOCK, Q_BLOCK), dtype=bool), -1)
LOWER_INCL = np.tril(np.ones((Q_BLOCK, Q_BLOCK), dtype=bool), 0)
SUFFIX = np.tril(np.ones((Q_BLOCK, Q_BLOCK), dtype=np.float32), -1)

kernel_name = "hybrid_parallel_sb_swa_fox_s5"

F32 = jnp.float32


def rmsnorm(x, gain):
    x32 = x.astype(F32)
    y = x32 * lax.rsqrt(jnp.mean(x32 * x32, axis=-1, keepdims=True) + NORM_EPS)
    return (y * gain.astype(F32)).astype(x.dtype)


def to_query_blocks(t):
    b, s, h, d = t.shape
    return t.reshape(b, s // Q_BLOCK, Q_BLOCK, h, d).transpose(1, 0, 3, 2, 4)


def from_query_blocks(o):
    n, b, h, qb, d = o.shape
    return o.transpose(1, 0, 3, 2, 4).reshape(b, n * qb, h * d)


def causal_block_pairs(n):
    counts = np.arange(1, n + 1)
    qb = np.repeat(np.arange(n), counts)
    starts = np.repeat(np.cumsum(counts) - counts, counts)
    kb = qb - (np.arange(qb.size) - starts)
    return jnp.asarray(qb, jnp.int32), jnp.asarray(kb, jnp.int32)


def _take(blocks, i):
    return lax.dynamic_index_in_dim(blocks, i, 0, keepdims=False)


def stick_breaking_attention(q, k, v):
    b, s, h, d = q.shape
    n = s // Q_BLOCK
    scale = 1.0 / math.sqrt(d)
    qblk = to_query_blocks(q.astype(F32))
    kblk = to_query_blocks(k.astype(F32))
    vblk = to_query_blocks(v.astype(F32))
    strict = jnp.asarray(STRICT_LOWER)
    suffix = jnp.asarray(SUFFIX)

    def step(carry, idx):
        out, acc, run = carry
        qi, ki = idx
        first = qi == ki
        z = jnp.einsum('bhqd,bhkd->bhqk', _take(qblk, qi), _take(kblk, ki)) * scale
        mask = jnp.where(first, strict, True)
        lsz = jax.nn.log_sigmoid(z)
        lk = jnp.where(mask, lsz - z, 0.0)
        within = jnp.einsum('bhqk,kj->bhqj', lk, suffix)
        acc0 = jnp.where(first, 0.0, acc)
        run0 = jnp.where(first, 0.0, run)
        w = jnp.where(mask, jnp.exp(lsz + within + run0[..., None]), 0.0)
        acc = acc0 + jnp.einsum('bhqk,bhkd->bhqd', w, _take(vblk, ki))
        run = run0 + jnp.sum(lk, axis=-1)
        out = lax.dynamic_update_index_in_dim(out, acc, qi, 0)
        return (out, acc, run), None

    init = (jnp.zeros((n, b, h, Q_BLOCK, d), F32),
            jnp.zeros((b, h, Q_BLOCK, d), F32),
            jnp.zeros((b, h, Q_BLOCK), F32))
    (out, _, _), _ = lax.scan(step, init, causal_block_pairs(n))
    return from_query_blocks(out).astype(q.dtype)


def forgetting_attention(q, k, v, log_f):
    b, s, h, d = q.shape
    n = s // Q_BLOCK
    scale = 1.0 / math.sqrt(d)
    cum = jnp.cumsum(log_f.astype(F32), axis=1)
    fblk = cum.reshape(b, n, Q_BLOCK, h).transpose(1, 0, 3, 2)
    qblk = to_query_blocks(q.astype(F32))
    kblk = to_query_blocks(k.astype(F32))
    vblk = to_query_blocks(v.astype(F32))
    lower = jnp.asarray(LOWER_INCL)

    def step(carry, idx):
        out, acc, m, l = carry
        qi, ki = idx
        first = qi == ki
        z = jnp.einsum('bhqd,bhkd->bhqk', _take(qblk, qi), _take(kblk, ki)) * scale
        z = z + _take(fblk, qi)[..., None] - _take(fblk, ki)[..., None, :]
        mask = jnp.where(first, lower, True)
        zm = jnp.where(mask, z, -jnp.inf)
        m0 = jnp.where(first, -jnp.inf, m)
        l0 = jnp.where(first, 0.0, l)
        acc0 = jnp.where(first, 0.0, acc)
        m_new = jnp.maximum(m0, jnp.max(zm, axis=-1))
        corr = jnp.exp(m0 - m_new)
        e = jnp.exp(zm - m_new[..., None])
        l = l0 * corr + jnp.sum(e, axis=-1)
        acc = acc0 * corr[..., None] + jnp.einsum('bhqk,bhkd->bhqd', e, _take(vblk, ki))
        out = lax.dynamic_update_index_in_dim(out, acc / l[..., None], qi, 0)
        return (out, acc, m_new, l), None

    init = (jnp.zeros((n, b, h, Q_BLOCK, d), F32),
            jnp.zeros((b, h, Q_BLOCK, d), F32),
            jnp.full((b, h, Q_BLOCK), -jnp.inf, F32),
            jnp.zeros((b, h, Q_BLOCK), F32))
    (out, _, _, _), _ = lax.scan(step, init, causal_block_pairs(n))
    return from_query_blocks(out).astype(q.dtype)


def t5_causal_buckets(dist):
    max_exact = REL_BUCKETS // 2
    safe = np.maximum(dist, 1).astype(np.float32)
    large = max_exact + (np.log(safe / max_exact) / math.log(REL_MAX_DIST / max_exact)
                         * (REL_BUCKETS - max_exact)).astype(np.int32)
    large = np.minimum(large, REL_BUCKETS - 1)
    return np.where(dist < max_exact, dist, large).astype(np.int32)


def sliding_window_attention(q, k, v, rel_table, sink):
    b, s, hq, d = q.shape
    hkv = k.shape[2]
    g = hq // hkv
    n = s // WINDOW
    qb = q.astype(F32).reshape(b, n, WINDOW, hkv, g, d)

    def band(t):
        t = t.astype(F32).reshape(b, n, WINDOW, hkv, d)
        prev = jnp.pad(t, ((0, 0), (1, 0), (0, 0), (0, 0), (0, 0)))[:, :-1]
        return jnp.concatenate([prev, t], axis=2)

    kb, vb = band(k), band(v)
    z = jnp.einsum('bnqhgd,bnkhd->bnhgqk', qb, kb) / math.sqrt(d)
    i = np.arange(WINDOW)[:, None]
    j = np.arange(2 * WINDOW)[None, :]
    dist = WINDOW + i - j
    in_window = (dist >= 0) & (dist < WINDOW)
    bucket = t5_causal_buckets(np.clip(dist, 0, None))
    bias = rel_table.astype(F32)[bucket]
    bias = bias.transpose(2, 0, 1).reshape(hkv, g, WINDOW, 2 * WINDOW)
    key_exists = (jnp.arange(n)[:, None] > 0) | (j[0] >= WINDOW)[None, :]
    valid = in_window[None] & key_exists[:, None, :]
    z = jnp.where(valid[None, :, None, None], z + bias, -jnp.inf)
    sink_col = jnp.broadcast_to(sink.astype(F32).reshape(1, 1, hkv, g, 1, 1),
                                z.shape[:-1] + (1,))
    probs = jax.nn.softmax(jnp.concatenate([z, sink_col], axis=-1), axis=-1)[..., :-1]
    o = jnp.einsum('bnhgqk,bnkhd->bnqhgd', probs, vb)
    return o.reshape(b, s, hq * d).astype(q.dtype)


def _ssm_combine(e1, e2):
    a1r, a1i, b1r, b1i = e1
    a2r, a2i, b2r, b2i = e2
    ar = a2r * a1r - a2i * a1i
    ai = a2r * a1i + a2i * a1r
    br = a2r * b1r - a2i * b1i + b2r
    bi = a2r * b1i + a2i * b1r + b2i
    return (ar, ai, br, bi)


def s5_ssm(u, lam_re, lam_im, log_dt, b_re, b_im, c_re, c_im, d_skip, w_glu, b_glu):
    bsz, s, _ = u.shape
    u32 = u.astype(F32).reshape(bsz, s, N_SSM_GROUPS, SSM_CH_PER_GROUP)
    dt = jnp.exp(log_dt.astype(F32))[:, None]
    lr, li = lam_re.astype(F32), lam_im.astype(F32)
    mag = jnp.exp(lr * dt)
    ang = li * dt
    a_re, a_im = mag * jnp.cos(ang), mag * jnp.sin(ang)
    den = lr * lr + li * li
    nr, ni = a_re - 1.0, a_im
    coef_re = (nr * lr + ni * li) / den
    coef_im = (ni * lr - nr * li) / den
    br, bi = b_re.astype(F32), b_im.astype(F32)
    bb_re = coef_re[..., None] * br - coef_im[..., None] * bi
    bb_im = coef_re[..., None] * bi + coef_im[..., None] * br
    bu_re = jnp.einsum('bsgh,gph->bsgp', u32, bb_re)
    bu_im = jnp.einsum('bsgh,gph->bsgp', u32, bb_im)
    a_re_t = jnp.broadcast_to(a_re, bu_re.shape)
    a_im_t = jnp.broadcast_to(a_im, bu_re.shape)
    _, _, x_re, x_im = lax.associative_scan(_ssm_combine, (a_re_t, a_im_t, bu_re, bu_im), axis=1)
    y = (jnp.einsum('bsgp,ghp->bsgh', x_re, c_re.astype(F32))
         - jnp.einsum('bsgp,ghp->bsgh', x_im, c_im.astype(F32))
         + d_skip.astype(F32) * u32)
    y = jax.nn.gelu(y.reshape(bsz, s, N_SSM_GROUPS * SSM_CH_PER_GROUP))
    gate = jax.nn.sigmoid(y @ w_glu.astype(F32) + b_glu.astype(F32))
    return (y * gate).astype(u.dtype)


def setup_inputs(seed: int = 0) -> dict:
    key = jax.random.key(seed)
    ks = jax.random.split(key, 32)
    nrm = lambda k, shape, s=1.0: jax.random.normal(k, shape, F32) * s
    P, G, H = SSM_STATE, N_SSM_GROUPS, SSM_CH_PER_GROUP
    lam_im_base = math.pi * jnp.arange(P, dtype=F32)
    return {
        "x": nrm(ks[0], (BATCH, SEQ, D_MODEL)),
        "c": nrm(ks[1], (BATCH, D_MODEL)),
        "w_ada": nrm(ks[2], (DEPTH, D_MODEL, N_ADA * D_MODEL), 0.5 * D_MODEL ** -0.5),
        "b_ada": nrm(ks[3], (DEPTH, N_ADA * D_MODEL), 0.01),
        "norm1_gain": 1.0 + nrm(ks[4], (DEPTH, D_MODEL), 0.02),
        "norm2_gain": 1.0 + nrm(ks[5], (DEPTH, D_MODEL), 0.02),
        "w_in": nrm(ks[6], (DEPTH, D_MODEL, IN_WIDTH), D_MODEL ** -0.5),
        "rel_bias": nrm(ks[7], (REL_BUCKETS, N_SWA_HEADS), 0.5),
        "sinks": nrm(ks[8], (DEPTH, N_SWA_HEADS)),
        "forget_bias": 3.0 + nrm(ks[9], (DEPTH, N_FOX_HEADS), 0.5),
        "lam_re": -0.5 + nrm(ks[10], (DEPTH, G, P), 0.01),
        "lam_im": lam_im_base + nrm(ks[11], (DEPTH, G, P), 0.01),
        "log_dt": jax.random.uniform(ks[12], (DEPTH, G), F32, math.log(DT_MIN), math.log(DT_MAX)),
        "ssm_b_re": nrm(ks[13], (DEPTH, G, P, H), (2 * H) ** -0.5),
        "ssm_b_im": nrm(ks[14], (DEPTH, G, P, H), (2 * H) ** -0.5),
        "ssm_c_re": nrm(ks[15], (DEPTH, G, H, P), P ** -0.5),
        "ssm_c_im": nrm(ks[16], (DEPTH, G, H, P), P ** -0.5),
        "ssm_d": nrm(ks[17], (DEPTH, G, H)),
        "w_glu": nrm(ks[18], (DEPTH, GROUP_WIDTH, GROUP_WIDTH), GROUP_WIDTH ** -0.5),
        "b_glu": nrm(ks[19], (DEPTH, GROUP_WIDTH), 0.01),
        "out_gain": 1.0 + nrm(ks[20], (DEPTH, D_MODEL), 0.02),
        "w_out": nrm(ks[21], (DEPTH, D_MODEL, D_MODEL), D_MODEL ** -0.5),
        "w_mlp_in": nrm(ks[22], (DEPTH, D_MODEL, D_FF), D_MODEL ** -0.5),
        "w_mlp_out": nrm(ks[23], (DEPTH, D_FF, D_MODEL), D_FF ** -0.5),
        "final_gain": 1.0 + nrm(ks[24], (D_MODEL,), 0.02),
    }


def reference(x, c, w_ada, b_ada, norm1_gain, norm2_gain, w_in, rel_bias, sinks,
              forget_bias, lam_re, lam_im, log_dt, ssm_b_re, ssm_b_im, ssm_c_re,
              ssm_c_im, ssm_d, w_glu, b_glu, out_gain, w_out, w_mlp_in, w_mlp_out,
              final_gain):
    bsz, s, _ = x.shape
    split_points = [int(p) for p in np.cumsum(IN_SIZES)[:-1]]
    heads = lambda t, h: t.reshape(bsz, s, h, HEAD_DIM)
    c_act = jax.nn.silu(c)
    for l in range(DEPTH):
        mod = c_act @ w_ada[l] + b_ada[l]
        sh1, sc1, g1, sh2, sc2, g2 = [m[:, None, :] for m in jnp.split(mod, N_ADA, axis=-1)]

        h = rmsnorm(x, norm1_gain[l]) * (1.0 + sc1) + sh1
        proj = h @ w_in[l]
        (sb_q, sb_k, sb_v, sw_q, sw_k, sw_v,
         fx_q, fx_k, fx_v, fx_f, ssm_u) = jnp.split(proj, split_points, axis=-1)

        o_sb = stick_breaking_attention(heads(sb_q, N_SB_HEADS), heads(sb_k, N_SB_HEADS),
                                        heads(sb_v, N_SB_HEADS))
        o_sw = sliding_window_attention(heads(sw_q, N_SWA_HEADS), heads(sw_k, N_SWA_KV_HEADS),
                                        heads(sw_v, N_SWA_KV_HEADS), rel_bias, sinks[l])
        log_f = jax.nn.log_sigmoid(fx_f.astype(F32) + forget_bias[l].astype(F32))
        o_fx = forgetting_attention(heads(fx_q, N_FOX_HEADS), heads(fx_k, N_FOX_HEADS),
                                    heads(fx_v, N_FOX_HEADS), log_f)
        o_ssm = s5_ssm(ssm_u, lam_re[l], lam_im[l], log_dt[l], ssm_b_re[l], ssm_b_im[l],
                       ssm_c_re[l], ssm_c_im[l], ssm_d[l], w_glu[l], b_glu[l])

        mixed = jnp.concatenate([o_sb, o_sw, o_fx, o_ssm], axis=-1)
        mixed = rmsnorm(mixed.reshape(bsz, s, N_GROUPS_MIX, GROUP_WIDTH),
                        out_gain[l].reshape(N_GROUPS_MIX, GROUP_WIDTH)).reshape(bsz, s, D_MODEL)
        x = x + g1 * (mixed @ w_out[l])

        h = rmsnorm(x, norm2_gain[l]) * (1.0 + sc2) + sh2
        x = x + g2 * (jnp.square(jax.nn.relu(h @ w_mlp_in[l])) @ w_mlp_out[l])
    return rmsnorm(x, final_gain)
```

```python
import functools
import math

import numpy as np
import jax
import jax.numpy as jnp
from jax import lax
from jax.experimental import pallas as pl
from jax.experimental.pallas import tpu as pltpu

F32 = jnp.float32
BF16 = jnp.bfloat16

HEAD_DIM = 64
LANES = 128
GROUP_WIDTH = 256
N_KV_SWA = 2
WINDOW = 128
REL_BUCKETS = 32
REL_MAX_DIST = 128
N_ADA = 6
NORM_EPS = 1e-6
SSM_H = 16
SSM_P = 64
SSM_CHUNK = 16

ATT_TILE = 512
ROW_TILE = 512
SSM_TILE = 1024
VMEM_LIMIT_BYTES = 56 * 1024 * 1024


def _dot(a, b):
    return jnp.dot(a, b, preferred_element_type=F32)


def _dot_nt(a, b):
    return lax.dot_general(a, b, (((1,), (1,)), ((), ())), preferred_element_type=F32)


def _split_bf16(a):
    hi = a.astype(BF16)
    lo = (a - hi.astype(F32)).astype(BF16)
    return hi, lo


def _rms(x):
    return x * lax.rsqrt(jnp.mean(x * x, axis=-1, keepdims=True) + NORM_EPS)


def _params(*sem):
    return pltpu.CompilerParams(dimension_semantics=sem, vmem_limit_bytes=VMEM_LIMIT_BYTES)


def _ada_kernel(c_ref, w_ref, b_ref, o_ref):
    c = c_ref[...]
    ca = c * jax.nn.sigmoid(c)
    a_hi, a_lo = _split_bf16(ca)
    w_hi, w_lo = _split_bf16(w_ref[...])
    o_ref[...] = _dot(a_hi, w_hi) + _dot(a_hi, w_lo) + _dot(a_lo, w_hi) + b_ref[...]


def _ada(c_pad, w_ada, b_ada):
    depth, d, n = w_ada.shape
    tn = 2048
    return pl.pallas_call(
        _ada_kernel,
        out_shape=jax.ShapeDtypeStruct((depth, c_pad.shape[0], n), F32),
        grid=(depth, n // tn),
        in_specs=[
            pl.BlockSpec(c_pad.shape, lambda l, j: (0, 0)),
            pl.BlockSpec((None, d, tn), lambda l, j: (l, 0, j)),
            pl.BlockSpec((None, 1, tn), lambda l, j: (l, 0, j)),
        ],
        out_specs=pl.BlockSpec((None, c_pad.shape[0], tn), lambda l, j: (l, 0, j)),
        compiler_params=_params("arbitrary", "arbitrary"),
        name="ada_mod",
    )(c_pad, w_ada, b_ada.reshape(depth, 1, n))


def _ssm_prep_kernel(lr_ref, li_ref, ldt_ref, br_ref, bi_ref, avec_ref, bbr_ref, bbi_ref):
    lr = lr_ref[...]
    li = li_ref[...]
    dt = jnp.exp(ldt_ref[...])
    mag = jnp.exp(lr * dt)
    ang = li * dt
    ar = mag * jnp.cos(ang)
    ai = mag * jnp.sin(ang)
    den = lr * lr + li * li
    nr = ar - 1.0
    ni = ai
    cr = (nr * lr + ni * li) / den
    ci = (ni * lr - nr * li) / den
    pr, pi_ = ar, ai
    for _ in range(int(math.log2(SSM_CHUNK))):
        pr, pi_ = pr * pr - pi_ * pi_, 2.0 * pr * pi_
    avec_ref[0:1, :] = ar
    avec_ref[1:2, :] = ai
    avec_ref[2:3, :] = pr
    avec_ref[3:4, :] = pi_
    avec_ref[4:8, :] = jnp.zeros((4, ar.shape[1]), F32)
    br = br_ref[...]
    bi = bi_ref[...]
    bbr_ref[...] = cr * br - ci * bi
    bbi_ref[...] = cr * bi + ci * br


def _ssm_prep(lam_re, lam_im, log_dt, b_re, b_im):
    depth, g, p = lam_re.shape
    h = b_re.shape[-1]
    n = g * p
    row = lambda a: a.reshape(depth, 1, n)
    ldt = jnp.broadcast_to(log_dt[:, :, None], (depth, g, p))
    bt = lambda a: a.transpose(0, 3, 1, 2).reshape(depth, h, n)
    vec = pl.BlockSpec((None, 1, n), lambda l: (l, 0, 0))
    mat = pl.BlockSpec((None, h, n), lambda l: (l, 0, 0))
    return pl.pallas_call(
        _ssm_prep_kernel,
        out_shape=(jax.ShapeDtypeStruct((depth, 8, n), F32),
                   jax.ShapeDtypeStruct((depth, h, n), F32),
                   jax.ShapeDtypeStruct((depth, h, n), F32)),
        grid=(depth,),
        in_specs=[vec, vec, vec, mat, mat],
        out_specs=(pl.BlockSpec((None, 8, n), lambda l: (l, 0, 0)), mat, mat),
        compiler_params=_params("arbitrary"),
        name="ssm_prep",
    )(row(lam_re), row(lam_im), row(ldt), bt(b_re), bt(b_im))


def _inproj_kernel(x_ref, mod_ref, gain_ref, w_sbq, w_sbv, w_swq, w_swv, w_fxq, w_fxv, w_u,
                   wkt_ref, wft_ref, fb_ref,
                   sbq_o, sbv_o, swq_o, swv_o, fxq_o, fxv_o, u_o, sbkt_o, fxkt_o, swkt_o, ft_o,
                   carry_ref):
    ts = x_ref.shape[0]
    mod = mod_ref[...]
    sh1 = mod[0:1]
    sc1 = mod[1:2]
    h = (_rms(x_ref[...]) * gain_ref[...] * (1.0 + sc1) + sh1).astype(BF16)
    sbq_o[...] = _dot(h, w_sbq[...]).astype(BF16)
    sbv_o[...] = _dot(h, w_sbv[...]).astype(BF16)
    swq_o[...] = _dot(h, w_swq[...]).astype(BF16)
    swv_o[...] = _dot(h, w_swv[...]).astype(BF16)
    fxq_o[...] = _dot(h, w_fxq[...]).astype(BF16)
    fxv_o[...] = _dot(h, w_fxv[...]).astype(BF16)
    u_o[...] = _dot(h, w_u[...])
    kt = _dot_nt(wkt_ref[...], h)
    sbkt_o[...] = kt[0:GROUP_WIDTH].astype(BF16)
    fxkt_o[...] = kt[GROUP_WIDTH:2 * GROUP_WIDTH].astype(BF16)
    swkt_o[...] = kt[2 * GROUP_WIDTH:].astype(BF16)
    f = _dot_nt(wft_ref[...], h) + fb_ref[...]
    logf = jnp.minimum(f, 0.0) - jnp.log1p(jnp.exp(-jnp.abs(f)))
    lane = lax.broadcasted_iota(jnp.int32, logf.shape, 1)
    cum = logf
    k = 1
    while k < ts:
        cum = cum + jnp.where(lane >= k, pltpu.roll(cum, k, axis=1), 0.0)
        k *= 2

    @pl.when(pl.program_id(1) == 0)
    def _():
        carry_ref[...] = jnp.zeros_like(carry_ref)

    cum = cum + carry_ref[:, 0:1]
    ft_o[...] = cum
    carry_ref[...] = jnp.broadcast_to(cum[:, ts - 1:ts], carry_ref.shape)


def _inproj(l, x, mod, gain, wts, forget_b):
    b, s, d = x.shape
    ts = ATT_TILE
    nk = s // ts
    (w_sbq, w_sbv, w_swq, w_swv, w_fxq, w_fxv, w_u, wkt, wft) = wts
    row = lambda width: pl.BlockSpec((None, ts, width), lambda bi, i: (bi, i, 0))
    wspec = lambda w: pl.BlockSpec((None,) + w.shape[1:], lambda bi, i: (l, 0, 0))
    ktspec = lambda width: pl.BlockSpec((None, None, width, ts), lambda bi, i: (bi, i, 0, 0))
    kv_w = N_KV_SWA * HEAD_DIM
    out_shape = (
        jax.ShapeDtypeStruct((b, s, GROUP_WIDTH), BF16),
        jax.ShapeDtypeStruct((b, s, GROUP_WIDTH), BF16),
        jax.ShapeDtypeStruct((b, s, GROUP_WIDTH), BF16),
        jax.ShapeDtypeStruct((b, s, kv_w), BF16),
        jax.ShapeDtypeStruct((b, s, GROUP_WIDTH), BF16),
        jax.ShapeDtypeStruct((b, s, GROUP_WIDTH), BF16),
        jax.ShapeDtypeStruct((b, s, GROUP_WIDTH), F32),
        jax.ShapeDtypeStruct((b, nk, GROUP_WIDTH, ts), BF16),
        jax.ShapeDtypeStruct((b, nk, GROUP_WIDTH, ts), BF16),
        jax.ShapeDtypeStruct((b, nk, kv_w, ts), BF16),
        jax.ShapeDtypeStruct((b, 8, s), F32),
    )
    out_specs = (row(GROUP_WIDTH), row(GROUP_WIDTH), row(GROUP_WIDTH), row(kv_w), row(GROUP_WIDTH),
                 row(GROUP_WIDTH), row(GROUP_WIDTH), ktspec(GROUP_WIDTH), ktspec(GROUP_WIDTH),
                 ktspec(kv_w), pl.BlockSpec((None, 8, ts), lambda bi, i: (bi, 0, i)))
    return pl.pallas_call(
        _inproj_kernel,
        out_shape=out_shape,
        grid=(b, nk),
        in_specs=[
            pl.BlockSpec((None, ts, d), lambda bi, i: (bi, i, 0)),
            pl.BlockSpec((None, None, N_ADA, d), lambda bi, i: (l, bi, 0, 0)),
            pl.BlockSpec((None, 1, d), lambda bi, i: (l, 0, 0)),
            wspec(w_sbq), wspec(w_sbv), wspec(w_swq), wspec(w_swv), wspec(w_fxq), wspec(w_fxv),
            wspec(w_u), wspec(wkt), wspec(wft),
            pl.BlockSpec((None, 8, 1), lambda bi, i: (l, 0, 0)),
        ],
        out_specs=out_specs,
        scratch_shapes=[pltpu.VMEM((8, LANES), F32)],
        compiler_params=_params("arbitrary", "arbitrary"),
        name="in_proj",
    )(x, mod, gain, w_sbq, w_sbv, w_swq, w_swv, w_fxq, w_fxv, w_u, wkt, wft, forget_b)


def _pair_split(q):
    lane = lax.broadcasted_iota(jnp.int32, q.shape, 1)
    left = lane < HEAD_DIM
    zero = jnp.zeros_like(q)
    return left, (jnp.where(left, q, zero), jnp.where(left, zero, q))


def _sb_kernel(q_ref, kt_ref, v_ref, suf_ref, o_ref):
    t = q_ref.shape[0]
    qi = pl.program_id(2)
    left, qh = _pair_split(q_ref[...])
    row = lax.broadcasted_iota(jnp.int32, (t, t), 0)
    col = lax.broadcasted_iota(jnp.int32, (t, t), 1)
    strict = col < row
    suf = suf_ref[...]

    def tile(ki, carry, diag):
        run, acc = carry
        kt = kt_ref[ki]
        v = v_ref[pl.ds(pl.multiple_of(ki * t, t), t), :]
        new_run, pv = [], []
        for j in range(2):
            z = _dot(qh[j], kt)
            sp = jnp.log1p(jnp.exp(-jnp.abs(z)))
            lsz = jnp.minimum(z, 0.0) - sp
            lk = jnp.minimum(-z, 0.0) - sp
            if diag:
                lk = jnp.where(strict, lk, 0.0)
            hi, lo = _split_bf16(lk)
            within = _dot(hi, suf) + _dot(lo, suf)
            w = jnp.exp(lsz + within + run[j])
            if diag:
                w = jnp.where(strict, w, 0.0)
            pv.append(_dot(w.astype(BF16), v))
            new_run.append(run[j] + jnp.sum(lk, axis=-1, keepdims=True))
        return tuple(new_run), acc + jnp.where(left, pv[0], pv[1])

    zero = jnp.zeros((t, 1), F32)
    carry = tile(qi, ((zero, zero), jnp.zeros((t, LANES), F32)), True)
    carry = lax.fori_loop(0, qi, lambda i, c: tile(qi - 1 - i, c, False), carry)
    o_ref[...] = carry[1]


def _fox_kernel(q_ref, kt_ref, v_ref, f_ref, o_ref):
    t = q_ref.shape[0]
    qi = pl.program_id(2)
    left, qh = _pair_split(q_ref[...])
    row = lax.broadcasted_iota(jnp.int32, (t, t), 0)
    col = lax.broadcasted_iota(jnp.int32, (t, t), 1)
    causal = col <= row

    def tile(ki, carry, diag):
        m, l, acc = carry
        kt = kt_ref[ki]
        v = v_ref[pl.ds(pl.multiple_of(ki * t, t), t), :]
        f = f_ref[ki]
        new_m, new_l, corr, pv = [], [], [], []
        for j in range(2):
            s = _dot(qh[j], kt) - f[j:j + 1, :]
            if diag:
                s = jnp.where(causal, s, -jnp.inf)
            mj = jnp.maximum(m[j], jnp.max(s, axis=-1, keepdims=True))
            cj = jnp.exp(m[j] - mj)
            e = jnp.exp(s - mj)
            new_m.append(mj)
            new_l.append(l[j] * cj + jnp.sum(e, axis=-1, keepdims=True))
            corr.append(cj)
            pv.append(_dot(e.astype(BF16), v))
        acc = acc * jnp.where(left, corr[0], corr[1]) + jnp.where(left, pv[0], pv[1])
        return tuple(new_m), tuple(new_l), acc

    ninf = jnp.full((t, 1), -jnp.inf, F32)
    zero = jnp.zeros((t, 1), F32)
    carry = tile(qi, ((ninf, ninf), (zero, zero), jnp.zeros((t, LANES), F32)), True)
    carry = lax.fori_loop(0, qi, lambda i, c: tile(qi - 1 - i, c, False), carry)
    _, l, acc = carry
    o_ref[...] = acc / jnp.where(left, l[0], l[1])


def _causal_attention(kernel, name, q, kt, v, extra, extra_spec):
    b, s, _ = q.shape
    t = ATT_TILE
    nk = s // t
    n_pairs = GROUP_WIDTH // LANES
    return pl.pallas_call(
        kernel,
        out_shape=jax.ShapeDtypeStruct((b, s, GROUP_WIDTH), F32),
        grid=(b, n_pairs, nk),
        in_specs=[
            pl.BlockSpec((None, t, LANES), lambda bi, p, i: (bi, i, p)),
            pl.BlockSpec((None, nk, LANES, t), lambda bi, p, i: (bi, 0, p, 0)),
            pl.BlockSpec((None, s, LANES), lambda bi, p, i: (bi, 0, p)),
            extra_spec,
        ],
        out_specs=pl.BlockSpec((None, t, LANES), lambda bi, p, i: (bi, i, p)),
        compiler_params=_params("arbitrary", "arbitrary", "arbitrary"),
        name=name,
    )(q, kt, v, extra)


def _stick_breaking(q, kt, v):
    t = ATT_TILE
    suffix = jnp.asarray(np.tril(np.ones((t, t), np.float32), -1), BF16)
    return _causal_attention(_sb_kernel, "stick_breaking", q, kt, v, suffix,
                             pl.BlockSpec((t, t), lambda bi, p, i: (0, 0)))


def _forgetting(q, kt, v, ft):
    b, _, s = ft.shape
    t = ATT_TILE
    nk = s // t
    n_pairs = GROUP_WIDTH // LANES
    f = ft[:, :2 * n_pairs].reshape(b, n_pairs, 2, nk, t).transpose(0, 1, 3, 2, 4)
    return _causal_attention(_fox_kernel, "forgetting", q, kt, v, f,
                             pl.BlockSpec((None, None, nk, 2, t), lambda bi, p, i: (bi, p, 0, 0, 0)))


def _swa_kernel(q_ref, ktc_ref, ktp_ref, vc_ref, vp_ref, bias_ref, sink_ref, o_ref):
    r = q_ref.shape[0]
    first = pl.program_id(1) == 0
    lane = lax.broadcasted_iota(jnp.int32, (WINDOW, LANES), 1)
    left = lane < HEAD_DIM
    col = lax.broadcasted_iota(jnp.int32, (WINDOW, 2 * WINDOW), 1)
    for w in range(r // WINDOW):
        lo = w * WINDOW
        if w == 0:
            kt = jnp.concatenate([ktp_ref[:, r - WINDOW:], ktc_ref[:, :WINDOW]], axis=1)
            v = jnp.concatenate([vp_ref[...], vc_ref[:WINDOW, :]], axis=0)
        else:
            kt = ktc_ref[:, lo - WINDOW:lo + WINDOW]
            v = vc_ref[lo - WINDOW:lo + WINDOW, :]
        for g in range(2):
            qg = q_ref[lo:lo + WINDOW, g * LANES:(g + 1) * LANES]
            zero = jnp.zeros_like(qg)
            pv = []
            for kv in range(N_KV_SWA):
                head = 2 * kv + g
                qm = jnp.where(left, qg, zero) if kv == 0 else jnp.where(left, zero, qg)
                z = _dot(qm, kt) + bias_ref[head]
                if w == 0:
                    z = jnp.where(jnp.logical_and(first, col < WINDOW), -jnp.inf, z)
                sink = sink_ref[head:head + 1, 0:1]
                m = jnp.maximum(jnp.max(z, axis=-1, keepdims=True), sink)
                e = jnp.exp(z - m)
                den = jnp.sum(e, axis=-1, keepdims=True) + jnp.exp(sink - m)
                pv.append(_dot((e / den).astype(BF16), v))
            o_ref[lo:lo + WINDOW, g * LANES:(g + 1) * LANES] = jnp.where(left, pv[0], pv[1])


def _sliding_window(l, q, kt, v, bias, sinks):
    b, s, _ = q.shape
    r = ATT_TILE
    kv_w = N_KV_SWA * HEAD_DIM
    wpt = r // WINDOW
    return pl.pallas_call(
        _swa_kernel,
        out_shape=jax.ShapeDtypeStruct((b, s, GROUP_WIDTH), F32),
        grid=(b, s // r),
        in_specs=[
            pl.BlockSpec((None, r, GROUP_WIDTH), lambda bi, i: (bi, i, 0)),
            pl.BlockSpec((None, None, kv_w, r), lambda bi, i: (bi, i, 0, 0)),
            pl.BlockSpec((None, None, kv_w, r), lambda bi, i: (bi, jnp.maximum(i - 1, 0), 0, 0)),
            pl.BlockSpec((None, r, kv_w), lambda bi, i: (bi, i, 0)),
            pl.BlockSpec((None, WINDOW, kv_w), lambda bi, i: (bi, jnp.maximum(i * wpt - 1, 0), 0)),
            pl.BlockSpec(bias.shape, lambda bi, i: (0, 0, 0)),
            pl.BlockSpec((None,) + sinks.shape[1:], lambda bi, i: (l, 0, 0)),
        ],
        out_specs=pl.BlockSpec((None, r, GROUP_WIDTH), lambda bi, i: (bi, i, 0)),
        compiler_params=_params("arbitrary", "arbitrary"),
        name="sliding_window",
    )(q, kt, kt, v, v, bias, sinks)


def _swa_bias(rel_bias):
    i = np.arange(WINDOW)[:, None]
    j = np.arange(2 * WINDOW)[None, :]
    dist = WINDOW + i - j
    in_window = (dist >= 0) & (dist < WINDOW)
    d = np.clip(dist, 0, None)
    max_exact = REL_BUCKETS // 2
    safe = np.maximum(d, 1).astype(np.float32)
    large = max_exact + (np.log(safe / max_exact) / math.log(REL_MAX_DIST / max_exact)
                         * (REL_BUCKETS - max_exact)).astype(np.int32)
    large = np.minimum(large, REL_BUCKETS - 1)
    bucket = np.where(d < max_exact, d, large).astype(np.int32)
    bias = rel_bias.astype(F32)[bucket].transpose(2, 0, 1)
    return jnp.where(jnp.asarray(in_window)[None], bias, -jnp.inf)


def _ssm_kernel(u_ref, avec_ref, bmat_ref, cmat_ref, d_ref, wglu_ref, bglu_ref, o_ref,
                xr_ref, xi_ref, er_ref, ei_ref, sr_ref, si_ref, carry_ref):
    n_slab, ts, _ = xr_ref.shape
    n = ts // SSM_CHUNK
    np_ = n_slab * LANES

    @pl.when(pl.program_id(1) == 0)
    def _():
        carry_ref[...] = jnp.zeros_like(carry_ref)

    u = u_ref[...]
    bu = _dot(u.astype(BF16), bmat_ref[...])
    for j in range(n_slab):
        xr_ref[j] = bu[:, j * LANES:(j + 1) * LANES]
        xi_ref[j] = bu[:, np_ + j * LANES:np_ + (j + 1) * LANES]

    def rows(i):
        return pl.ds(i, n, stride=SSM_CHUNK)

    def coef(k, j):
        return avec_ref[k:k + 1, j * LANES:(j + 1) * LANES]

    for j in range(n_slab):
        ar, ai = coef(0, j), coef(1, j)
        pr, pi_ = xr_ref[j, rows(0), :], xi_ref[j, rows(0), :]
        for i in range(1, SSM_CHUNK):
            nr = ar * pr - ai * pi_ + xr_ref[j, rows(i), :]
            ni = ar * pi_ + ai * pr + xi_ref[j, rows(i), :]
            xr_ref[j, rows(i), :] = nr
            xi_ref[j, rows(i), :] = ni
            pr, pi_ = nr, ni
        er_ref[j] = pr
        ei_ref[j] = pi_

    a16 = [(coef(2, j), coef(3, j)) for j in range(n_slab)]

    def chunk(c, s):
        out = []
        for j in range(n_slab):
            sr, si = s[j]
            sr_ref[j, pl.ds(c, 1), :] = sr
            si_ref[j, pl.ds(c, 1), :] = si
            er = er_ref[j, pl.ds(c, 1), :]
            ei = ei_ref[j, pl.ds(c, 1), :]
            a16r, a16i = a16[j]
            out.append((a16r * sr - a16i * si + er, a16r * si + a16i * sr + ei))
        return tuple(out)

    init = tuple((carry_ref[0:1, j * LANES:(j + 1) * LANES], carry_ref[1:2, j * LANES:(j + 1) * LANES])
                 for j in range(n_slab))
    fin = lax.fori_loop(0, n, chunk, init)
    for j in range(n_slab):
        carry_ref[0:1, j * LANES:(j + 1) * LANES] = fin[j][0]
        carry_ref[1:2, j * LANES:(j + 1) * LANES] = fin[j][1]

    for j in range(n_slab):
        ar, ai = coef(0, j), coef(1, j)
        pr, pi_ = sr_ref[j], si_ref[j]
        for i in range(SSM_CHUNK):
            pr, pi_ = ar * pr - ai * pi_, ar * pi_ + ai * pr
            xr_ref[j, rows(i), :] = xr_ref[j, rows(i), :] + pr
            xi_ref[j, rows(i), :] = xi_ref[j, rows(i), :] + pi_

    xr = jnp.concatenate([xr_ref[j].astype(BF16) for j in range(n_slab)], axis=1)
    xi = jnp.concatenate([xi_ref[j].astype(BF16) for j in range(n_slab)], axis=1)
    y = _dot(xr, cmat_ref[0:np_, :]) + _dot(xi, cmat_ref[np_:, :]) + d_ref[...] * u
    y = 0.5 * y * (1.0 + jnp.tanh(math.sqrt(2.0 / math.pi) * (y + 0.044715 * (y * y * y))))
    gate = jax.nn.sigmoid(_dot(y.astype(BF16), wglu_ref[...]) + bglu_ref[...])
    o_ref[...] = y * gate


def _ssm(l, u, avec, bmat, cmat, d_skip, w_glu, b_glu):
    b, s, cw = u.shape
    ts = min(SSM_TILE, s)
    np_ = avec.shape[-1]
    n = ts // SSM_CHUNK
    n_slab = np_ // LANES
    lspec = lambda a: pl.BlockSpec((None,) + a.shape[1:], lambda bi, i: (l, 0, 0))
    return pl.pallas_call(
        _ssm_kernel,
        out_shape=jax.ShapeDtypeStruct((b, s, cw), F32),
        grid=(b, s // ts),
        in_specs=[pl.BlockSpec((None, ts, cw), lambda bi, i: (bi, i, 0)),
                  lspec(avec), lspec(bmat), lspec(cmat), lspec(d_skip), lspec(w_glu), lspec(b_glu)],
        out_specs=pl.BlockSpec((None, ts, cw), lambda bi, i: (bi, i, 0)),
        scratch_shapes=[pltpu.VMEM((n_slab, ts, LANES), F32), pltpu.VMEM((n_slab, ts, LANES), F32),
                        pltpu.VMEM((n_slab, n, LANES), F32), pltpu.VMEM((n_slab, n, LANES), F32),
                        pltpu.VMEM((n_slab, n, LANES), F32), pltpu.VMEM((n_slab, n, LANES), F32),
                        pltpu.VMEM((8, np_), F32)],
        compiler_params=_params("arbitrary", "arbitrary"),
        name="s5_ssm",
    )(u, avec, bmat, cmat, d_skip, w_glu, b_glu)


def _post_kernel(osb_ref, osw_ref, ofx_ref, ossm_ref, x_ref, mod_ref, og_ref, n2_ref, wout_ref,
                 w1_ref, w2_ref, fg_ref, o_ref, *, final):
    mod = mod_ref[...]
    g1, sh2, sc2, g2 = mod[2:3], mod[3:4], mod[4:5], mod[5:6]
    mo = None
    for k, r in enumerate((osb_ref, osw_ref, ofx_ref, ossm_ref)):
        lo = k * GROUP_WIDTH
        y = (_rms(r[...]) * og_ref[:, lo:lo + GROUP_WIDTH]).astype(BF16)
        part = _dot(y, wout_ref[lo:lo + GROUP_WIDTH, :])
        mo = part if mo is None else mo + part
    x1 = x_ref[...] + g1 * mo
    h = (_rms(x1) * n2_ref[...] * (1.0 + sc2) + sh2).astype(BF16)
    d_ff = w1_ref.shape[1]
    fc = 1024
    acc = None
    for c in range(d_ff // fc):
        a = _dot(h, w1_ref[:, c * fc:(c + 1) * fc])
        a = jnp.square(jnp.maximum(a, 0.0)).astype(BF16)
        part = _dot(a, w2_ref[c * fc:(c + 1) * fc, :])
        acc = part if acc is None else acc + part
    x2 = x1 + g2 * acc
    if final:
        x2 = _rms(x2) * fg_ref[...]
    o_ref[...] = x2


def _post(l, final, o_sb, o_sw, o_fx, o_ssm, x, mod, out_gain, norm2_gain, w_out, w1, w2, final_gain):
    b, s, d = x.shape
    tm = ROW_TILE
    grp = pl.BlockSpec((None, tm, GROUP_WIDTH), lambda bi, i: (bi, i, 0))
    xspec = pl.BlockSpec((None, tm, d), lambda bi, i: (bi, i, 0))
    lspec = lambda a: pl.BlockSpec((None,) + a.shape[1:], lambda bi, i: (l, 0, 0),
                                   pipeline_mode=pl.Buffered(1))
    return pl.pallas_call(
        functools.partial(_post_kernel, final=final),
        out_shape=jax.ShapeDtypeStruct((b, s, d), F32),
        grid=(b, s // tm),
        in_specs=[grp, grp, grp, grp, xspec,
                  pl.BlockSpec((None, None, N_ADA, d), lambda bi, i: (l, bi, 0, 0)),
                  lspec(out_gain), lspec(norm2_gain), lspec(w_out), lspec(w1), lspec(w2),
                  pl.BlockSpec(final_gain.shape, lambda bi, i: (0, 0))],
        out_specs=xspec,
        compiler_params=_params("arbitrary", "arbitrary"),
        name="post_mlp",
    )(o_sb, o_sw, o_fx, o_ssm, x, mod, out_gain, norm2_gain, w_out, w1, w2, final_gain)


def _swa_head_perm(a, axis):
    shape = a.shape
    a = a.reshape(shape[:axis] + (4, HEAD_DIM) + shape[axis + 1:])
    a = jnp.take(a, jnp.asarray([0, 2, 1, 3]), axis=axis)
    return a.reshape(shape)


def kernel(x, c, w_ada, b_ada, norm1_gain, norm2_gain, w_in, rel_bias, sinks, forget_bias, lam_re, lam_im, log_dt, ssm_b_re, ssm_b_im, ssm_c_re, ssm_c_im, ssm_d, w_glu, b_glu, out_gain, w_out, w_mlp_in, w_mlp_out, final_gain):
    b, s, d = x.shape
    depth = w_in.shape[0]
    g, p = lam_re.shape[1:]
    h = ssm_b_re.shape[-1]
    n_fox = forget_bias.shape[-1]
    scale = 1.0 / math.sqrt(HEAD_DIM)
    assert s % SSM_TILE == 0 and s % ATT_TILE == 0 and ATT_TILE == ROW_TILE

    gw, kvw = GROUP_WIDTH, N_KV_SWA * HEAD_DIM
    sizes = (gw, gw, gw, gw, kvw, kvw, gw, gw, gw, n_fox, gw)
    offs = np.concatenate([[0], np.cumsum(sizes)])
    col = lambda k: w_in[:, :, offs[k]:offs[k + 1]]
    w_sbq = (col(0) * scale).astype(BF16)
    w_sbv = col(2).astype(BF16)
    w_swq = (_swa_head_perm(col(3), 2) * scale).astype(BF16)
    w_swv = col(5).astype(BF16)
    w_fxq = (col(6) * scale).astype(BF16)
    w_fxv = col(8).astype(BF16)
    w_u = col(10).astype(BF16)
    wkt = jnp.concatenate([col(1), col(7), col(4)], axis=2).transpose(0, 2, 1).astype(BF16)
    wft = jnp.pad(col(9).transpose(0, 2, 1), ((0, 0), (0, 8 - n_fox), (0, 0))).astype(BF16)
    forget_b = jnp.pad(forget_bias.astype(F32), ((0, 0), (0, 8 - n_fox)))[:, :, None]
    in_wts = (w_sbq, w_sbv, w_swq, w_swv, w_fxq, w_fxv, w_u, wkt, wft)

    og = out_gain.astype(F32)
    og = jnp.concatenate([og[:, :gw], _swa_head_perm(og[:, gw:2 * gw], 1), og[:, 2 * gw:]], axis=1)
    wo = jnp.concatenate([w_out[:, :gw], _swa_head_perm(w_out[:, gw:2 * gw], 1), w_out[:, 2 * gw:]],
                         axis=1).astype(BF16)
    og = og[:, None, :]
    n1 = norm1_gain.astype(F32)[:, None, :]
    n2 = norm2_gain.astype(F32)[:, None, :]
    w1 = w_mlp_in.astype(BF16)
    w2 = w_mlp_out.astype(BF16)
    fg = final_gain.astype(F32)[None, :]
    bias = _swa_bias(rel_bias)
    sinks_b = jnp.broadcast_to(sinks.astype(F32)[:, :, None], sinks.shape + (LANES,))

    avec, bbr, bbi = _ssm_prep(lam_re.astype(F32), lam_im.astype(F32), log_dt.astype(F32),
                               ssm_b_re.astype(F32), ssm_b_im.astype(F32))
    eye = jnp.eye(g, dtype=F32)
    blockdiag_b = lambda bb: jnp.einsum('lhgp,gk->lghkp', bb.reshape(depth, h, g, p), eye
                                        ).reshape(depth, g * h, g * p)
    bmat = jnp.concatenate([blockdiag_b(bbr), blockdiag_b(bbi)], axis=2).astype(BF16)
    blockdiag_c = lambda cc: jnp.einsum('lghp,gk->lgpkh', cc.astype(F32), eye
                                        ).reshape(depth, g * p, g * h)
    cmat = jnp.concatenate([blockdiag_c(ssm_c_re), -blockdiag_c(ssm_c_im)], axis=1).astype(BF16)
    d_skip = ssm_d.astype(F32).reshape(depth, 1, g * h)
    wglu = w_glu.astype(BF16)
    bglu = b_glu.astype(F32)[:, None, :]

    c_pad = jnp.pad(c.astype(F32), ((0, 8 - b), (0, 0)))
    mod = _ada(c_pad, w_ada.astype(F32), b_ada.astype(F32))[:, :b].reshape(depth, b, N_ADA, d)

    x = x.astype(F32)
    for l in range(depth):
        (sbq, sbv, swq, swv, fxq, fxv, u, sbkt, fxkt, swkt, ft) = _inproj(l, x, mod, n1, in_wts, forget_b)
        o_sb = _stick_breaking(sbq, sbkt, sbv)
        o_sw = _sliding_window(l, swq, swkt, swv, bias, sinks_b)
        o_fx = _forgetting(fxq, fxkt, fxv, ft)
        o_ssm = _ssm(l, u, avec, bmat, cmat, d_skip, wglu, bglu)
        x = _post(l, l == depth - 1, o_sb, o_sw, o_fx, o_ssm, x, mod, og, n2, wo, w1, w2, fg)
    return x
```

```python
import functools
import math

import numpy as np
import jax
import jax.numpy as jnp
from jax import lax
from jax.experimental import pallas as pl
from jax.experimental.pallas import tpu as pltpu

F32 = jnp.float32
BF16 = jnp.bfloat16

HEAD_DIM = 64
LANES = 128
GROUP_WIDTH = 256
N_KV_SWA = 2
WINDOW = 128
REL_BUCKETS = 32
REL_MAX_DIST = 128
N_ADA = 6
NORM_EPS = 1e-6
SSM_H = 16
SSM_P = 64
SSM_CHUNK = 16

ROW_TILE = 512
SB_TILE = 256
FOX_TILE = 512
SSM_TILE = 1024
VMEM_LIMIT_BYTES = 56 * 1024 * 1024
EXP_UNDERFLOW = -105.0


def _dot(a, b):
    return jnp.dot(a, b, preferred_element_type=F32)


def _dot_nt(a, b):
    return lax.dot_general(a, b, (((1,), (1,)), ((), ())), preferred_element_type=F32)


def _split_bf16(a):
    hi = a.astype(BF16)
    lo = (a - hi.astype(F32)).astype(BF16)
    return hi, lo


def _rms(x):
    return x * lax.rsqrt(jnp.mean(x * x, axis=-1, keepdims=True) + NORM_EPS)


def _params(*sem):
    return pltpu.CompilerParams(dimension_semantics=sem, vmem_limit_bytes=VMEM_LIMIT_BYTES)


def _ada_kernel(c_ref, w_ref, b_ref, o_ref):
    c = c_ref[...]
    ca = c * jax.nn.sigmoid(c)
    a_hi, a_lo = _split_bf16(ca)
    w_hi, w_lo = _split_bf16(w_ref[...])
    o_ref[...] = _dot(a_hi, w_hi) + _dot(a_hi, w_lo) + _dot(a_lo, w_hi) + b_ref[...]


def _ada(c_pad, w_ada, b_ada):
    depth, d, n = w_ada.shape
    tn = 2048
    return pl.pallas_call(
        _ada_kernel,
        out_shape=jax.ShapeDtypeStruct((depth, c_pad.shape[0], n), F32),
        grid=(depth, n // tn),
        in_specs=[
            pl.BlockSpec(c_pad.shape, lambda l, j: (0, 0)),
            pl.BlockSpec((None, d, tn), lambda l, j: (l, 0, j)),
            pl.BlockSpec((None, 1, tn), lambda l, j: (l, 0, j)),
        ],
        out_specs=pl.BlockSpec((None, c_pad.shape[0], tn), lambda l, j: (l, 0, j)),
        compiler_params=_params("arbitrary", "arbitrary"),
        name="ada_mod",
    )(c_pad, w_ada, b_ada.reshape(depth, 1, n))


def _ssm_prep_kernel(lr_ref, li_ref, ldt_ref, br_ref, bi_ref, avec_ref, bbr_ref, bbi_ref):
    lr = lr_ref[...]
    li = li_ref[...]
    dt = jnp.exp(ldt_ref[...])
    mag = jnp.exp(lr * dt)
    ang = li * dt
    ar = mag * jnp.cos(ang)
    ai = mag * jnp.sin(ang)
    den = lr * lr + li * li
    nr = ar - 1.0
    ni = ai
    cr = (nr * lr + ni * li) / den
    ci = (ni * lr - nr * li) / den
    pr, pi_ = ar, ai
    for _ in range(int(math.log2(SSM_CHUNK))):
        pr, pi_ = pr * pr - pi_ * pi_, 2.0 * pr * pi_
    avec_ref[0:1, :] = ar
    avec_ref[1:2, :] = ai
    avec_ref[2:3, :] = pr
    avec_ref[3:4, :] = pi_
    avec_ref[4:8, :] = jnp.zeros((4, ar.shape[1]), F32)
    br = br_ref[...]
    bi = bi_ref[...]
    bbr_ref[...] = cr * br - ci * bi
    bbi_ref[...] = cr * bi + ci * br


def _ssm_prep(lam_re, lam_im, log_dt, b_re, b_im):
    depth, g, p = lam_re.shape
    h = b_re.shape[-1]
    n = g * p
    row = lambda a: a.reshape(depth, 1, n)
    ldt = jnp.broadcast_to(log_dt[:, :, None], (depth, g, p))
    bt = lambda a: a.transpose(0, 3, 1, 2).reshape(depth, h, n)
    vec = pl.BlockSpec((None, 1, n), lambda l: (l, 0, 0))
    mat = pl.BlockSpec((None, h, n), lambda l: (l, 0, 0))
    return pl.pallas_call(
        _ssm_prep_kernel,
        out_shape=(jax.ShapeDtypeStruct((depth, 8, n), F32),
                   jax.ShapeDtypeStruct((depth, h, n), F32),
                   jax.ShapeDtypeStruct((depth, h, n), F32)),
        grid=(depth,),
        in_specs=[vec, vec, vec, mat, mat],
        out_specs=(pl.BlockSpec((None, 8, n), lambda l: (l, 0, 0)), mat, mat),
        compiler_params=_params("arbitrary"),
        name="ssm_prep",
    )(row(lam_re), row(lam_im), row(ldt), bt(b_re), bt(b_im))


def _inproj_kernel(x_ref, mod_ref, gain_ref, w_sbq, w_sbv, w_swq, w_swv, w_fxq, w_fxv, w_u,
                   wkt_ref, wft_ref, fb_ref,
                   sbq_o, sbv_o, swq_o, swv_o, fxq_o, fxv_o, u_o, sbkt_o, fxkt_o, swkt_o, ft_o,
                   carry_ref):
    ts = x_ref.shape[0]
    mod = mod_ref[...]
    sh1 = mod[0:1]
    sc1 = mod[1:2]
    h = (_rms(x_ref[...]) * gain_ref[...] * (1.0 + sc1) + sh1).astype(BF16)
    sbq_o[...] = _dot(h, w_sbq[...]).astype(BF16)
    sbv_o[...] = _dot(h, w_sbv[...]).astype(BF16)
    swq_o[...] = _dot(h, w_swq[...]).astype(BF16)
    swv_o[...] = _dot(h, w_swv[...]).astype(BF16)
    fxq_o[...] = _dot(h, w_fxq[...]).astype(BF16)
    fxv_o[...] = _dot(h, w_fxv[...]).astype(BF16)
    u_o[...] = _dot(h, w_u[...])
    kt = _dot_nt(wkt_ref[...], h)
    row0 = 0
    for o in (sbkt_o, fxkt_o, swkt_o):
        n_sub, width, tile = o.shape
        for c in range(n_sub):
            o[c] = kt[row0:row0 + width, c * tile:(c + 1) * tile].astype(BF16)
        row0 += width
    f = _dot_nt(wft_ref[...], h) + fb_ref[...]
    logf = jnp.minimum(f, 0.0) - jnp.log1p(jnp.exp(-jnp.abs(f)))
    lane = lax.broadcasted_iota(jnp.int32, logf.shape, 1)
    cum = logf
    k = 1
    while k < ts:
        cum = cum + jnp.where(lane >= k, pltpu.roll(cum, k, axis=1), 0.0)
        k *= 2

    @pl.when(pl.program_id(1) == 0)
    def _():
        carry_ref[...] = jnp.zeros_like(carry_ref)

    cum = cum + carry_ref[:, 0:1]
    ft_o[...] = cum
    carry_ref[...] = jnp.broadcast_to(cum[:, ts - 1:ts], carry_ref.shape)


def _inproj(l, x, mod, gain, wts, forget_b):
    b, s, d = x.shape
    ts = ROW_TILE
    nk = s // ts
    (w_sbq, w_sbv, w_swq, w_swv, w_fxq, w_fxv, w_u, wkt, wft) = wts
    row = lambda width: pl.BlockSpec((None, ts, width), lambda bi, i: (bi, i, 0))
    wspec = lambda w: pl.BlockSpec((None,) + w.shape[1:], lambda bi, i: (l, 0, 0))
    ktspec = lambda width, tile: pl.BlockSpec((None, ts // tile, width, tile), lambda bi, i: (bi, i, 0, 0))
    kv_w = N_KV_SWA * HEAD_DIM
    out_shape = (
        jax.ShapeDtypeStruct((b, s, GROUP_WIDTH), BF16),
        jax.ShapeDtypeStruct((b, s, GROUP_WIDTH), BF16),
        jax.ShapeDtypeStruct((b, s, GROUP_WIDTH), BF16),
        jax.ShapeDtypeStruct((b, s, kv_w), BF16),
        jax.ShapeDtypeStruct((b, s, GROUP_WIDTH), BF16),
        jax.ShapeDtypeStruct((b, s, GROUP_WIDTH), BF16),
        jax.ShapeDtypeStruct((b, s, GROUP_WIDTH), F32),
        jax.ShapeDtypeStruct((b, s // SB_TILE, GROUP_WIDTH, SB_TILE), BF16),
        jax.ShapeDtypeStruct((b, s // FOX_TILE, GROUP_WIDTH, FOX_TILE), BF16),
        jax.ShapeDtypeStruct((b, nk, kv_w, ts), BF16),
        jax.ShapeDtypeStruct((b, 8, s), F32),
    )
    out_specs = (row(GROUP_WIDTH), row(GROUP_WIDTH), row(GROUP_WIDTH), row(kv_w), row(GROUP_WIDTH),
                 row(GROUP_WIDTH), row(GROUP_WIDTH), ktspec(GROUP_WIDTH, SB_TILE),
                 ktspec(GROUP_WIDTH, FOX_TILE), ktspec(kv_w, ts),
                 pl.BlockSpec((None, 8, ts), lambda bi, i: (bi, 0, i)))
    return pl.pallas_call(
        _inproj_kernel,
        out_shape=out_shape,
        grid=(b, nk),
        in_specs=[
            pl.BlockSpec((None, ts, d), lambda bi, i: (bi, i, 0)),
            pl.BlockSpec((None, None, N_ADA, d), lambda bi, i: (l, bi, 0, 0)),
            pl.BlockSpec((None, 1, d), lambda bi, i: (l, 0, 0)),
            wspec(w_sbq), wspec(w_sbv), wspec(w_swq), wspec(w_swv), wspec(w_fxq), wspec(w_fxv),
            wspec(w_u), wspec(wkt), wspec(wft),
            pl.BlockSpec((None, 8, 1), lambda bi, i: (l, 0, 0)),
        ],
        out_specs=out_specs,
        scratch_shapes=[pltpu.VMEM((8, LANES), F32)],
        compiler_params=_params("arbitrary", "arbitrary"),
        name="in_proj",
    )(x, mod, gain, w_sbq, w_sbv, w_swq, w_swv, w_fxq, w_fxv, w_u, wkt, wft, forget_b)


def _pair_split(q):
    lane = lax.broadcasted_iota(jnp.int32, q.shape, 1)
    left = lane < HEAD_DIM
    zero = jnp.zeros_like(q)
    return left, (jnp.where(left, q, zero), jnp.where(left, zero, q))


def _sb_kernel(q_ref, kt_ref, v_ref, suf_ref, o_ref):
    t = q_ref.shape[0]
    qi = pl.program_id(2)
    left, qh = _pair_split(q_ref[...])
    row = lax.broadcasted_iota(jnp.int32, (t, t), 0)
    col = lax.broadcasted_iota(jnp.int32, (t, t), 1)
    strict = col < row
    suf = suf_ref[...]

    def tile(ki, carry, diag):
        run, acc = carry
        kt = kt_ref[ki]
        v = v_ref[pl.ds(pl.multiple_of(ki * t, t), t), :]
        new_run, pv = [], []
        for j in range(2):
            z = _dot(qh[j], kt)
            sp = jnp.log1p(jnp.exp(-jnp.abs(z)))
            lsz = jnp.minimum(z, 0.0) - sp
            lk = jnp.minimum(-z, 0.0) - sp
            if diag:
                lk = jnp.where(strict, lk, 0.0)
            hi, lo = _split_bf16(lk)
            within = _dot(hi, suf) + _dot(lo, suf)
            w = jnp.exp(lsz + within + run[j])
            if diag:
                w = jnp.where(strict, w, 0.0)
            pv.append(_dot(w.astype(BF16), v))
            new_run.append(run[j] + jnp.sum(lk, axis=-1, keepdims=True))
        return tuple(new_run), acc + jnp.where(left, pv[0], pv[1])

    def alive(run):
        return jnp.max(jnp.maximum(run[0], run[1])) > EXP_UNDERFLOW

    zero = jnp.zeros((t, 1), F32)
    state = tile(qi, ((zero, zero), jnp.zeros((t, LANES), F32)), True)

    def body(c):
        i, _, st = c
        st = tile(qi - 1 - i, st, False)
        return i + 1, alive(st[0]), st

    _, _, state = lax.while_loop(lambda c: jnp.logical_and(c[0] < qi, c[1]), body,
                                 (jnp.int32(0), alive(state[0]), state))
    o_ref[...] = state[1]


def _fox_kernel(q_ref, kt_ref, v_ref, f_ref, o_ref, bound_ref):
    t = q_ref.shape[0]
    nk = kt_ref.shape[0]
    qi = pl.program_id(2)
    q = q_ref[...]
    left, qh = _pair_split(q)
    row = lax.broadcasted_iota(jnp.int32, (t, t), 0)
    col = lax.broadcasted_iota(jnp.int32, (t, t), 1)
    causal = col <= row

    @pl.when(qi == 0)
    def _():
        def scan(ki, c):
            k = kt_ref[ki].astype(F32)
            k2 = k * k
            f = f_ref[ki]
            out = []
            for j in range(2):
                n2 = jnp.sum(k2[j * HEAD_DIM:(j + 1) * HEAD_DIM], axis=0, keepdims=True)
                kn = jnp.maximum(c[j], jnp.sqrt(jnp.max(n2)))
                nf = jnp.maximum(c[2 + j], jnp.max(-f[j:j + 1, :]))
                bound_ref[j, ki] = kn
                bound_ref[2 + j, ki] = nf
                out.append((kn, nf))
            return out[0][0], out[1][0], out[0][1], out[1][1]

        ninf_s = jnp.float32(-jnp.inf)
        lax.fori_loop(0, nk, scan, (jnp.float32(0.0), jnp.float32(0.0), ninf_s, ninf_s))

    qf = q.astype(F32)
    q2 = qf * qf
    qn = (jnp.sqrt(jnp.sum(jnp.where(left, q2, 0.0), axis=-1, keepdims=True)),
          jnp.sqrt(jnp.sum(jnp.where(left, 0.0, q2), axis=-1, keepdims=True)))

    def tile(ki, carry, diag):
        m, l, acc = carry
        kt = kt_ref[ki]
        v = v_ref[pl.ds(pl.multiple_of(ki * t, t), t), :]
        f = f_ref[ki]
        new_m, new_l, corr, pv = [], [], [], []
        for j in range(2):
            s = _dot(qh[j], kt) - f[j:j + 1, :]
            if diag:
                s = jnp.where(causal, s, -jnp.inf)
            mj = jnp.maximum(m[j], jnp.max(s, axis=-1, keepdims=True))
            cj = jnp.exp(m[j] - mj)
            e = jnp.exp(s - mj)
            new_m.append(mj)
            new_l.append(l[j] * cj + jnp.sum(e, axis=-1, keepdims=True))
            corr.append(cj)
            pv.append(_dot(e.astype(BF16), v))
        acc = acc * jnp.where(left, corr[0], corr[1]) + jnp.where(left, pv[0], pv[1])
        return tuple(new_m), tuple(new_l), acc

    def alive(m, ki):
        kic = jnp.maximum(ki, 0)
        gap = None
        for j in range(2):
            ub = qn[j] * bound_ref[j, kic] + bound_ref[2 + j, kic] - m[j]
            g = jnp.max(ub)
            gap = g if gap is None else jnp.maximum(gap, g)
        return gap > EXP_UNDERFLOW

    ninf = jnp.full((t, 1), -jnp.inf, F32)
    zero = jnp.zeros((t, 1), F32)
    state = tile(qi, ((ninf, ninf), (zero, zero), jnp.zeros((t, LANES), F32)), True)

    def body(c):
        i, _, st = c
        ki = qi - 1 - i
        st = tile(ki, st, False)
        return i + 1, alive(st[0], ki - 1), st

    _, _, state = lax.while_loop(lambda c: jnp.logical_and(c[0] < qi, c[1]), body,
                                 (jnp.int32(0), alive(state[0], qi - 1), state))
    _, l, acc = state
    o_ref[...] = acc / jnp.where(left, l[0], l[1])


def _causal_attention(kernel, name, t, q, kt, v, extra, extra_spec, scratch=()):
    b, s, _ = q.shape
    nk = s // t
    n_pairs = GROUP_WIDTH // LANES
    return pl.pallas_call(
        kernel,
        out_shape=jax.ShapeDtypeStruct((b, s, GROUP_WIDTH), F32),
        grid=(b, n_pairs, nk),
        in_specs=[
            pl.BlockSpec((None, t, LANES), lambda bi, p, i: (bi, i, p)),
            pl.BlockSpec((None, nk, LANES, t), lambda bi, p, i: (bi, 0, p, 0)),
            pl.BlockSpec((None, s, LANES), lambda bi, p, i: (bi, 0, p)),
            extra_spec,
        ],
        out_specs=pl.BlockSpec((None, t, LANES), lambda bi, p, i: (bi, i, p)),
        scratch_shapes=list(scratch),
        compiler_params=_params("arbitrary", "arbitrary", "arbitrary"),
        name=name,
    )(q, kt, v, extra)


def _stick_breaking(q, kt, v):
    t = SB_TILE
    suffix = jnp.asarray(np.tril(np.ones((t, t), np.float32), -1), BF16)
    return _causal_attention(_sb_kernel, "stick_breaking", t, q, kt, v, suffix,
                             pl.BlockSpec((t, t), lambda bi, p, i: (0, 0)))


def _forgetting(q, kt, v, ft):
    b, _, s = ft.shape
    t = FOX_TILE
    nk = s // t
    n_pairs = GROUP_WIDTH // LANES
    f = ft[:, :2 * n_pairs].reshape(b, n_pairs, 2, nk, t).transpose(0, 1, 3, 2, 4)
    return _causal_attention(_fox_kernel, "forgetting", t, q, kt, v, f,
                             pl.BlockSpec((None, None, nk, 2, t), lambda bi, p, i: (bi, p, 0, 0, 0)),
                             scratch=[pltpu.SMEM((4, nk), F32)])


def _swa_kernel(q_ref, ktc_ref, ktp_ref, vc_ref, vp_ref, bias_ref, sink_ref, o_ref):
    r = q_ref.shape[0]
    first = pl.program_id(1) == 0
    lane = lax.broadcasted_iota(jnp.int32, (WINDOW, LANES), 1)
    left = lane < HEAD_DIM
    col = lax.broadcasted_iota(jnp.int32, (WINDOW, 2 * WINDOW), 1)
    for w in range(r // WINDOW):
        lo = w * WINDOW
        if w == 0:
            kt = jnp.concatenate([ktp_ref[0, :, r - WINDOW:], ktc_ref[0, :, :WINDOW]], axis=1)
            v = jnp.concatenate([vp_ref[...], vc_ref[:WINDOW, :]], axis=0)
        else:
            kt = ktc_ref[0, :, lo - WINDOW:lo + WINDOW]
            v = vc_ref[lo - WINDOW:lo + WINDOW, :]
        for g in range(2):
            qg = q_ref[lo:lo + WINDOW, g * LANES:(g + 1) * LANES]
            zero = jnp.zeros_like(qg)
            pv = []
            for kv in range(N_KV_SWA):
                head = 2 * kv + g
                qm = jnp.where(left, qg, zero) if kv == 0 else jnp.where(left, zero, qg)
                z = _dot(qm, kt) + bias_ref[head]
                if w == 0:
                    z = jnp.where(jnp.logical_and(first, col < WINDOW), -jnp.inf, z)
                sink = sink_ref[head:head + 1, 0:1]
                m = jnp.maximum(jnp.max(z, axis=-1, keepdims=True), sink)
                e = jnp.exp(z - m)
                den = jnp.sum(e, axis=-1, keepdims=True) + jnp.exp(sink - m)
                pv.append(_dot((e / den).astype(BF16), v))
            o_ref[lo:lo + WINDOW, g * LANES:(g + 1) * LANES] = jnp.where(left, pv[0], pv[1])


def _sliding_window(l, q, kt, v, bias, sinks):
    b, s, _ = q.shape
    r = ROW_TILE
    kv_w = N_KV_SWA * HEAD_DIM
    wpt = r // WINDOW
    return pl.pallas_call(
        _swa_kernel,
        out_shape=jax.ShapeDtypeStruct((b, s, GROUP_WIDTH), F32),
        grid=(b, s // r),
        in_specs=[
            pl.BlockSpec((None, r, GROUP_WIDTH), lambda bi, i: (bi, i, 0)),
            pl.BlockSpec((None, 1, kv_w, r), lambda bi, i: (bi, i, 0, 0)),
            pl.BlockSpec((None, 1, kv_w, r), lambda bi, i: (bi, jnp.maximum(i - 1, 0), 0, 0)),
            pl.BlockSpec((None, r, kv_w), lambda bi, i: (bi, i, 0)),
            pl.BlockSpec((None, WINDOW, kv_w), lambda bi, i: (bi, jnp.maximum(i * wpt - 1, 0), 0)),
            pl.BlockSpec(bias.shape, lambda bi, i: (0, 0, 0)),
            pl.BlockSpec((None,) + sinks.shape[1:], lambda bi, i: (l, 0, 0)),
        ],
        out_specs=pl.BlockSpec((None, r, GROUP_WIDTH), lambda bi, i: (bi, i, 0)),
        compiler_params=_params("arbitrary", "arbitrary"),
        name="sliding_window",
    )(q, kt, kt, v, v, bias, sinks)


def _swa_bias(rel_bias):
    i = np.arange(WINDOW)[:, None]
    j = np.arange(2 * WINDOW)[None, :]
    dist = WINDOW + i - j
    in_window = (dist >= 0) & (dist < WINDOW)
    d = np.clip(dist, 0, None)
    max_exact = REL_BUCKETS // 2
    safe = np.maximum(d, 1).astype(np.float32)
    large = max_exact + (np.log(safe / max_exact) / math.log(REL_MAX_DIST / max_exact)
                         * (REL_BUCKETS - max_exact)).astype(np.int32)
    large = np.minimum(large, REL_BUCKETS - 1)
    bucket = np.where(d < max_exact, d, large).astype(np.int32)
    bias = rel_bias.astype(F32)[bucket].transpose(2, 0, 1)
    return jnp.where(jnp.asarray(in_window)[None], bias, -jnp.inf)


def _ssm_kernel(u_ref, avec_ref, bmat_ref, cmat_ref, d_ref, wglu_ref, bglu_ref, o_ref,
                xr_ref, xi_ref, er_ref, ei_ref, sr_ref, si_ref, carry_ref):
    n_slab, ts, _ = xr_ref.shape
    n = ts // SSM_CHUNK
    np_ = n_slab * LANES

    @pl.when(pl.program_id(1) == 0)
    def _():
        carry_ref[...] = jnp.zeros_like(carry_ref)

    u = u_ref[...]
    bu = _dot(u.astype(BF16), bmat_ref[...])
    for j in range(n_slab):
        xr_ref[j] = bu[:, j * LANES:(j + 1) * LANES]
        xi_ref[j] = bu[:, np_ + j * LANES:np_ + (j + 1) * LANES]

    def rows(i):
        return pl.ds(i, n, stride=SSM_CHUNK)

    def coef(k, j):
        return avec_ref[k:k + 1, j * LANES:(j + 1) * LANES]

    for j in range(n_slab):
        ar, ai = coef(0, j), coef(1, j)
        pr, pi_ = xr_ref[j, rows(0), :], xi_ref[j, rows(0), :]
        for i in range(1, SSM_CHUNK):
            nr = ar * pr - ai * pi_ + xr_ref[j, rows(i), :]
            ni = ar * pi_ + ai * pr + xi_ref[j, rows(i), :]
            xr_ref[j, rows(i), :] = nr
            xi_ref[j, rows(i), :] = ni
            pr, pi_ = nr, ni
        er_ref[j] = pr
        ei_ref[j] = pi_

    a16 = [(coef(2, j), coef(3, j)) for j in range(n_slab)]

    def chunk(c, s):
        out = []
        for j in range(n_slab):
            sr, si = s[j]
            sr_ref[j, pl.ds(c, 1), :] = sr
            si_ref[j, pl.ds(c, 1), :] = si
            er = er_ref[j, pl.ds(c, 1), :]
            ei = ei_ref[j, pl.ds(c, 1), :]
            a16r, a16i = a16[j]
            out.append((a16r * sr - a16i * si + er, a16r * si + a16i * sr + ei))
        return tuple(out)

    init = tuple((carry_ref[0:1, j * LANES:(j + 1) * LANES], carry_ref[1:2, j * LANES:(j + 1) * LANES])
                 for j in range(n_slab))
    fin = lax.fori_loop(0, n, chunk, init)
    for j in range(n_slab):
        carry_ref[0:1, j * LANES:(j + 1) * LANES] = fin[j][0]
        carry_ref[1:2, j * LANES:(j + 1) * LANES] = fin[j][1]

    for j in range(n_slab):
        ar, ai = coef(0, j), coef(1, j)
        pr, pi_ = sr_ref[j], si_ref[j]
        for i in range(SSM_CHUNK):
            pr, pi_ = ar * pr - ai * pi_, ar * pi_ + ai * pr
            xr_ref[j, rows(i), :] = xr_ref[j, rows(i), :] + pr
            xi_ref[j, rows(i), :] = xi_ref[j, rows(i), :] + pi_

    xr = jnp.concatenate([xr_ref[j].astype(BF16) for j in range(n_slab)], axis=1)
    xi = jnp.concatenate([xi_ref[j].astype(BF16) for j in range(n_slab)], axis=1)
    y = _dot(xr, cmat_ref[0:np_, :]) + _dot(xi, cmat_ref[np_:, :]) + d_ref[...] * u
    y = 0.5 * y * (1.0 + jnp.tanh(math.sqrt(2.0 / math.pi) * (y + 0.044715 * (y * y * y))))
    gate = jax.nn.sigmoid(_dot(y.astype(BF16), wglu_ref[...]) + bglu_ref[...])
    o_ref[...] = y * gate


def _ssm(l, u, avec, bmat, cmat, d_skip, w_glu, b_glu):
    b, s, cw = u.shape
    ts = min(SSM_TILE, s)
    np_ = avec.shape[-1]
    n = ts // SSM_CHUNK
    n_slab = np_ // LANES
    lspec = lambda a: pl.BlockSpec((None,) + a.shape[1:], lambda bi, i: (l, 0, 0))
    return pl.pallas_call(
        _ssm_kernel,
        out_shape=jax.ShapeDtypeStruct((b, s, cw), F32),
        grid=(b, s // ts),
        in_specs=[pl.BlockSpec((None, ts, cw), lambda bi, i: (bi, i, 0)),
                  lspec(avec), lspec(bmat), lspec(cmat), lspec(d_skip), lspec(w_glu), lspec(b_glu)],
        out_specs=pl.BlockSpec((None, ts, cw), lambda bi, i: (bi, i, 0)),
        scratch_shapes=[pltpu.VMEM((n_slab, ts, LANES), F32), pltpu.VMEM((n_slab, ts, LANES), F32),
                        pltpu.VMEM((n_slab, n, LANES), F32), pltpu.VMEM((n_slab, n, LANES), F32),
                        pltpu.VMEM((n_slab, n, LANES), F32), pltpu.VMEM((n_slab, n, LANES), F32),
                        pltpu.VMEM((8, np_), F32)],
        compiler_params=_params("arbitrary", "arbitrary"),
        name="s5_ssm",
    )(u, avec, bmat, cmat, d_skip, w_glu, b_glu)


def _post_kernel(osb_ref, osw_ref, ofx_ref, ossm_ref, x_ref, mod_ref, og_ref, n2_ref, wout_ref,
                 w1_ref, w2_ref, fg_ref, o_ref, *, final):
    mod = mod_ref[...]
    g1, sh2, sc2, g2 = mod[2:3], mod[3:4], mod[4:5], mod[5:6]
    mo = None
    for k, r in enumerate((osb_ref, osw_ref, ofx_ref, ossm_ref)):
        lo = k * GROUP_WIDTH
        y = (_rms(r[...]) * og_ref[:, lo:lo + GROUP_WIDTH]).astype(BF16)
        part = _dot(y, wout_ref[lo:lo + GROUP_WIDTH, :])
        mo = part if mo is None else mo + part
    x1 = x_ref[...] + g1 * mo
    h = (_rms(x1) * n2_ref[...] * (1.0 + sc2) + sh2).astype(BF16)
    d_ff = w1_ref.shape[1]
    fc = 1024
    acc = None
    for c in range(d_ff // fc):
        a = _dot(h, w1_ref[:, c * fc:(c + 1) * fc])
        a = jnp.square(jnp.maximum(a, 0.0)).astype(BF16)
        part = _dot(a, w2_ref[c * fc:(c + 1) * fc, :])
        acc = part if acc is None else acc + part
    x2 = x1 + g2 * acc
    if final:
        x2 = _rms(x2) * fg_ref[...]
    o_ref[...] = x2


def _post(l, final, o_sb, o_sw, o_fx, o_ssm, x, mod, out_gain, norm2_gain, w_out, w1, w2, final_gain):
    b, s, d = x.shape
    tm = ROW_TILE
    grp = pl.BlockSpec((None, tm, GROUP_WIDTH), lambda bi, i: (bi, i, 0))
    xspec = pl.BlockSpec((None, tm, d), lambda bi, i: (bi, i, 0))
    lspec = lambda a: pl.BlockSpec((None,) + a.shape[1:], lambda bi, i: (l, 0, 0),
                                   pipeline_mode=pl.Buffered(1))
    return pl.pallas_call(
        functools.partial(_post_kernel, final=final),
        out_shape=jax.ShapeDtypeStruct((b, s, d), F32),
        grid=(b, s // tm),
        in_specs=[grp, grp, grp, grp, xspec,
                  pl.BlockSpec((None, None, N_ADA, d), lambda bi, i: (l, bi, 0, 0)),
                  lspec(out_gain), lspec(norm2_gain), lspec(w_out), lspec(w1), lspec(w2),
                  pl.BlockSpec(final_gain.shape, lambda bi, i: (0, 0))],
        out_specs=xspec,
        compiler_params=_params("arbitrary", "arbitrary"),
        name="post_mlp",
    )(o_sb, o_sw, o_fx, o_ssm, x, mod, out_gain, norm2_gain, w_out, w1, w2, final_gain)


def _swa_head_perm(a, axis):
    shape = a.shape
    a = a.reshape(shape[:axis] + (4, HEAD_DIM) + shape[axis + 1:])
    a = jnp.take(a, jnp.asarray([0, 2, 1, 3]), axis=axis)
    return a.reshape(shape)


def kernel(x, c, w_ada, b_ada, norm1_gain, norm2_gain, w_in, rel_bias, sinks, forget_bias, lam_re, lam_im, log_dt, ssm_b_re, ssm_b_im, ssm_c_re, ssm_c_im, ssm_d, w_glu, b_glu, out_gain, w_out, w_mlp_in, w_mlp_out, final_gain):
    b, s, d = x.shape
    depth = w_in.shape[0]
    g, p = lam_re.shape[1:]
    h = ssm_b_re.shape[-1]
    n_fox = forget_bias.shape[-1]
    scale = 1.0 / math.sqrt(HEAD_DIM)
    assert s % SSM_TILE == 0 and s % ROW_TILE == 0 and ROW_TILE % SB_TILE == 0 and ROW_TILE % FOX_TILE == 0

    gw, kvw = GROUP_WIDTH, N_KV_SWA * HEAD_DIM
    sizes = (gw, gw, gw, gw, kvw, kvw, gw, gw, gw, n_fox, gw)
    offs = np.concatenate([[0], np.cumsum(sizes)])
    col = lambda k: w_in[:, :, offs[k]:offs[k + 1]]
    w_sbq = (col(0) * scale).astype(BF16)
    w_sbv = col(2).astype(BF16)
    w_swq = (_swa_head_perm(col(3), 2) * scale).astype(BF16)
    w_swv = col(5).astype(BF16)
    w_fxq = (col(6) * scale).astype(BF16)
    w_fxv = col(8).astype(BF16)
    w_u = col(10).astype(BF16)
    wkt = jnp.concatenate([col(1), col(7), col(4)], axis=2).transpose(0, 2, 1).astype(BF16)
    wft = jnp.pad(col(9).transpose(0, 2, 1), ((0, 0), (0, 8 - n_fox), (0, 0))).astype(BF16)
    forget_b = jnp.pad(forget_bias.astype(F32), ((0, 0), (0, 8 - n_fox)))[:, :, None]
    in_wts = (w_sbq, w_sbv, w_swq, w_swv, w_fxq, w_fxv, w_u, wkt, wft)

    og = out_gain.astype(F32)
    og = jnp.concatenate([og[:, :gw], _swa_head_perm(og[:, gw:2 * gw], 1), og[:, 2 * gw:]], axis=1)
    wo = jnp.concatenate([w_out[:, :gw], _swa_head_perm(w_out[:, gw:2 * gw], 1), w_out[:, 2 * gw:]],
                         axis=1).astype(BF16)
    og = og[:, None, :]
    n1 = norm1_gain.astype(F32)[:, None, :]
    n2 = norm2_gain.astype(F32)[:, None, :]
    w1 = w_mlp_in.astype(BF16)
    w2 = w_mlp_out.astype(BF16)
    fg = final_gain.astype(F32)[None, :]
    bias = _swa_bias(rel_bias)
    sinks_b = jnp.broadcast_to(sinks.astype(F32)[:, :, None], sinks.shape + (LANES,))

    avec, bbr, bbi = _ssm_prep(lam_re.astype(F32), lam_im.astype(F32), log_dt.astype(F32),
                               ssm_b_re.astype(F32), ssm_b_im.astype(F32))
    eye = jnp.eye(g, dtype=F32)
    blockdiag_b = lambda bb: jnp.einsum('lhgp,gk->lghkp', bb.reshape(depth, h, g, p), eye
                                        ).reshape(depth, g * h, g * p)
    bmat = jnp.concatenate([blockdiag_b(bbr), blockdiag_b(bbi)], axis=2).astype(BF16)
    blockdiag_c = lambda cc: jnp.einsum('lghp,gk->lgpkh', cc.astype(F32), eye
                                        ).reshape(depth, g * p, g * h)
    cmat = jnp.concatenate([blockdiag_c(ssm_c_re), -blockdiag_c(ssm_c_im)], axis=1).astype(BF16)
    d_skip = ssm_d.astype(F32).reshape(depth, 1, g * h)
    wglu = w_glu.astype(BF16)
    bglu = b_glu.astype(F32)[:, None, :]

    c_pad = jnp.pad(c.astype(F32), ((0, 8 - b), (0, 0)))
    mod = _ada(c_pad, w_ada.astype(F32), b_ada.astype(F32))[:, :b].reshape(depth, b, N_ADA, d)

    x = x.astype(F32)
    for l in range(depth):
        (sbq, sbv, swq, swv, fxq, fxv, u, sbkt, fxkt, swkt, ft) = _inproj(l, x, mod, n1, in_wts, forget_b)
        o_sb = _stick_breaking(sbq, sbkt, sbv)
        o_sw = _sliding_window(l, swq, swkt, swv, bias, sinks_b)
        o_fx = _forgetting(fxq, fxkt, fxv, ft)
        o_ssm = _ssm(l, u, avec, bmat, cmat, d_skip, wglu, bglu)
        x = _post(l, l == depth - 1, o_sb, o_sw, o_fx, o_ssm, x, mod, og, n2, wo, w1, w2, fg)
    return x
```

```python
import functools
import math

import numpy as np
import jax
import jax.numpy as jnp
from jax import lax
from jax.experimental import pallas as pl
from jax.experimental.pallas import tpu as pltpu

F32 = jnp.float32
BF16 = jnp.bfloat16

HEAD_DIM = 64
LANES = 128
GROUP_WIDTH = 256
N_KV_SWA = 2
WINDOW = 128
REL_BUCKETS = 32
REL_MAX_DIST = 128
N_ADA = 6
NORM_EPS = 1e-6
SSM_H = 16
SSM_P = 64
SSM_CHUNK = 16

ROW_TILE = 512
SB_TILE = 256
FOX_TILE = 512
SSM_TILE = 1024
VMEM_LIMIT_BYTES = 56 * 1024 * 1024
EXP_UNDERFLOW = -105.0
FAST_MARGIN = 60.0
FOX_AUG = 8


def _dot(a, b):
    return jnp.dot(a, b, preferred_element_type=F32)


def _dot_nt(a, b):
    return lax.dot_general(a, b, (((1,), (1,)), ((), ())), preferred_element_type=F32)


def _split_bf16(a):
    hi = a.astype(BF16)
    lo = (a - hi.astype(F32)).astype(BF16)
    return hi, lo


def _rms(x):
    return x * lax.rsqrt(jnp.mean(x * x, axis=-1, keepdims=True) + NORM_EPS)


def _params(*sem):
    return pltpu.CompilerParams(dimension_semantics=sem, vmem_limit_bytes=VMEM_LIMIT_BYTES)


def _ada_kernel(c_ref, w_ref, b_ref, o_ref):
    c = c_ref[...]
    ca = c * jax.nn.sigmoid(c)
    a_hi, a_lo = _split_bf16(ca)
    w_hi, w_lo = _split_bf16(w_ref[...])
    o_ref[...] = _dot(a_hi, w_hi) + _dot(a_hi, w_lo) + _dot(a_lo, w_hi) + b_ref[...]


def _ada(c_pad, w_ada, b_ada):
    depth, d, n = w_ada.shape
    tn = 2048
    return pl.pallas_call(
        _ada_kernel,
        out_shape=jax.ShapeDtypeStruct((depth, c_pad.shape[0], n), F32),
        grid=(depth, n // tn),
        in_specs=[
            pl.BlockSpec(c_pad.shape, lambda l, j: (0, 0)),
            pl.BlockSpec((None, d, tn), lambda l, j: (l, 0, j)),
            pl.BlockSpec((None, 1, tn), lambda l, j: (l, 0, j)),
        ],
        out_specs=pl.BlockSpec((None, c_pad.shape[0], tn), lambda l, j: (l, 0, j)),
        compiler_params=_params("arbitrary", "arbitrary"),
        name="ada_mod",
    )(c_pad, w_ada, b_ada.reshape(depth, 1, n))


def _ssm_prep_kernel(lr_ref, li_ref, ldt_ref, br_ref, bi_ref, avec_ref, bbr_ref, bbi_ref):
    lr = lr_ref[...]
    li = li_ref[...]
    dt = jnp.exp(ldt_ref[...])
    mag = jnp.exp(lr * dt)
    ang = li * dt
    ar = mag * jnp.cos(ang)
    ai = mag * jnp.sin(ang)
    den = lr * lr + li * li
    nr = ar - 1.0
    ni = ai
    cr = (nr * lr + ni * li) / den
    ci = (ni * lr - nr * li) / den
    pr, pi_ = ar, ai
    for _ in range(int(math.log2(SSM_CHUNK))):
        pr, pi_ = pr * pr - pi_ * pi_, 2.0 * pr * pi_
    avec_ref[0:1, :] = ar
    avec_ref[1:2, :] = ai
    avec_ref[2:3, :] = pr
    avec_ref[3:4, :] = pi_
    avec_ref[4:8, :] = jnp.zeros((4, ar.shape[1]), F32)
    br = br_ref[...]
    bi = bi_ref[...]
    bbr_ref[...] = cr * br - ci * bi
    bbi_ref[...] = cr * bi + ci * br


def _ssm_prep(lam_re, lam_im, log_dt, b_re, b_im):
    depth, g, p = lam_re.shape
    h = b_re.shape[-1]
    n = g * p
    row = lambda a: a.reshape(depth, 1, n)
    ldt = jnp.broadcast_to(log_dt[:, :, None], (depth, g, p))
    bt = lambda a: a.transpose(0, 3, 1, 2).reshape(depth, h, n)
    vec = pl.BlockSpec((None, 1, n), lambda l: (l, 0, 0))
    mat = pl.BlockSpec((None, h, n), lambda l: (l, 0, 0))
    return pl.pallas_call(
        _ssm_prep_kernel,
        out_shape=(jax.ShapeDtypeStruct((depth, 8, n), F32),
                   jax.ShapeDtypeStruct((depth, h, n), F32),
                   jax.ShapeDtypeStruct((depth, h, n), F32)),
        grid=(depth,),
        in_specs=[vec, vec, vec, mat, mat],
        out_specs=(pl.BlockSpec((None, 8, n), lambda l: (l, 0, 0)), mat, mat),
        compiler_params=_params("arbitrary"),
        name="ssm_prep",
    )(row(lam_re), row(lam_im), row(ldt), bt(b_re), bt(b_im))


def _inproj_kernel(x_ref, mod_ref, gain_ref, w_sbq, w_sbv, w_swq, w_swv, w_fxq, w_fxv, w_u,
                   wkt_ref, wft_ref, fb_ref, sel_ref,
                   sbq_o, sbv_o, swq_o, swv_o, fxq_o, fxv_o, u0_o, u1_o, sbkt_o, fxkt_o, swkt_o,
                   carry_ref):
    ts = x_ref.shape[0]
    mod = mod_ref[...]
    sh1 = mod[0:1]
    sc1 = mod[1:2]
    h = (_rms(x_ref[...]) * gain_ref[...] * (1.0 + sc1) + sh1).astype(BF16)
    sbq_o[...] = _dot(h, w_sbq[...]).astype(BF16)
    sbv_o[...] = _dot(h, w_sbv[...]).astype(BF16)
    swq_o[...] = _dot(h, w_swq[...]).astype(BF16)
    swv_o[...] = _dot(h, w_swv[...]).astype(BF16)
    fxq_o[...] = _dot(h, w_fxq[...]).astype(BF16)
    fxv_o[...] = _dot(h, w_fxv[...]).astype(BF16)
    u = _dot(h, w_u[...])
    u0_o[...] = u[:, :LANES]
    u1_o[...] = u[:, LANES:]
    kt = _dot_nt(wkt_ref[...], h)
    for o, row0 in ((sbkt_o, 0), (swkt_o, 2 * GROUP_WIDTH)):
        n_sub, width, tile = o.shape
        for c in range(n_sub):
            o[c] = kt[row0:row0 + width, c * tile:(c + 1) * tile].astype(BF16)
    n_sub, n_pairs, _, tile = fxkt_o.shape
    for c in range(n_sub):
        for p in range(n_pairs):
            lo = GROUP_WIDTH + p * LANES
            fxkt_o[c, p, 0:LANES, :] = kt[lo:lo + LANES, c * tile:(c + 1) * tile].astype(BF16)
    f = _dot_nt(wft_ref[...], h) + fb_ref[...]
    logf = jnp.minimum(f, 0.0) - jnp.log1p(jnp.exp(-jnp.abs(f)))
    lane = lax.broadcasted_iota(jnp.int32, logf.shape, 1)
    cum = logf
    k = 1
    while k < ts:
        cum = cum + jnp.where(lane >= k, pltpu.roll(cum, k, axis=1), 0.0)
        k *= 2

    @pl.when(pl.program_id(1) == 0)
    def _():
        carry_ref[...] = jnp.zeros_like(carry_ref)

    cum = cum + carry_ref[:, 0:1]
    carry_ref[...] = jnp.broadcast_to(cum[:, ts - 1:ts], carry_ref.shape)
    negf = -cum
    hi = negf.astype(BF16).astype(F32)
    r1 = negf - hi
    mid = r1.astype(BF16).astype(F32)
    lo3 = (r1 - mid).astype(BF16).astype(F32)
    stack = jnp.concatenate([hi, mid, lo3, jnp.zeros_like(hi)], axis=0).astype(BF16)
    for p in range(n_pairs):
        aug = _dot(sel_ref[p], stack).astype(BF16)
        for c in range(n_sub):
            fxkt_o[c, p, LANES:2 * LANES, :] = aug[:, c * tile:(c + 1) * tile]


def _fox_aug_select():
    n_pairs = GROUP_WIDTH // LANES
    sel = np.zeros((n_pairs, LANES, 4 * 8), np.float32)
    for p in range(n_pairs):
        for j in range(2):
            for r in range(3):
                sel[p, FOX_AUG * j + r, 8 * r + 2 * p + j] = 1.0
    return jnp.asarray(sel, BF16)


def _inproj(l, x, mod, gain, wts, forget_b):
    b, s, d = x.shape
    ts = ROW_TILE
    nk = s // ts
    (w_sbq, w_sbv, w_swq, w_swv, w_fxq, w_fxv, w_u, wkt, wft) = wts
    sel = _fox_aug_select()
    row = lambda width: pl.BlockSpec((None, ts, width), lambda bi, i: (bi, i, 0))
    wspec = lambda w: pl.BlockSpec((None,) + w.shape[1:], lambda bi, i: (l, 0, 0))
    ktspec = lambda width, tile: pl.BlockSpec((None, ts // tile, width, tile), lambda bi, i: (bi, i, 0, 0))
    kv_w = N_KV_SWA * HEAD_DIM
    out_shape = (
        jax.ShapeDtypeStruct((b, s, GROUP_WIDTH), BF16),
        jax.ShapeDtypeStruct((b, s, GROUP_WIDTH), BF16),
        jax.ShapeDtypeStruct((b, s, GROUP_WIDTH), BF16),
        jax.ShapeDtypeStruct((b, s, kv_w), BF16),
        jax.ShapeDtypeStruct((b, s, GROUP_WIDTH), BF16),
        jax.ShapeDtypeStruct((b, s, GROUP_WIDTH), BF16),
        jax.ShapeDtypeStruct((b, s, LANES), F32),
        jax.ShapeDtypeStruct((b, s, LANES), F32),
        jax.ShapeDtypeStruct((b, s // SB_TILE, GROUP_WIDTH, SB_TILE), BF16),
        jax.ShapeDtypeStruct((b, s // FOX_TILE, GROUP_WIDTH // LANES, 2 * LANES, FOX_TILE), BF16),
        jax.ShapeDtypeStruct((b, nk, kv_w, ts), BF16),
    )
    out_specs = (row(GROUP_WIDTH), row(GROUP_WIDTH), row(GROUP_WIDTH), row(kv_w), row(GROUP_WIDTH),
                 row(GROUP_WIDTH), row(LANES), row(LANES), ktspec(GROUP_WIDTH, SB_TILE),
                 pl.BlockSpec((None, ts // FOX_TILE, GROUP_WIDTH // LANES, 2 * LANES, FOX_TILE),
                              lambda bi, i: (bi, i, 0, 0, 0)),
                 ktspec(kv_w, ts))
    return pl.pallas_call(
        _inproj_kernel,
        out_shape=out_shape,
        grid=(b, nk),
        in_specs=[
            pl.BlockSpec((None, ts, d), lambda bi, i: (bi, i, 0)),
            pl.BlockSpec((None, None, N_ADA, d), lambda bi, i: (l, bi, 0, 0)),
            pl.BlockSpec((None, 1, d), lambda bi, i: (l, 0, 0)),
            wspec(w_sbq), wspec(w_sbv), wspec(w_swq), wspec(w_swv), wspec(w_fxq), wspec(w_fxv),
            wspec(w_u), wspec(wkt), wspec(wft),
            pl.BlockSpec((None, 8, 1), lambda bi, i: (l, 0, 0)),
            pl.BlockSpec(sel.shape, lambda bi, i: (0, 0, 0)),
        ],
        out_specs=out_specs,
        scratch_shapes=[pltpu.VMEM((8, LANES), F32)],
        compiler_params=_params("arbitrary", "arbitrary"),
        name="in_proj",
    )(x, mod, gain, w_sbq, w_sbv, w_swq, w_swv, w_fxq, w_fxv, w_u, wkt, wft, forget_b, sel)


def _pair_split(q):
    lane = lax.broadcasted_iota(jnp.int32, q.shape, 1)
    left = lane < HEAD_DIM
    zero = jnp.zeros_like(q)
    return left, (jnp.where(left, q, zero), jnp.where(left, zero, q))


def _sb_kernel(q_ref, kt_ref, v_ref, suf_ref, o_ref):
    t = q_ref.shape[0]
    n_pairs = q_ref.shape[1] // LANES
    qi = pl.program_id(1)
    split = [_pair_split(q_ref[:, p * LANES:(p + 1) * LANES]) for p in range(n_pairs)]
    left = split[0][0]
    row = lax.broadcasted_iota(jnp.int32, (t, t), 0)
    col = lax.broadcasted_iota(jnp.int32, (t, t), 1)
    strict = col < row
    suf = suf_ref[...]

    def tile(ki, carry, diag):
        run, acc = carry
        rows = pl.ds(pl.multiple_of(ki * t, t), t)
        heads = [(p, j) for p in range(n_pairs) for j in range(2)]
        z = [_dot(split[p][1][j], kt_ref[ki, p * LANES:(p + 1) * LANES, :]) for p, j in heads]
        lsz, lk = [], []
        for zh in z:
            sp = jnp.log(1.0 + jnp.exp(-jnp.abs(zh)))
            lsz.append(jnp.minimum(zh, 0.0) - sp)
            lkh = jnp.minimum(-zh, 0.0) - sp
            lk.append(jnp.where(strict, lkh, 0.0) if diag else lkh)
        parts = [_split_bf16(lkh) for lkh in lk]
        within = [_dot(hi, suf) + _dot(lo, suf) for hi, lo in parts]
        w = []
        for h in range(len(heads)):
            wh = jnp.exp(lsz[h] + within[h] + run[h])
            w.append((jnp.where(strict, wh, 0.0) if diag else wh).astype(BF16))
        pv = [_dot(w[h], v_ref[rows, p * LANES:(p + 1) * LANES]) for h, (p, j) in enumerate(heads)]
        new_run = tuple(run[h] + jnp.sum(lk[h], axis=-1, keepdims=True) for h in range(len(heads)))
        new_acc = tuple(acc[p] + jnp.where(left, pv[2 * p], pv[2 * p + 1]) for p in range(n_pairs))
        return new_run, new_acc

    def alive(run):
        top = run[0]
        for r in run[1:]:
            top = jnp.maximum(top, r)
        return jnp.max(top) > EXP_UNDERFLOW

    zero = jnp.zeros((t, 1), F32)
    state = tile(qi, ((zero,) * (2 * n_pairs), (jnp.zeros((t, LANES), F32),) * n_pairs), True)

    def body(c):
        i, _, st = c
        st = tile(qi - 1 - i, st, False)
        return i + 1, alive(st[0]), st

    _, _, state = lax.while_loop(lambda c: jnp.logical_and(c[0] < qi, c[1]), body,
                                 (jnp.int32(0), alive(state[0]), state))
    for p in range(n_pairs):
        o_ref[:, p * LANES:(p + 1) * LANES] = state[1][p]


def _fox_kernel(q_ref, kt_ref, v_ref, o_ref, bound_ref):
    t = q_ref.shape[0]
    nk = kt_ref.shape[0]
    qi = pl.program_id(2)
    q = q_ref[...]
    left, qm = _pair_split(q)
    lane = lax.broadcasted_iota(jnp.int32, q.shape, 1)
    qh = tuple(
        jnp.concatenate([qm[j], jnp.where(jnp.logical_and(lane >= FOX_AUG * j, lane < FOX_AUG * j + 3),
                                          1.0, 0.0).astype(BF16)], axis=1)
        for j in range(2))
    row = lax.broadcasted_iota(jnp.int32, (t, t), 0)
    col = lax.broadcasted_iota(jnp.int32, (t, t), 1)
    causal = col <= row

    @pl.when(qi == 0)
    def _():
        def scan(ki, c):
            k = kt_ref[ki].astype(F32)
            k2 = k * k
            out = []
            for j in range(2):
                n2 = jnp.sum(k2[j * HEAD_DIM:(j + 1) * HEAD_DIM], axis=0, keepdims=True)
                lo = LANES + FOX_AUG * j
                negf = jnp.sum(k[lo:lo + FOX_AUG], axis=0, keepdims=True)
                kn = jnp.maximum(c[j], jnp.sqrt(jnp.max(n2)))
                nf = jnp.maximum(c[2 + j], jnp.max(negf))
                bound_ref[j, ki] = kn
                bound_ref[2 + j, ki] = nf
                out.append((kn, nf))
            return out[0][0], out[1][0], out[0][1], out[1][1]

        ninf_s = jnp.float32(-jnp.inf)
        lax.fori_loop(0, nk, scan, (jnp.float32(0.0), jnp.float32(0.0), ninf_s, ninf_s))

    qf = q.astype(F32)
    q2 = qf * qf
    qn = (jnp.sqrt(jnp.sum(jnp.where(left, q2, 0.0), axis=-1, keepdims=True)),
          jnp.sqrt(jnp.sum(jnp.where(left, 0.0, q2), axis=-1, keepdims=True)))

    def tile(ki, carry, mode):
        m, l, acc = carry
        kt = kt_ref[ki]
        v = v_ref[pl.ds(pl.multiple_of(ki * t, t), t), :]
        s = [_dot(qh[j], kt) for j in range(2)]
        if mode == "diag":
            s = [jnp.where(causal, sj, -jnp.inf) for sj in s]
        if mode == "fast":
            new_m = m
        else:
            new_m = tuple(jnp.maximum(m[j], jnp.max(s[j], axis=-1, keepdims=True)) for j in range(2))
        e = [jnp.exp(s[j] - new_m[j]) for j in range(2)]
        pv = [_dot(e[j].astype(BF16), v) for j in range(2)]
        rowsum = [jnp.sum(e[j], axis=-1, keepdims=True) for j in range(2)]
        if mode == "fast":
            new_l = tuple(l[j] + rowsum[j] for j in range(2))
        else:
            corr = [jnp.exp(m[j] - new_m[j]) for j in range(2)]
            new_l = tuple(l[j] * corr[j] + rowsum[j] for j in range(2))
            acc = acc * jnp.where(left, corr[0], corr[1])
        return new_m, new_l, acc + jnp.where(left, pv[0], pv[1])

    def row_gap(m, ki):
        kic = jnp.maximum(ki, 0)
        return tuple(jnp.max(qn[j] * bound_ref[j, kic] - m[j]) for j in range(2))

    def gap(rg, ki):
        kic = jnp.maximum(ki, 0)
        return jnp.maximum(rg[0] + bound_ref[2, kic], rg[1] + bound_ref[3, kic])

    ninf = jnp.full((t, 1), -jnp.inf, F32)
    zero = jnp.zeros((t, 1), F32)
    state = tile(qi, ((ninf, ninf), (zero, zero), jnp.zeros((t, LANES), F32)), "diag")

    def general(c):
        i, _, st = c
        ki = qi - 1 - i
        st = tile(ki, st, "general")
        return i + 1, gap(row_gap(st[0], ki - 1), ki - 1), st

    i0, _, (m, l, acc) = lax.while_loop(
        lambda c: jnp.logical_and(c[0] < qi, c[1] >= FAST_MARGIN), general,
        (jnp.int32(0), gap(row_gap(state[0], qi - 1), qi - 1), state))
    rg = row_gap(m, qi - 1 - i0)

    def fast(c):
        i, _, l, acc = c
        ki = qi - 1 - i
        _, l, acc = tile(ki, (m, l, acc), "fast")
        return i + 1, gap(rg, ki - 1), l, acc

    _, _, l, acc = lax.while_loop(
        lambda c: jnp.logical_and(c[0] < qi, c[1] > EXP_UNDERFLOW), fast,
        (i0, gap(rg, qi - 1 - i0), l, acc))
    o_ref[...] = acc / jnp.where(left, l[0], l[1])


def _stick_breaking(q, kt, v):
    b, s, width = q.shape
    t = SB_TILE
    nk = s // t
    suffix = jnp.asarray(np.tril(np.ones((t, t), np.float32), -1), BF16)
    once = pl.Buffered(1)
    return pl.pallas_call(
        _sb_kernel,
        out_shape=jax.ShapeDtypeStruct((b, s, width), F32),
        grid=(b, nk),
        in_specs=[
            pl.BlockSpec((None, t, width), lambda bi, i: (bi, i, 0)),
            pl.BlockSpec((None, nk, width, t), lambda bi, i: (bi, 0, 0, 0), pipeline_mode=once),
            pl.BlockSpec((None, s, width), lambda bi, i: (bi, 0, 0), pipeline_mode=once),
            pl.BlockSpec((t, t), lambda bi, i: (0, 0), pipeline_mode=once),
        ],
        out_specs=pl.BlockSpec((None, t, width), lambda bi, i: (bi, i, 0)),
        compiler_params=_params("arbitrary", "arbitrary"),
        name="stick_breaking",
    )(q, kt, v, suffix)


def _forgetting(q, kt, v):
    b, s, width = q.shape
    t = FOX_TILE
    nk = s // t
    return pl.pallas_call(
        _fox_kernel,
        out_shape=jax.ShapeDtypeStruct((b, s, width), F32),
        grid=(b, width // LANES, nk),
        in_specs=[
            pl.BlockSpec((None, t, LANES), lambda bi, p, i: (bi, i, p)),
            pl.BlockSpec((None, nk, None, 2 * LANES, t), lambda bi, p, i: (bi, 0, p, 0, 0)),
            pl.BlockSpec((None, s, LANES), lambda bi, p, i: (bi, 0, p)),
        ],
        out_specs=pl.BlockSpec((None, t, LANES), lambda bi, p, i: (bi, i, p)),
        scratch_shapes=[pltpu.SMEM((4, nk), F32)],
        compiler_params=_params("arbitrary", "arbitrary", "arbitrary"),
        name="forgetting",
    )(q, kt, v)


def _swa_kernel(q_ref, ktc_ref, ktp_ref, vc_ref, vp_ref, bias_ref, sink_ref, o_ref):
    r = q_ref.shape[0]
    first = pl.program_id(1) == 0
    lane = lax.broadcasted_iota(jnp.int32, (WINDOW, LANES), 1)
    left = lane < HEAD_DIM
    col = lax.broadcasted_iota(jnp.int32, (WINDOW, 2 * WINDOW), 1)
    for w in range(r // WINDOW):
        lo = w * WINDOW
        if w == 0:
            kt = jnp.concatenate([ktp_ref[0, :, r - WINDOW:], ktc_ref[0, :, :WINDOW]], axis=1)
            v = jnp.concatenate([vp_ref[...], vc_ref[:WINDOW, :]], axis=0)
        else:
            kt = ktc_ref[0, :, lo - WINDOW:lo + WINDOW]
            v = vc_ref[lo - WINDOW:lo + WINDOW, :]
        for g in range(2):
            qg = q_ref[lo:lo + WINDOW, g * LANES:(g + 1) * LANES]
            zero = jnp.zeros_like(qg)
            pv = []
            for kv in range(N_KV_SWA):
                head = 2 * kv + g
                qm = jnp.where(left, qg, zero) if kv == 0 else jnp.where(left, zero, qg)
                z = _dot(qm, kt) + bias_ref[head]
                if w == 0:
                    z = jnp.where(jnp.logical_and(first, col < WINDOW), -jnp.inf, z)
                sink = sink_ref[head:head + 1, 0:1]
                m = jnp.maximum(jnp.max(z, axis=-1, keepdims=True), sink)
                e = jnp.exp(z - m)
                den = jnp.sum(e, axis=-1, keepdims=True) + jnp.exp(sink - m)
                pv.append(_dot((e / den).astype(BF16), v))
            o_ref[lo:lo + WINDOW, g * LANES:(g + 1) * LANES] = jnp.where(left, pv[0], pv[1])


def _sliding_window(l, q, kt, v, bias, sinks):
    b, s, _ = q.shape
    r = ROW_TILE
    kv_w = N_KV_SWA * HEAD_DIM
    wpt = r // WINDOW
    return pl.pallas_call(
        _swa_kernel,
        out_shape=jax.ShapeDtypeStruct((b, s, GROUP_WIDTH), F32),
        grid=(b, s // r),
        in_specs=[
            pl.BlockSpec((None, r, GROUP_WIDTH), lambda bi, i: (bi, i, 0)),
            pl.BlockSpec((None, 1, kv_w, r), lambda bi, i: (bi, i, 0, 0)),
            pl.BlockSpec((None, 1, kv_w, r), lambda bi, i: (bi, jnp.maximum(i - 1, 0), 0, 0)),
            pl.BlockSpec((None, r, kv_w), lambda bi, i: (bi, i, 0)),
            pl.BlockSpec((None, WINDOW, kv_w), lambda bi, i: (bi, jnp.maximum(i * wpt - 1, 0), 0)),
            pl.BlockSpec(bias.shape, lambda bi, i: (0, 0, 0)),
            pl.BlockSpec((None,) + sinks.shape[1:], lambda bi, i: (l, 0, 0)),
        ],
        out_specs=pl.BlockSpec((None, r, GROUP_WIDTH), lambda bi, i: (bi, i, 0)),
        compiler_params=_params("arbitrary", "arbitrary"),
        name="sliding_window",
    )(q, kt, kt, v, v, bias, sinks)


def _swa_bias(rel_bias):
    i = np.arange(WINDOW)[:, None]
    j = np.arange(2 * WINDOW)[None, :]
    dist = WINDOW + i - j
    in_window = (dist >= 0) & (dist < WINDOW)
    d = np.clip(dist, 0, None)
    max_exact = REL_BUCKETS // 2
    safe = np.maximum(d, 1).astype(np.float32)
    large = max_exact + (np.log(safe / max_exact) / math.log(REL_MAX_DIST / max_exact)
                         * (REL_BUCKETS - max_exact)).astype(np.int32)
    large = np.minimum(large, REL_BUCKETS - 1)
    bucket = np.where(d < max_exact, d, large).astype(np.int32)
    onehot = jnp.asarray(np.eye(REL_BUCKETS, dtype=np.float32)[bucket.reshape(-1)])
    bias = jnp.dot(onehot, rel_bias.astype(F32), precision=lax.Precision.HIGHEST)
    bias = bias.T.reshape(-1, WINDOW, 2 * WINDOW)
    return jnp.where(jnp.asarray(in_window)[None], bias, -jnp.inf)


def _ssm_kernel(u0_ref, u1_ref, avec_ref, bmat_ref, cmat_ref, d_ref, wglu_ref, bglu_ref,
                o0_ref, o1_ref, up_ref, xr_ref, xi_ref, er_ref, ei_ref, sr_ref, si_ref, carry_ref):
    n_slab, ts, _ = xr_ref.shape
    n = ts // SSM_CHUNK
    np_ = n_slab * LANES

    @pl.when(pl.program_id(1) == 0)
    def _():
        carry_ref[...] = jnp.zeros_like(carry_ref)

    def rows(i):
        return pl.ds(i * n, n)

    for k, u_ref in enumerate((u0_ref, u1_ref)):
        for i in range(SSM_CHUNK):
            up_ref[rows(i), k * LANES:(k + 1) * LANES] = u_ref[pl.ds(i, n, stride=SSM_CHUNK), :]
    u = up_ref[...]
    bu = _dot(u.astype(BF16), bmat_ref[...])
    for j in range(n_slab):
        xr_ref[j] = bu[:, j * LANES:(j + 1) * LANES]
        xi_ref[j] = bu[:, np_ + j * LANES:np_ + (j + 1) * LANES]

    def coef(k, j):
        return avec_ref[k:k + 1, j * LANES:(j + 1) * LANES]

    for j in range(n_slab):
        ar, ai = coef(0, j), coef(1, j)
        pr, pi_ = xr_ref[j, rows(0), :], xi_ref[j, rows(0), :]
        for i in range(1, SSM_CHUNK):
            nr = ar * pr - ai * pi_ + xr_ref[j, rows(i), :]
            ni = ar * pi_ + ai * pr + xi_ref[j, rows(i), :]
            xr_ref[j, rows(i), :] = nr
            xi_ref[j, rows(i), :] = ni
            pr, pi_ = nr, ni
        er_ref[j] = pr
        ei_ref[j] = pi_

    a16 = [(coef(2, j), coef(3, j)) for j in range(n_slab)]

    def chunk(c, s):
        out = []
        for j in range(n_slab):
            sr, si = s[j]
            sr_ref[j, pl.ds(c, 1), :] = sr
            si_ref[j, pl.ds(c, 1), :] = si
            er = er_ref[j, pl.ds(c, 1), :]
            ei = ei_ref[j, pl.ds(c, 1), :]
            a16r, a16i = a16[j]
            out.append((a16r * sr - a16i * si + er, a16r * si + a16i * sr + ei))
        return tuple(out)

    init = tuple((carry_ref[0:1, j * LANES:(j + 1) * LANES], carry_ref[1:2, j * LANES:(j + 1) * LANES])
                 for j in range(n_slab))
    fin = lax.fori_loop(0, n, chunk, init)
    for j in range(n_slab):
        carry_ref[0:1, j * LANES:(j + 1) * LANES] = fin[j][0]
        carry_ref[1:2, j * LANES:(j + 1) * LANES] = fin[j][1]

    for j in range(n_slab):
        ar, ai = coef(0, j), coef(1, j)
        pr, pi_ = sr_ref[j], si_ref[j]
        for i in range(SSM_CHUNK):
            pr, pi_ = ar * pr - ai * pi_, ar * pi_ + ai * pr
            xr_ref[j, rows(i), :] = xr_ref[j, rows(i), :] + pr
            xi_ref[j, rows(i), :] = xi_ref[j, rows(i), :] + pi_

    xr = jnp.concatenate([xr_ref[j].astype(BF16) for j in range(n_slab)], axis=1)
    xi = jnp.concatenate([xi_ref[j].astype(BF16) for j in range(n_slab)], axis=1)
    y = _dot(xr, cmat_ref[0:np_, :]) + _dot(xi, cmat_ref[np_:, :]) + d_ref[...] * u
    y = 0.5 * y * (1.0 + jnp.tanh(math.sqrt(2.0 / math.pi) * (y + 0.044715 * (y * y * y))))
    gate = jax.nn.sigmoid(_dot(y.astype(BF16), wglu_ref[...]) + bglu_ref[...])
    out = y * gate
    for k, o_ref in enumerate((o0_ref, o1_ref)):
        for i in range(SSM_CHUNK):
            o_ref[pl.ds(i, n, stride=SSM_CHUNK), :] = out[i * n:(i + 1) * n, k * LANES:(k + 1) * LANES]


def _ssm(l, u0, u1, avec, bmat, cmat, d_skip, w_glu, b_glu):
    b, s, hw = u0.shape
    ts = min(SSM_TILE, s)
    np_ = avec.shape[-1]
    n = ts // SSM_CHUNK
    n_slab = np_ // LANES
    lspec = lambda a: pl.BlockSpec((None,) + a.shape[1:], lambda bi, i: (l, 0, 0))
    half = pl.BlockSpec((None, ts, hw), lambda bi, i: (bi, i, 0))
    return pl.pallas_call(
        _ssm_kernel,
        out_shape=(jax.ShapeDtypeStruct((b, s, hw), F32), jax.ShapeDtypeStruct((b, s, hw), F32)),
        grid=(b, s // ts),
        in_specs=[half, half,
                  lspec(avec), lspec(bmat), lspec(cmat), lspec(d_skip), lspec(w_glu), lspec(b_glu)],
        out_specs=(half, half),
        scratch_shapes=[pltpu.VMEM((ts, 2 * hw), F32),
                        pltpu.VMEM((n_slab, ts, LANES), F32), pltpu.VMEM((n_slab, ts, LANES), F32),
                        pltpu.VMEM((n_slab, n, LANES), F32), pltpu.VMEM((n_slab, n, LANES), F32),
                        pltpu.VMEM((n_slab, n, LANES), F32), pltpu.VMEM((n_slab, n, LANES), F32),
                        pltpu.VMEM((8, np_), F32)],
        compiler_params=_params("arbitrary", "arbitrary"),
        name="s5_ssm",
    )(u0, u1, avec, bmat, cmat, d_skip, w_glu, b_glu)


def _post_kernel(osb_ref, osw_ref, ofx_ref, ossm0_ref, ossm1_ref, x_ref, mod_ref, og_ref, n2_ref,
                 wout_ref, w1_ref, w2_ref, fg_ref, o_ref, *, final):
    mod = mod_ref[...]
    g1, sh2, sc2, g2 = mod[2:3], mod[3:4], mod[4:5], mod[5:6]
    ssm = jnp.concatenate([ossm0_ref[...], ossm1_ref[...]], axis=1)
    mo = None
    for k, o in enumerate((osb_ref[...], osw_ref[...], ofx_ref[...], ssm)):
        lo = k * GROUP_WIDTH
        y = (_rms(o) * og_ref[:, lo:lo + GROUP_WIDTH]).astype(BF16)
        part = _dot(y, wout_ref[lo:lo + GROUP_WIDTH, :])
        mo = part if mo is None else mo + part
    x1 = x_ref[...] + g1 * mo
    h = (_rms(x1) * n2_ref[...] * (1.0 + sc2) + sh2).astype(BF16)
    d_ff = w1_ref.shape[1]
    fc = 1024
    acc = None
    for c in range(d_ff // fc):
        a = _dot(h, w1_ref[:, c * fc:(c + 1) * fc])
        a = jnp.square(jnp.maximum(a, 0.0)).astype(BF16)
        part = _dot(a, w2_ref[c * fc:(c + 1) * fc, :])
        acc = part if acc is None else acc + part
    x2 = x1 + g2 * acc
    if final:
        x2 = _rms(x2) * fg_ref[...]
    o_ref[...] = x2


def _post(l, final, o_sb, o_sw, o_fx, o_ssm0, o_ssm1, x, mod, out_gain, norm2_gain, w_out, w1, w2,
          final_gain):
    b, s, d = x.shape
    tm = ROW_TILE
    grp = pl.BlockSpec((None, tm, GROUP_WIDTH), lambda bi, i: (bi, i, 0))
    half = pl.BlockSpec((None, tm, LANES), lambda bi, i: (bi, i, 0))
    xspec = pl.BlockSpec((None, tm, d), lambda bi, i: (bi, i, 0))
    lspec = lambda a: pl.BlockSpec((None,) + a.shape[1:], lambda bi, i: (l, 0, 0),
                                   pipeline_mode=pl.Buffered(1))
    return pl.pallas_call(
        functools.partial(_post_kernel, final=final),
        out_shape=jax.ShapeDtypeStruct((b, s, d), F32),
        grid=(b, s // tm),
        in_specs=[grp, grp, grp, half, half, xspec,
                  pl.BlockSpec((None, None, N_ADA, d), lambda bi, i: (l, bi, 0, 0)),
                  lspec(out_gain), lspec(norm2_gain), lspec(w_out), lspec(w1), lspec(w2),
                  pl.BlockSpec(final_gain.shape, lambda bi, i: (0, 0))],
        out_specs=xspec,
        compiler_params=_params("arbitrary", "arbitrary"),
        name="post_mlp",
    )(o_sb, o_sw, o_fx, o_ssm0, o_ssm1, x, mod, out_gain, norm2_gain, w_out, w1, w2, final_gain)


def _swa_head_perm(a, axis):
    shape = a.shape
    a = a.reshape(shape[:axis] + (4, HEAD_DIM) + shape[axis + 1:])
    a = jnp.take(a, jnp.asarray([0, 2, 1, 3]), axis=axis)
    return a.reshape(shape)


def kernel(x, c, w_ada, b_ada, norm1_gain, norm2_gain, w_in, rel_bias, sinks, forget_bias, lam_re, lam_im, log_dt, ssm_b_re, ssm_b_im, ssm_c_re, ssm_c_im, ssm_d, w_glu, b_glu, out_gain, w_out, w_mlp_in, w_mlp_out, final_gain):
    b, s, d = x.shape
    depth = w_in.shape[0]
    g, p = lam_re.shape[1:]
    h = ssm_b_re.shape[-1]
    n_fox = forget_bias.shape[-1]
    scale = 1.0 / math.sqrt(HEAD_DIM)
    assert s % SSM_TILE == 0 and s % ROW_TILE == 0 and ROW_TILE % SB_TILE == 0 and ROW_TILE % FOX_TILE == 0

    gw, kvw = GROUP_WIDTH, N_KV_SWA * HEAD_DIM
    sizes = (gw, gw, gw, gw, kvw, kvw, gw, gw, gw, n_fox, gw)
    offs = np.concatenate([[0], np.cumsum(sizes)])
    col = lambda k: w_in[:, :, offs[k]:offs[k + 1]]
    w_sbq = (col(0) * scale).astype(BF16)
    w_sbv = col(2).astype(BF16)
    w_swq = (_swa_head_perm(col(3), 2) * scale).astype(BF16)
    w_swv = col(5).astype(BF16)
    w_fxq = (col(6) * scale).astype(BF16)
    w_fxv = col(8).astype(BF16)
    w_u = col(10).astype(BF16)
    wkt = jnp.concatenate([col(1), col(7), col(4)], axis=2).transpose(0, 2, 1).astype(BF16)
    wft = jnp.pad(col(9).transpose(0, 2, 1), ((0, 0), (0, 8 - n_fox), (0, 0))).astype(BF16)
    forget_b = jnp.pad(forget_bias.astype(F32), ((0, 0), (0, 8 - n_fox)))[:, :, None]
    in_wts = (w_sbq, w_sbv, w_swq, w_swv, w_fxq, w_fxv, w_u, wkt, wft)

    og = out_gain.astype(F32)
    og = jnp.concatenate([og[:, :gw], _swa_head_perm(og[:, gw:2 * gw], 1), og[:, 2 * gw:]], axis=1)
    wo = jnp.concatenate([w_out[:, :gw], _swa_head_perm(w_out[:, gw:2 * gw], 1), w_out[:, 2 * gw:]],
                         axis=1).astype(BF16)
    og = og[:, None, :]
    n1 = norm1_gain.astype(F32)[:, None, :]
    n2 = norm2_gain.astype(F32)[:, None, :]
    w1 = w_mlp_in.astype(BF16)
    w2 = w_mlp_out.astype(BF16)
    fg = final_gain.astype(F32)[None, :]
    bias = _swa_bias(rel_bias)
    sinks_b = jnp.broadcast_to(sinks.astype(F32)[:, :, None], sinks.shape + (LANES,))

    avec, bbr, bbi = _ssm_prep(lam_re.astype(F32), lam_im.astype(F32), log_dt.astype(F32),
                               ssm_b_re.astype(F32), ssm_b_im.astype(F32))
    eye = jnp.eye(g, dtype=F32)
    blockdiag_b = lambda bb: jnp.einsum('lhgp,gk->lghkp', bb.reshape(depth, h, g, p), eye
                                        ).reshape(depth, g * h, g * p)
    bmat = jnp.concatenate([blockdiag_b(bbr), blockdiag_b(bbi)], axis=2).astype(BF16)
    blockdiag_c = lambda cc: jnp.einsum('lghp,gk->lgpkh', cc.astype(F32), eye
                                        ).reshape(depth, g * p, g * h)
    cmat = jnp.concatenate([blockdiag_c(ssm_c_re), -blockdiag_c(ssm_c_im)], axis=1).astype(BF16)
    d_skip = ssm_d.astype(F32).reshape(depth, 1, g * h)
    wglu = w_glu.astype(BF16)
    bglu = b_glu.astype(F32)[:, None, :]

    c_pad = jnp.pad(c.astype(F32), ((0, 8 - b), (0, 0)))
    mod = _ada(c_pad, w_ada.astype(F32), b_ada.astype(F32))[:, :b].reshape(depth, b, N_ADA, d)

    x = x.astype(F32)
    for l in range(depth):
        (sbq, sbv, swq, swv, fxq, fxv, u0, u1, sbkt, fxkt, swkt) = _inproj(l, x, mod, n1, in_wts, forget_b)
        o_sb = _stick_breaking(sbq, sbkt, sbv)
        o_sw = _sliding_window(l, swq, swkt, swv, bias, sinks_b)
        o_fx = _forgetting(fxq, fxkt, fxv)
        o_ssm0, o_ssm1 = _ssm(l, u0, u1, avec, bmat, cmat, d_skip, wglu, bglu)
        x = _post(l, l == depth - 1, o_sb, o_sw, o_fx, o_ssm0, o_ssm1, x, mod, og, n2, wo, w1, w2, fg)
    return x
```

```python
import functools
import math

import numpy as np
import jax
import jax.numpy as jnp
from jax import lax
from jax.experimental import pallas as pl
from jax.experimental.pallas import tpu as pltpu

F32 = jnp.float32
BF16 = jnp.bfloat16

HEAD_DIM = 64
LANES = 128
GROUP_WIDTH = 256
N_KV_SWA = 2
WINDOW = 128
REL_BUCKETS = 32
REL_MAX_DIST = 128
N_ADA = 6
NORM_EPS = 1e-6
SSM_H = 16
SSM_P = 64
SSM_CHUNK = 16

ROW_TILE = 512
SB_TILE = 256
FOX_TILE = 512
SSM_TILE = 1024
VMEM_LIMIT_BYTES = 56 * 1024 * 1024
EXP_UNDERFLOW = -105.0
FAST_MARGIN = 60.0
FOX_AUG = 8


def _dot(a, b):
    return jnp.dot(a, b, preferred_element_type=F32)


def _dot_nt(a, b):
    return lax.dot_general(a, b, (((1,), (1,)), ((), ())), preferred_element_type=F32)


def _split_bf16(a):
    hi = a.astype(BF16)
    lo = (a - hi.astype(F32)).astype(BF16)
    return hi, lo


def _rms(x):
    return x * lax.rsqrt(jnp.mean(x * x, axis=-1, keepdims=True) + NORM_EPS)


def _params(*sem):
    return pltpu.CompilerParams(dimension_semantics=sem, vmem_limit_bytes=VMEM_LIMIT_BYTES)


def _ada_kernel(c_ref, w_ref, b_ref, o_ref):
    c = c_ref[...]
    ca = c * jax.nn.sigmoid(c)
    a_hi, a_lo = _split_bf16(ca)
    w_hi, w_lo = _split_bf16(w_ref[...])
    o_ref[...] = _dot(a_hi, w_hi) + _dot(a_hi, w_lo) + _dot(a_lo, w_hi) + b_ref[...]


def _ada(c_pad, w_ada, b_ada):
    depth, d, n = w_ada.shape
    tn = 2048
    return pl.pallas_call(
        _ada_kernel,
        out_shape=jax.ShapeDtypeStruct((depth, c_pad.shape[0], n), F32),
        grid=(depth, n // tn),
        in_specs=[
            pl.BlockSpec(c_pad.shape, lambda l, j: (0, 0)),
            pl.BlockSpec((None, d, tn), lambda l, j: (l, 0, j)),
            pl.BlockSpec((None, 1, tn), lambda l, j: (l, 0, j)),
        ],
        out_specs=pl.BlockSpec((None, c_pad.shape[0], tn), lambda l, j: (l, 0, j)),
        compiler_params=_params("arbitrary", "arbitrary"),
        name="ada_mod",
    )(c_pad, w_ada, b_ada.reshape(depth, 1, n))


def _ssm_prep_kernel(lr_ref, li_ref, ldt_ref, br_ref, bi_ref, avec_ref, bbr_ref, bbi_ref):
    lr = lr_ref[...]
    li = li_ref[...]
    dt = jnp.exp(ldt_ref[...])
    mag = jnp.exp(lr * dt)
    ang = li * dt
    ar = mag * jnp.cos(ang)
    ai = mag * jnp.sin(ang)
    den = lr * lr + li * li
    nr = ar - 1.0
    ni = ai
    cr = (nr * lr + ni * li) / den
    ci = (ni * lr - nr * li) / den
    pr, pi_ = ar, ai
    for _ in range(int(math.log2(SSM_CHUNK))):
        pr, pi_ = pr * pr - pi_ * pi_, 2.0 * pr * pi_
    avec_ref[0:1, :] = ar
    avec_ref[1:2, :] = ai
    avec_ref[2:3, :] = pr
    avec_ref[3:4, :] = pi_
    avec_ref[4:8, :] = jnp.zeros((4, ar.shape[1]), F32)
    br = br_ref[...]
    bi = bi_ref[...]
    bbr_ref[...] = cr * br - ci * bi
    bbi_ref[...] = cr * bi + ci * br


def _ssm_prep(lam_re, lam_im, log_dt, b_re, b_im):
    depth, g, p = lam_re.shape
    h = b_re.shape[-1]
    n = g * p
    row = lambda a: a.reshape(depth, 1, n)
    ldt = jnp.broadcast_to(log_dt[:, :, None], (depth, g, p))
    bt = lambda a: a.transpose(0, 3, 1, 2).reshape(depth, h, n)
    vec = pl.BlockSpec((None, 1, n), lambda l: (l, 0, 0))
    mat = pl.BlockSpec((None, h, n), lambda l: (l, 0, 0))
    return pl.pallas_call(
        _ssm_prep_kernel,
        out_shape=(jax.ShapeDtypeStruct((depth, 8, n), F32),
                   jax.ShapeDtypeStruct((depth, h, n), F32),
                   jax.ShapeDtypeStruct((depth, h, n), F32)),
        grid=(depth,),
        in_specs=[vec, vec, vec, mat, mat],
        out_specs=(pl.BlockSpec((None, 8, n), lambda l: (l, 0, 0)), mat, mat),
        compiler_params=_params("arbitrary"),
        name="ssm_prep",
    )(row(lam_re), row(lam_im), row(ldt), bt(b_re), bt(b_im))


def _inproj_kernel(x_ref, mod_ref, gain_ref, w_sbq, w_sbv, w_swq, w_swv, w_fxq, w_fxv, w_u,
                   wkt_ref, wft_ref, fb_ref, sel_ref,
                   sbq_o, sbv_o, swq_o, swv_o, fxq_o, fxv_o, u0_o, u1_o, sbkt_o, fxkt_o, swkt_o,
                   carry_ref):
    ts = x_ref.shape[0]
    mod = mod_ref[...]
    sh1 = mod[0:1]
    sc1 = mod[1:2]
    h = (_rms(x_ref[...]) * gain_ref[...] * (1.0 + sc1) + sh1).astype(BF16)
    sbq_o[...] = _dot(h, w_sbq[...]).astype(BF16)
    sbv_o[...] = _dot(h, w_sbv[...]).astype(BF16)
    swq_o[...] = _dot(h, w_swq[...]).astype(BF16)
    swv_o[...] = _dot(h, w_swv[...]).astype(BF16)
    fxq_o[...] = _dot(h, w_fxq[...]).astype(BF16)
    fxv_o[...] = _dot(h, w_fxv[...]).astype(BF16)
    u = _dot(h, w_u[...])
    u0_o[...] = u[:, :LANES]
    u1_o[...] = u[:, LANES:]
    kt = _dot_nt(wkt_ref[...], h)
    for o, row0 in ((sbkt_o, 0), (swkt_o, 2 * GROUP_WIDTH)):
        n_sub, width, tile = o.shape
        for c in range(n_sub):
            o[c] = kt[row0:row0 + width, c * tile:(c + 1) * tile].astype(BF16)
    n_sub, n_pairs, _, tile = fxkt_o.shape
    for c in range(n_sub):
        for p in range(n_pairs):
            lo = GROUP_WIDTH + p * LANES
            fxkt_o[c, p, 0:LANES, :] = kt[lo:lo + LANES, c * tile:(c + 1) * tile].astype(BF16)
    f = _dot_nt(wft_ref[...], h) + fb_ref[...]
    logf = jnp.minimum(f, 0.0) - jnp.log1p(jnp.exp(-jnp.abs(f)))
    lane = lax.broadcasted_iota(jnp.int32, logf.shape, 1)
    cum = logf
    k = 1
    while k < ts:
        cum = cum + jnp.where(lane >= k, pltpu.roll(cum, k, axis=1), 0.0)
        k *= 2

    @pl.when(pl.program_id(1) == 0)
    def _():
        carry_ref[...] = jnp.zeros_like(carry_ref)

    cum = cum + carry_ref[:, 0:1]
    carry_ref[...] = jnp.broadcast_to(cum[:, ts - 1:ts], carry_ref.shape)
    negf = -cum
    hi = negf.astype(BF16).astype(F32)
    r1 = negf - hi
    mid = r1.astype(BF16).astype(F32)
    lo3 = (r1 - mid).astype(BF16).astype(F32)
    stack = jnp.concatenate([hi, mid, lo3, jnp.zeros_like(hi)], axis=0).astype(BF16)
    for p in range(n_pairs):
        aug = _dot(sel_ref[p], stack).astype(BF16)
        for c in range(n_sub):
            fxkt_o[c, p, LANES:2 * LANES, :] = aug[:, c * tile:(c + 1) * tile]


def _fox_aug_select():
    n_pairs = GROUP_WIDTH // LANES
    sel = np.zeros((n_pairs, LANES, 4 * 8), np.float32)
    for p in range(n_pairs):
        for j in range(2):
            for r in range(3):
                sel[p, FOX_AUG * j + r, 8 * r + 2 * p + j] = 1.0
    return jnp.asarray(sel, BF16)


def _inproj(l, x, mod, gain, wts, forget_b):
    b, s, d = x.shape
    ts = ROW_TILE
    nk = s // ts
    (w_sbq, w_sbv, w_swq, w_swv, w_fxq, w_fxv, w_u, wkt, wft) = wts
    sel = _fox_aug_select()
    row = lambda width: pl.BlockSpec((None, ts, width), lambda bi, i: (bi, i, 0))
    wspec = lambda w: pl.BlockSpec((None,) + w.shape[1:], lambda bi, i: (l, 0, 0))
    ktspec = lambda width, tile: pl.BlockSpec((None, ts // tile, width, tile), lambda bi, i: (bi, i, 0, 0))
    kv_w = N_KV_SWA * HEAD_DIM
    out_shape = (
        jax.ShapeDtypeStruct((b, s, GROUP_WIDTH), BF16),
        jax.ShapeDtypeStruct((b, s, GROUP_WIDTH), BF16),
        jax.ShapeDtypeStruct((b, s, GROUP_WIDTH), BF16),
        jax.ShapeDtypeStruct((b, s, kv_w), BF16),
        jax.ShapeDtypeStruct((b, s, GROUP_WIDTH), BF16),
        jax.ShapeDtypeStruct((b, s, GROUP_WIDTH), BF16),
        jax.ShapeDtypeStruct((b, s, LANES), F32),
        jax.ShapeDtypeStruct((b, s, LANES), F32),
        jax.ShapeDtypeStruct((b, s // SB_TILE, GROUP_WIDTH, SB_TILE), BF16),
        jax.ShapeDtypeStruct((b, s // FOX_TILE, GROUP_WIDTH // LANES, 2 * LANES, FOX_TILE), BF16),
        jax.ShapeDtypeStruct((b, nk, kv_w, ts), BF16),
    )
    out_specs = (row(GROUP_WIDTH), row(GROUP_WIDTH), row(GROUP_WIDTH), row(kv_w), row(GROUP_WIDTH),
                 row(GROUP_WIDTH), row(LANES), row(LANES), ktspec(GROUP_WIDTH, SB_TILE),
                 pl.BlockSpec((None, ts // FOX_TILE, GROUP_WIDTH // LANES, 2 * LANES, FOX_TILE),
                              lambda bi, i: (bi, i, 0, 0, 0)),
                 ktspec(kv_w, ts))
    return pl.pallas_call(
        _inproj_kernel,
        out_shape=out_shape,
        grid=(b, nk),
        in_specs=[
            pl.BlockSpec((None, ts, d), lambda bi, i: (bi, i, 0)),
            pl.BlockSpec((None, None, N_ADA, d), lambda bi, i: (l, bi, 0, 0)),
            pl.BlockSpec((None, 1, d), lambda bi, i: (l, 0, 0)),
            wspec(w_sbq), wspec(w_sbv), wspec(w_swq), wspec(w_swv), wspec(w_fxq), wspec(w_fxv),
            wspec(w_u), wspec(wkt), wspec(wft),
            pl.BlockSpec((None, 8, 1), lambda bi, i: (l, 0, 0)),
            pl.BlockSpec(sel.shape, lambda bi, i: (0, 0, 0)),
        ],
        out_specs=out_specs,
        scratch_shapes=[pltpu.VMEM((8, LANES), F32)],
        compiler_params=_params("arbitrary", "arbitrary"),
        name="in_proj",
    )(x, mod, gain, w_sbq, w_sbv, w_swq, w_swv, w_fxq, w_fxv, w_u, wkt, wft, forget_b, sel)


def _pair_split(q):
    lane = lax.broadcasted_iota(jnp.int32, q.shape, 1)
    left = lane < HEAD_DIM
    zero = jnp.zeros_like(q)
    return left, (jnp.where(left, q, zero), jnp.where(left, zero, q))


def _sb_kernel(q_ref, kt_ref, v_ref, suf_ref, o_ref):
    t = q_ref.shape[0]
    n_pairs = q_ref.shape[1] // LANES
    qi = pl.program_id(1)
    split = [_pair_split(q_ref[:, p * LANES:(p + 1) * LANES]) for p in range(n_pairs)]
    left = split[0][0]
    row = lax.broadcasted_iota(jnp.int32, (t, t), 0)
    col = lax.broadcasted_iota(jnp.int32, (t, t), 1)
    strict = col < row
    suf = suf_ref[...]

    def tile(ki, carry, diag):
        run, acc = carry
        rows = pl.ds(pl.multiple_of(ki * t, t), t)
        heads = [(p, j) for p in range(n_pairs) for j in range(2)]
        z = [_dot(split[p][1][j], kt_ref[ki, p * LANES:(p + 1) * LANES, :]) for p, j in heads]
        lsz, lk = [], []
        for zh in z:
            sp = jnp.log(1.0 + jnp.exp(-jnp.abs(zh)))
            lszh = jnp.minimum(zh, 0.0) - sp
            lkh = lszh - zh
            lsz.append(lszh)
            lk.append(jnp.where(strict, lkh, 0.0) if diag else lkh)
        within = [_dot(lkh.astype(BF16), suf) for lkh in lk]
        w = []
        for h in range(len(heads)):
            wh = jnp.exp(lsz[h] + within[h] + run[h])
            w.append((jnp.where(strict, wh, 0.0) if diag else wh).astype(BF16))
        pv = [_dot(w[h], v_ref[rows, p * LANES:(p + 1) * LANES]) for h, (p, j) in enumerate(heads)]
        new_run = tuple(run[h] + jnp.sum(lk[h], axis=-1, keepdims=True) for h in range(len(heads)))
        new_acc = tuple(acc[p] + jnp.where(left, pv[2 * p], pv[2 * p + 1]) for p in range(n_pairs))
        return new_run, new_acc

    def alive(run):
        top = run[0]
        for r in run[1:]:
            top = jnp.maximum(top, r)
        return jnp.max(top) > EXP_UNDERFLOW

    zero = jnp.zeros((t, 1), F32)
    state = tile(qi, ((zero,) * (2 * n_pairs), (jnp.zeros((t, LANES), F32),) * n_pairs), True)

    def body(c):
        i, _, st = c
        st = tile(qi - 1 - i, st, False)
        return i + 1, alive(st[0]), st

    _, _, state = lax.while_loop(lambda c: jnp.logical_and(c[0] < qi, c[1]), body,
                                 (jnp.int32(0), alive(state[0]), state))
    for p in range(n_pairs):
        o_ref[:, p * LANES:(p + 1) * LANES] = state[1][p]


def _fox_kernel(q_ref, kt_ref, v_ref, o_ref, bound_ref):
    t = q_ref.shape[0]
    nk, n_pairs = kt_ref.shape[0], kt_ref.shape[1]
    heads = [(p, j) for p in range(n_pairs) for j in range(2)]
    nh = len(heads)
    qi = pl.program_id(1)
    lane = lax.broadcasted_iota(jnp.int32, (t, LANES), 1)
    left = lane < HEAD_DIM
    qh, qn = [], []
    for p in range(n_pairs):
        q = q_ref[:, p * LANES:(p + 1) * LANES]
        _, qm = _pair_split(q)
        q2 = q.astype(F32) * q.astype(F32)
        for j in range(2):
            ones = jnp.where(jnp.logical_and(lane >= FOX_AUG * j, lane < FOX_AUG * j + 3), 1.0, 0.0)
            qh.append(jnp.concatenate([qm[j], ones.astype(BF16)], axis=1))
            sel = left if j == 0 else jnp.logical_not(left)
            qn.append(jnp.sqrt(jnp.sum(jnp.where(sel, q2, 0.0), axis=-1, keepdims=True)))
    row = lax.broadcasted_iota(jnp.int32, (t, t), 0)
    col = lax.broadcasted_iota(jnp.int32, (t, t), 1)
    causal = col <= row

    @pl.when(qi == 0)
    def _():
        def scan(ki, c):
            out_kn, out_nf = [], []
            for h, (p, j) in enumerate(heads):
                k = kt_ref[ki, p].astype(F32)
                n2 = jnp.sum((k * k)[j * HEAD_DIM:(j + 1) * HEAD_DIM], axis=0, keepdims=True)
                lo = LANES + FOX_AUG * j
                negf = jnp.sum(k[lo:lo + FOX_AUG], axis=0, keepdims=True)
                kn = jnp.maximum(c[h], jnp.sqrt(jnp.max(n2)))
                nf = jnp.maximum(c[nh + h], jnp.max(negf))
                bound_ref[h, ki] = kn
                bound_ref[nh + h, ki] = nf
                out_kn.append(kn)
                out_nf.append(nf)
            return tuple(out_kn + out_nf)

        lax.fori_loop(0, nk, scan, (jnp.float32(0.0),) * nh + (jnp.float32(-jnp.inf),) * nh)

    def tile(ki, carry, mode):
        m, l, acc = carry
        rows = pl.ds(pl.multiple_of(ki * t, t), t)
        s = [_dot(qh[h], kt_ref[ki, p]) for h, (p, j) in enumerate(heads)]
        if mode == "diag":
            s = [jnp.where(causal, sh, -jnp.inf) for sh in s]
        if mode == "fast":
            new_m = m
        else:
            new_m = tuple(jnp.maximum(m[h], jnp.max(s[h], axis=-1, keepdims=True)) for h in range(nh))
        e = [jnp.exp(s[h] - new_m[h]) for h in range(nh)]
        pv = [_dot(e[h].astype(BF16), v_ref[rows, p * LANES:(p + 1) * LANES])
              for h, (p, j) in enumerate(heads)]
        rowsum = [jnp.sum(e[h], axis=-1, keepdims=True) for h in range(nh)]
        if mode == "fast":
            new_l = tuple(l[h] + rowsum[h] for h in range(nh))
        else:
            corr = [jnp.exp(m[h] - new_m[h]) for h in range(nh)]
            new_l = tuple(l[h] * corr[h] + rowsum[h] for h in range(nh))
            acc = tuple(acc[p] * jnp.where(left, corr[2 * p], corr[2 * p + 1]) for p in range(n_pairs))
        new_acc = tuple(acc[p] + jnp.where(left, pv[2 * p], pv[2 * p + 1]) for p in range(n_pairs))
        return new_m, new_l, new_acc

    def row_gap(m, ki):
        kic = jnp.maximum(ki, 0)
        return tuple(jnp.max(qn[h] * bound_ref[h, kic] - m[h]) for h in range(nh))

    def gap(rg, ki):
        kic = jnp.maximum(ki, 0)
        g = rg[0] + bound_ref[nh, kic]
        for h in range(1, nh):
            g = jnp.maximum(g, rg[h] + bound_ref[nh + h, kic])
        return g

    ninf = jnp.full((t, 1), -jnp.inf, F32)
    zero = jnp.zeros((t, 1), F32)
    state = tile(qi, ((ninf,) * nh, (zero,) * nh, (jnp.zeros((t, LANES), F32),) * n_pairs), "diag")

    def general(c):
        i, _, st = c
        ki = qi - 1 - i
        st = tile(ki, st, "general")
        return i + 1, gap(row_gap(st[0], ki - 1), ki - 1), st

    i0, _, (m, l, acc) = lax.while_loop(
        lambda c: jnp.logical_and(c[0] < qi, c[1] >= FAST_MARGIN), general,
        (jnp.int32(0), gap(row_gap(state[0], qi - 1), qi - 1), state))
    rg = row_gap(m, qi - 1 - i0)

    def fast(c):
        i, _, l, acc = c
        ki = qi - 1 - i
        _, l, acc = tile(ki, (m, l, acc), "fast")
        return i + 1, gap(rg, ki - 1), l, acc

    _, _, l, acc = lax.while_loop(
        lambda c: jnp.logical_and(c[0] < qi, c[1] > EXP_UNDERFLOW), fast,
        (i0, gap(rg, qi - 1 - i0), l, acc))
    for p in range(n_pairs):
        o_ref[:, p * LANES:(p + 1) * LANES] = acc[p] / jnp.where(left, l[2 * p], l[2 * p + 1])


def _stick_breaking(q, kt, v):
    b, s, width = q.shape
    t = SB_TILE
    nk = s // t
    suffix = jnp.asarray(np.tril(np.ones((t, t), np.float32), -1), BF16)
    once = pl.Buffered(1)
    return pl.pallas_call(
        _sb_kernel,
        out_shape=jax.ShapeDtypeStruct((b, s, width), F32),
        grid=(b, nk),
        in_specs=[
            pl.BlockSpec((None, t, width), lambda bi, i: (bi, i, 0)),
            pl.BlockSpec((None, nk, width, t), lambda bi, i: (bi, 0, 0, 0), pipeline_mode=once),
            pl.BlockSpec((None, s, width), lambda bi, i: (bi, 0, 0), pipeline_mode=once),
            pl.BlockSpec((t, t), lambda bi, i: (0, 0), pipeline_mode=once),
        ],
        out_specs=pl.BlockSpec((None, t, width), lambda bi, i: (bi, i, 0)),
        compiler_params=_params("arbitrary", "arbitrary"),
        name="stick_breaking",
    )(q, kt, v, suffix)


def _forgetting(q, kt, v):
    b, s, width = q.shape
    t = FOX_TILE
    nk = s // t
    n_pairs = width // LANES
    once = pl.Buffered(1)
    return pl.pallas_call(
        _fox_kernel,
        out_shape=jax.ShapeDtypeStruct((b, s, width), F32),
        grid=(b, nk),
        in_specs=[
            pl.BlockSpec((None, t, width), lambda bi, i: (bi, i, 0)),
            pl.BlockSpec((None, nk, n_pairs, 2 * LANES, t), lambda bi, i: (bi, 0, 0, 0, 0),
                         pipeline_mode=once),
            pl.BlockSpec((None, s, width), lambda bi, i: (bi, 0, 0), pipeline_mode=once),
        ],
        out_specs=pl.BlockSpec((None, t, width), lambda bi, i: (bi, i, 0)),
        scratch_shapes=[pltpu.SMEM((4 * n_pairs, nk), F32)],
        compiler_params=_params("arbitrary", "arbitrary"),
        name="forgetting",
    )(q, kt, v)


def _swa_kernel(q_ref, ktc_ref, ktp_ref, vc_ref, vp_ref, bias_ref, sink_ref, o_ref):
    r = q_ref.shape[0]
    first = pl.program_id(1) == 0
    lane = lax.broadcasted_iota(jnp.int32, (WINDOW, LANES), 1)
    left = lane < HEAD_DIM
    col = lax.broadcasted_iota(jnp.int32, (WINDOW, 2 * WINDOW), 1)
    n_win = r // WINDOW
    kts, vs = [], []
    for w in range(n_win):
        lo = w * WINDOW
        if w == 0:
            kts.append(jnp.concatenate([ktp_ref[0, :, r - WINDOW:], ktc_ref[0, :, :WINDOW]], axis=1))
            vs.append(jnp.concatenate([vp_ref[...], vc_ref[:WINDOW, :]], axis=0))
        else:
            kts.append(ktc_ref[0, :, lo - WINDOW:lo + WINDOW])
            vs.append(vc_ref[lo - WINDOW:lo + WINDOW, :])
    units = [(w, g, kv) for w in range(n_win) for g in range(2) for kv in range(N_KV_SWA)]
    z = []
    for w, g, kv in units:
        qg = q_ref[w * WINDOW:(w + 1) * WINDOW, g * LANES:(g + 1) * LANES]
        zero = jnp.zeros_like(qg)
        qm = jnp.where(left, qg, zero) if kv == 0 else jnp.where(left, zero, qg)
        zu = _dot(qm, kts[w]) + bias_ref[2 * kv + g]
        if w == 0:
            zu = jnp.where(jnp.logical_and(first, col < WINDOW), -jnp.inf, zu)
        z.append(zu)
    p = []
    for (w, g, kv), zu in zip(units, z):
        head = 2 * kv + g
        sink = sink_ref[head:head + 1, 0:1]
        m = jnp.maximum(jnp.max(zu, axis=-1, keepdims=True), sink)
        e = jnp.exp(zu - m)
        den = jnp.sum(e, axis=-1, keepdims=True) + jnp.exp(sink - m)
        p.append((e * (1.0 / den)).astype(BF16))
    pv = {u: _dot(pu, vs[u[0]]) for u, pu in zip(units, p)}
    for w in range(n_win):
        for g in range(2):
            o_ref[w * WINDOW:(w + 1) * WINDOW, g * LANES:(g + 1) * LANES] = jnp.where(
                left, pv[(w, g, 0)], pv[(w, g, 1)])


def _sliding_window(l, q, kt, v, bias, sinks):
    b, s, _ = q.shape
    r = ROW_TILE
    kv_w = N_KV_SWA * HEAD_DIM
    wpt = r // WINDOW
    return pl.pallas_call(
        _swa_kernel,
        out_shape=jax.ShapeDtypeStruct((b, s, GROUP_WIDTH), F32),
        grid=(b, s // r),
        in_specs=[
            pl.BlockSpec((None, r, GROUP_WIDTH), lambda bi, i: (bi, i, 0)),
            pl.BlockSpec((None, 1, kv_w, r), lambda bi, i: (bi, i, 0, 0)),
            pl.BlockSpec((None, 1, kv_w, r), lambda bi, i: (bi, jnp.maximum(i - 1, 0), 0, 0)),
            pl.BlockSpec((None, r, kv_w), lambda bi, i: (bi, i, 0)),
            pl.BlockSpec((None, WINDOW, kv_w), lambda bi, i: (bi, jnp.maximum(i * wpt - 1, 0), 0)),
            pl.BlockSpec(bias.shape, lambda bi, i: (0, 0, 0)),
            pl.BlockSpec((None,) + sinks.shape[1:], lambda bi, i: (l, 0, 0)),
        ],
        out_specs=pl.BlockSpec((None, r, GROUP_WIDTH), lambda bi, i: (bi, i, 0)),
        compiler_params=_params("arbitrary", "arbitrary"),
        name="sliding_window",
    )(q, kt, kt, v, v, bias, sinks)


def _swa_bias(rel_bias):
    i = np.arange(WINDOW)[:, None]
    j = np.arange(2 * WINDOW)[None, :]
    dist = WINDOW + i - j
    in_window = (dist >= 0) & (dist < WINDOW)
    d = np.clip(dist, 0, None)
    max_exact = REL_BUCKETS // 2
    safe = np.maximum(d, 1).astype(np.float32)
    large = max_exact + (np.log(safe / max_exact) / math.log(REL_MAX_DIST / max_exact)
                         * (REL_BUCKETS - max_exact)).astype(np.int32)
    large = np.minimum(large, REL_BUCKETS - 1)
    bucket = np.where(d < max_exact, d, large).astype(np.int32)
    onehot = jnp.asarray(np.eye(REL_BUCKETS, dtype=np.float32)[bucket.reshape(-1)])
    bias = jnp.dot(onehot, rel_bias.astype(F32), precision=lax.Precision.HIGHEST)
    bias = bias.T.reshape(-1, WINDOW, 2 * WINDOW)
    return jnp.where(jnp.asarray(in_window)[None], bias, -jnp.inf)


def _ssm_kernel(u0_ref, u1_ref, avec_ref, bmat_ref, cmat_ref, d_ref, wglu_ref, bglu_ref,
                o0_ref, o1_ref, up_ref, xr_ref, xi_ref, er_ref, ei_ref, sr_ref, si_ref, carry_ref):
    n_slab, ts, _ = xr_ref.shape
    n = ts // SSM_CHUNK
    np_ = n_slab * LANES

    @pl.when(pl.program_id(1) == 0)
    def _():
        carry_ref[...] = jnp.zeros_like(carry_ref)

    def rows(i):
        return pl.ds(i * n, n)

    for k, u_ref in enumerate((u0_ref, u1_ref)):
        for i in range(SSM_CHUNK):
            up_ref[rows(i), k * LANES:(k + 1) * LANES] = u_ref[pl.ds(i, n, stride=SSM_CHUNK), :]
    u = up_ref[...]
    bu = _dot(u.astype(BF16), bmat_ref[...])
    for j in range(n_slab):
        xr_ref[j] = bu[:, j * LANES:(j + 1) * LANES]
        xi_ref[j] = bu[:, np_ + j * LANES:np_ + (j + 1) * LANES]

    def coef(k, j):
        return avec_ref[k:k + 1, j * LANES:(j + 1) * LANES]

    for j in range(n_slab):
        ar, ai = coef(0, j), coef(1, j)
        pr, pi_ = xr_ref[j, rows(0), :], xi_ref[j, rows(0), :]
        for i in range(1, SSM_CHUNK):
            nr = ar * pr - ai * pi_ + xr_ref[j, rows(i), :]
            ni = ar * pi_ + ai * pr + xi_ref[j, rows(i), :]
            xr_ref[j, rows(i), :] = nr
            xi_ref[j, rows(i), :] = ni
            pr, pi_ = nr, ni
        er_ref[j] = pr
        ei_ref[j] = pi_

    a16 = [(coef(2, j), coef(3, j)) for j in range(n_slab)]

    def chunk(c, s):
        out = []
        for j in range(n_slab):
            sr, si = s[j]
            sr_ref[j, pl.ds(c, 1), :] = sr
            si_ref[j, pl.ds(c, 1), :] = si
            er = er_ref[j, pl.ds(c, 1), :]
            ei = ei_ref[j, pl.ds(c, 1), :]
            a16r, a16i = a16[j]
            out.append((a16r * sr - a16i * si + er, a16r * si + a16i * sr + ei))
        return tuple(out)

    init = tuple((carry_ref[0:1, j * LANES:(j + 1) * LANES], carry_ref[1:2, j * LANES:(j + 1) * LANES])
                 for j in range(n_slab))
    fin = lax.fori_loop(0, n, chunk, init)
    for j in range(n_slab):
        carry_ref[0:1, j * LANES:(j + 1) * LANES] = fin[j][0]
        carry_ref[1:2, j * LANES:(j + 1) * LANES] = fin[j][1]

    for j in range(n_slab):
        ar, ai = coef(0, j), coef(1, j)
        pr, pi_ = sr_ref[j], si_ref[j]
        for i in range(SSM_CHUNK):
            pr, pi_ = ar * pr - ai * pi_, ar * pi_ + ai * pr
            xr_ref[j, rows(i), :] = xr_ref[j, rows(i), :] + pr
            xi_ref[j, rows(i), :] = xi_ref[j, rows(i), :] + pi_

    xr = jnp.concatenate([xr_ref[j].astype(BF16) for j in range(n_slab)], axis=1)
    xi = jnp.concatenate([xi_ref[j].astype(BF16) for j in range(n_slab)], axis=1)
    y = _dot(xr, cmat_ref[0:np_, :]) + _dot(xi, cmat_ref[np_:, :]) + d_ref[...] * u
    y = 0.5 * y * (1.0 + jnp.tanh(math.sqrt(2.0 / math.pi) * (y + 0.044715 * (y * y * y))))
    gate = jax.nn.sigmoid(_dot(y.astype(BF16), wglu_ref[...]) + bglu_ref[...])
    out = y * gate
    for k, o_ref in enumerate((o0_ref, o1_ref)):
        for i in range(SSM_CHUNK):
            o_ref[pl.ds(i, n, stride=SSM_CHUNK), :] = out[i * n:(i + 1) * n, k * LANES:(k + 1) * LANES]


def _ssm(l, u0, u1, avec, bmat, cmat, d_skip, w_glu, b_glu):
    b, s, hw = u0.shape
    ts = min(SSM_TILE, s)
    np_ = avec.shape[-1]
    n = ts // SSM_CHUNK
    n_slab = np_ // LANES
    lspec = lambda a: pl.BlockSpec((None,) + a.shape[1:], lambda bi, i: (l, 0, 0))
    half = pl.BlockSpec((None, ts, hw), lambda bi, i: (bi, i, 0))
    return pl.pallas_call(
        _ssm_kernel,
        out_shape=(jax.ShapeDtypeStruct((b, s, hw), F32), jax.ShapeDtypeStruct((b, s, hw), F32)),
        grid=(b, s // ts),
        in_specs=[half, half,
                  lspec(avec), lspec(bmat), lspec(cmat), lspec(d_skip), lspec(w_glu), lspec(b_glu)],
        out_specs=(half, half),
        scratch_shapes=[pltpu.VMEM((ts, 2 * hw), F32),
                        pltpu.VMEM((n_slab, ts, LANES), F32), pltpu.VMEM((n_slab, ts, LANES), F32),
                        pltpu.VMEM((n_slab, n, LANES), F32), pltpu.VMEM((n_slab, n, LANES), F32),
                        pltpu.VMEM((n_slab, n, LANES), F32), pltpu.VMEM((n_slab, n, LANES), F32),
                        pltpu.VMEM((8, np_), F32)],
        compiler_params=_params("arbitrary", "arbitrary"),
        name="s5_ssm",
    )(u0, u1, avec, bmat, cmat, d_skip, w_glu, b_glu)


def _post_kernel(osb_ref, osw_ref, ofx_ref, ossm0_ref, ossm1_ref, x_ref, mod_ref, og_ref, n2_ref,
                 wout_ref, w1_ref, w2_ref, fg_ref, o_ref, *, final):
    mod = mod_ref[...]
    g1, sh2, sc2, g2 = mod[2:3], mod[3:4], mod[4:5], mod[5:6]
    ssm = jnp.concatenate([ossm0_ref[...], ossm1_ref[...]], axis=1)
    mo = None
    for k, o in enumerate((osb_ref[...], osw_ref[...], ofx_ref[...], ssm)):
        lo = k * GROUP_WIDTH
        y = (_rms(o) * og_ref[:, lo:lo + GROUP_WIDTH]).astype(BF16)
        part = _dot(y, wout_ref[lo:lo + GROUP_WIDTH, :])
        mo = part if mo is None else mo + part
    x1 = x_ref[...] + g1 * mo
    h = (_rms(x1) * n2_ref[...] * (1.0 + sc2) + sh2).astype(BF16)
    d_ff = w1_ref.shape[1]
    fc = 1024
    acc = None
    for c in range(d_ff // fc):
        a = _dot(h, w1_ref[:, c * fc:(c + 1) * fc])
        a = jnp.square(jnp.maximum(a, 0.0)).astype(BF16)
        part = _dot(a, w2_ref[c * fc:(c + 1) * fc, :])
        acc = part if acc is None else acc + part
    x2 = x1 + g2 * acc
    if final:
        x2 = _rms(x2) * fg_ref[...]
    o_ref[...] = x2


def _post(l, final, o_sb, o_sw, o_fx, o_ssm0, o_ssm1, x, mod, out_gain, norm2_gain, w_out, w1, w2,
          final_gain):
    b, s, d = x.shape
    tm = ROW_TILE
    grp = pl.BlockSpec((None, tm, GROUP_WIDTH), lambda bi, i: (bi, i, 0))
    half = pl.BlockSpec((None, tm, LANES), lambda bi, i: (bi, i, 0))
    xspec = pl.BlockSpec((None, tm, d), lambda bi, i: (bi, i, 0))
    lspec = lambda a: pl.BlockSpec((None,) + a.shape[1:], lambda bi, i: (l, 0, 0),
                                   pipeline_mode=pl.Buffered(1))
    return pl.pallas_call(
        functools.partial(_post_kernel, final=final),
        out_shape=jax.ShapeDtypeStruct((b, s, d), F32),
        grid=(b, s // tm),
        in_specs=[grp, grp, grp, half, half, xspec,
                  pl.BlockSpec((None, None, N_ADA, d), lambda bi, i: (l, bi, 0, 0)),
                  lspec(out_gain), lspec(norm2_gain), lspec(w_out), lspec(w1), lspec(w2),
                  pl.BlockSpec(final_gain.shape, lambda bi, i: (0, 0))],
        out_specs=xspec,
        compiler_params=_params("arbitrary", "arbitrary"),
        name="post_mlp",
    )(o_sb, o_sw, o_fx, o_ssm0, o_ssm1, x, mod, out_gain, norm2_gain, w_out, w1, w2, final_gain)


def _swa_head_perm(a, axis):
    shape = a.shape
    a = a.reshape(shape[:axis] + (4, HEAD_DIM) + shape[axis + 1:])
    a = jnp.take(a, jnp.asarray([0, 2, 1, 3]), axis=axis)
    return a.reshape(shape)


def kernel(x, c, w_ada, b_ada, norm1_gain, norm2_gain, w_in, rel_bias, sinks, forget_bias, lam_re, lam_im, log_dt, ssm_b_re, ssm_b_im, ssm_c_re, ssm_c_im, ssm_d, w_glu, b_glu, out_gain, w_out, w_mlp_in, w_mlp_out, final_gain):
    b, s, d = x.shape
    depth = w_in.shape[0]
    g, p = lam_re.shape[1:]
    h = ssm_b_re.shape[-1]
    n_fox = forget_bias.shape[-1]
    scale = 1.0 / math.sqrt(HEAD_DIM)
    assert s % SSM_TILE == 0 and s % ROW_TILE == 0 and ROW_TILE % SB_TILE == 0 and ROW_TILE % FOX_TILE == 0

    gw, kvw = GROUP_WIDTH, N_KV_SWA * HEAD_DIM
    sizes = (gw, gw, gw, gw, kvw, kvw, gw, gw, gw, n_fox, gw)
    offs = np.concatenate([[0], np.cumsum(sizes)])
    col = lambda k: w_in[:, :, offs[k]:offs[k + 1]]
    w_sbq = (col(0) * scale).astype(BF16)
    w_sbv = col(2).astype(BF16)
    w_swq = (_swa_head_perm(col(3), 2) * scale).astype(BF16)
    w_swv = col(5).astype(BF16)
    w_fxq = (col(6) * scale).astype(BF16)
    w_fxv = col(8).astype(BF16)
    w_u = col(10).astype(BF16)
    wkt = jnp.concatenate([col(1), col(7), col(4)], axis=2).transpose(0, 2, 1).astype(BF16)
    wft = jnp.pad(col(9).transpose(0, 2, 1), ((0, 0), (0, 8 - n_fox), (0, 0))).astype(BF16)
    forget_b = jnp.pad(forget_bias.astype(F32), ((0, 0), (0, 8 - n_fox)))[:, :, None]
    in_wts = (w_sbq, w_sbv, w_swq, w_swv, w_fxq, w_fxv, w_u, wkt, wft)

    og = out_gain.astype(F32)
    og = jnp.concatenate([og[:, :gw], _swa_head_perm(og[:, gw:2 * gw], 1), og[:, 2 * gw:]], axis=1)
    wo = jnp.concatenate([w_out[:, :gw], _swa_head_perm(w_out[:, gw:2 * gw], 1), w_out[:, 2 * gw:]],
                         axis=1).astype(BF16)
    og = og[:, None, :]
    n1 = norm1_gain.astype(F32)[:, None, :]
    n2 = norm2_gain.astype(F32)[:, None, :]
    w1 = w_mlp_in.astype(BF16)
    w2 = w_mlp_out.astype(BF16)
    fg = final_gain.astype(F32)[None, :]
    bias = _swa_bias(rel_bias)
    sinks_b = jnp.broadcast_to(sinks.astype(F32)[:, :, None], sinks.shape + (LANES,))

    avec, bbr, bbi = _ssm_prep(lam_re.astype(F32), lam_im.astype(F32), log_dt.astype(F32),
                               ssm_b_re.astype(F32), ssm_b_im.astype(F32))
    eye = jnp.eye(g, dtype=F32)
    blockdiag_b = lambda bb: jnp.einsum('lhgp,gk->lghkp', bb.reshape(depth, h, g, p), eye
                                        ).reshape(depth, g * h, g * p)
    bmat = jnp.concatenate([blockdiag_b(bbr), blockdiag_b(bbi)], axis=2).astype(BF16)
    blockdiag_c = lambda cc: jnp.einsum('lghp,gk->lgpkh', cc.astype(F32), eye
                                        ).reshape(depth, g * p, g * h)
    cmat = jnp.concatenate([blockdiag_c(ssm_c_re), -blockdiag_c(ssm_c_im)], axis=1).astype(BF16)
    d_skip = ssm_d.astype(F32).reshape(depth, 1, g * h)
    wglu = w_glu.astype(BF16)
    bglu = b_glu.astype(F32)[:, None, :]

    c_pad = jnp.pad(c.astype(F32), ((0, 8 - b), (0, 0)))
    mod = _ada(c_pad, w_ada.astype(F32), b_ada.astype(F32))[:, :b].reshape(depth, b, N_ADA, d)

    x = x.astype(F32)
    for l in range(depth):
        (sbq, sbv, swq, swv, fxq, fxv, u0, u1, sbkt, fxkt, swkt) = _inproj(l, x, mod, n1, in_wts, forget_b)
        o_sb = _stick_breaking(sbq, sbkt, sbv)
        o_sw = _sliding_window(l, swq, swkt, swv, bias, sinks_b)
        o_fx = _forgetting(fxq, fxkt, fxv)
        o_ssm0, o_ssm1 = _ssm(l, u0, u1, avec, bmat, cmat, d_skip, wglu, bglu)
        x = _post(l, l == depth - 1, o_sb, o_sw, o_fx, o_ssm0, o_ssm1, x, mod, og, n2, wo, w1, w2, fg)
    return x
```

```python
import functools
import math

import numpy as np
import jax
import jax.numpy as jnp
from jax import lax
from jax.experimental import pallas as pl
from jax.experimental.pallas import tpu as pltpu

F32 = jnp.float32
BF16 = jnp.bfloat16

HEAD_DIM = 64
LANES = 128
GROUP_WIDTH = 256
N_KV_SWA = 2
WINDOW = 128
REL_BUCKETS = 32
REL_MAX_DIST = 128
N_ADA = 6
NORM_EPS = 1e-6
SSM_H = 16
SSM_P = 64
SSM_CHUNK = 16

ROW_TILE = 512
SB_TILE = 256
FOX_TILE = 512
SSM_TILE = 1024
VMEM_LIMIT_BYTES = 56 * 1024 * 1024
EXP_UNDERFLOW = -105.0
FAST_MARGIN = 60.0
FOX_AUG = 8


def _dot(a, b):
    return jnp.dot(a, b, preferred_element_type=F32)


def _dot_nt(a, b):
    return lax.dot_general(a, b, (((1,), (1,)), ((), ())), preferred_element_type=F32)


def _split_bf16(a):
    hi = a.astype(BF16)
    lo = (a - hi.astype(F32)).astype(BF16)
    return hi, lo


def _rms(x):
    return x * lax.rsqrt(jnp.mean(x * x, axis=-1, keepdims=True) + NORM_EPS)


def _params(*sem):
    return pltpu.CompilerParams(dimension_semantics=sem, vmem_limit_bytes=VMEM_LIMIT_BYTES)


def _ada_kernel(c_ref, w_ref, b_ref, o_ref):
    c = c_ref[...]
    ca = c * jax.nn.sigmoid(c)
    a_hi, a_lo = _split_bf16(ca)
    w_hi, w_lo = _split_bf16(w_ref[...])
    o_ref[...] = _dot(a_hi, w_hi) + _dot(a_hi, w_lo) + _dot(a_lo, w_hi) + b_ref[...]


def _ada(c_pad, w_ada, b_ada):
    depth, d, n = w_ada.shape
    tn = 2048
    return pl.pallas_call(
        _ada_kernel,
        out_shape=jax.ShapeDtypeStruct((depth, c_pad.shape[0], n), F32),
        grid=(depth, n // tn),
        in_specs=[
            pl.BlockSpec(c_pad.shape, lambda l, j: (0, 0)),
            pl.BlockSpec((None, d, tn), lambda l, j: (l, 0, j)),
            pl.BlockSpec((None, 1, tn), lambda l, j: (l, 0, j)),
        ],
        out_specs=pl.BlockSpec((None, c_pad.shape[0], tn), lambda l, j: (l, 0, j)),
        compiler_params=_params("arbitrary", "arbitrary"),
        name="ada_mod",
    )(c_pad, w_ada, b_ada.reshape(depth, 1, n))


def _ssm_prep_kernel(lr_ref, li_ref, ldt_ref, br_ref, bi_ref, avec_ref, bbr_ref, bbi_ref):
    lr = lr_ref[...]
    li = li_ref[...]
    dt = jnp.exp(ldt_ref[...])
    mag = jnp.exp(lr * dt)
    ang = li * dt
    ar = mag * jnp.cos(ang)
    ai = mag * jnp.sin(ang)
    den = lr * lr + li * li
    nr = ar - 1.0
    ni = ai
    cr = (nr * lr + ni * li) / den
    ci = (ni * lr - nr * li) / den
    pr, pi_ = ar, ai
    for _ in range(int(math.log2(SSM_CHUNK))):
        pr, pi_ = pr * pr - pi_ * pi_, 2.0 * pr * pi_
    avec_ref[0:1, :] = ar
    avec_ref[1:2, :] = ai
    avec_ref[2:3, :] = pr
    avec_ref[3:4, :] = pi_
    avec_ref[4:8, :] = jnp.zeros((4, ar.shape[1]), F32)
    br = br_ref[...]
    bi = bi_ref[...]
    bbr_ref[...] = cr * br - ci * bi
    bbi_ref[...] = cr * bi + ci * br


def _ssm_prep(lam_re, lam_im, log_dt, b_re, b_im):
    depth, g, p = lam_re.shape
    h = b_re.shape[-1]
    n = g * p
    row = lambda a: a.reshape(depth, 1, n)
    ldt = jnp.broadcast_to(log_dt[:, :, None], (depth, g, p))
    bt = lambda a: a.transpose(0, 3, 1, 2).reshape(depth, h, n)
    vec = pl.BlockSpec((None, 1, n), lambda l: (l, 0, 0))
    mat = pl.BlockSpec((None, h, n), lambda l: (l, 0, 0))
    return pl.pallas_call(
        _ssm_prep_kernel,
        out_shape=(jax.ShapeDtypeStruct((depth, 8, n), F32),
                   jax.ShapeDtypeStruct((depth, h, n), F32),
                   jax.ShapeDtypeStruct((depth, h, n), F32)),
        grid=(depth,),
        in_specs=[vec, vec, vec, mat, mat],
        out_specs=(pl.BlockSpec((None, 8, n), lambda l: (l, 0, 0)), mat, mat),
        compiler_params=_params("arbitrary"),
        name="ssm_prep",
    )(row(lam_re), row(lam_im), row(ldt), bt(b_re), bt(b_im))


def _inproj_kernel(x_ref, mod_ref, gain_ref, w_sbq, w_sbv, w_swq, w_swv, w_fxk, w_u,
                   wt_ref, wf_ref, fb_ref, sel_ref,
                   sbq_o, sbv_o, swq_o, swv_o, fxk_o, u0_o, u1_o, sbkt_o, fxqt_o, fxvt_o, swkt_o,
                   carry_ref):
    ts = x_ref.shape[0]
    mod = mod_ref[...]
    sh1 = mod[0:1]
    sc1 = mod[1:2]
    h = (_rms(x_ref[...]) * gain_ref[...] * (1.0 + sc1) + sh1).astype(BF16)

    f = _dot(h, wf_ref[...]) + fb_ref[...]
    logf = jnp.minimum(f, 0.0) - jnp.log1p(jnp.exp(-jnp.abs(f)))
    step = lax.broadcasted_iota(jnp.int32, logf.shape, 0)
    cum = logf
    k = 1
    while k < ts:
        cum = cum + jnp.where(step >= k, pltpu.roll(cum, k, axis=0), 0.0)
        k *= 2

    @pl.when(pl.program_id(1) == 0)
    def _():
        carry_ref[...] = jnp.zeros_like(carry_ref)

    cum = cum + carry_ref[0:1, :]
    carry_ref[...] = jnp.broadcast_to(cum[ts - 1:ts, :], carry_ref.shape)
    negf = -cum
    hi = negf.astype(BF16)
    r1 = negf - hi.astype(F32)
    mid = r1.astype(BF16)
    lo3 = (r1 - mid.astype(F32)).astype(BF16)

    k_fx = _dot(h, w_fxk[...])
    n_pairs = fxk_o.shape[1] // (2 * LANES)
    for p in range(n_pairs):
        base = 2 * p * LANES
        fxk_o[:, base:base + LANES] = k_fx[:, p * LANES:(p + 1) * LANES].astype(BF16)
        aug = _dot(hi, sel_ref[p, 0]) + _dot(mid, sel_ref[p, 1]) + _dot(lo3, sel_ref[p, 2])
        fxk_o[:, base + LANES:base + 2 * LANES] = aug.astype(BF16)

    sbq_o[...] = _dot(h, w_sbq[...]).astype(BF16)
    sbv_o[...] = _dot(h, w_sbv[...]).astype(BF16)
    swq_o[...] = _dot(h, w_swq[...]).astype(BF16)
    swv_o[...] = _dot(h, w_swv[...]).astype(BF16)
    u = _dot(h, w_u[...])
    u0_o[...] = u[:, :LANES]
    u1_o[...] = u[:, LANES:]
    tt = _dot_nt(wt_ref[...], h)
    row0 = 0
    for o in (sbkt_o, fxqt_o, fxvt_o, swkt_o):
        n_sub, width, tile = o.shape
        for c in range(n_sub):
            o[c] = tt[row0:row0 + width, c * tile:(c + 1) * tile].astype(BF16)
        row0 += width


def _fox_aug_select():
    n_pairs = GROUP_WIDTH // LANES
    sel = np.zeros((n_pairs, 3, LANES, LANES), np.float32)
    for p in range(n_pairs):
        for j in range(2):
            for r in range(3):
                sel[p, r, 2 * p + j, FOX_AUG * j + r] = 1.0
    return jnp.asarray(sel, BF16)


def _inproj(l, x, mod, gain, wts, forget_b):
    b, s, d = x.shape
    ts = ROW_TILE
    nk = s // ts
    (w_sbq, w_sbv, w_swq, w_swv, w_fxk, w_u, wt, wf) = wts
    sel = _fox_aug_select()
    row = lambda width: pl.BlockSpec((None, ts, width), lambda bi, i: (bi, i, 0))
    wspec = lambda w: pl.BlockSpec((None,) + w.shape[1:], lambda bi, i: (l, 0, 0))
    tspec = lambda width, tile: pl.BlockSpec((None, ts // tile, width, tile), lambda bi, i: (bi, i, 0, 0))
    kv_w = N_KV_SWA * HEAD_DIM
    out_shape = (
        jax.ShapeDtypeStruct((b, s, GROUP_WIDTH), BF16),
        jax.ShapeDtypeStruct((b, s, GROUP_WIDTH), BF16),
        jax.ShapeDtypeStruct((b, s, GROUP_WIDTH), BF16),
        jax.ShapeDtypeStruct((b, s, kv_w), BF16),
        jax.ShapeDtypeStruct((b, s, 2 * GROUP_WIDTH), BF16),
        jax.ShapeDtypeStruct((b, s, LANES), F32),
        jax.ShapeDtypeStruct((b, s, LANES), F32),
        jax.ShapeDtypeStruct((b, s // SB_TILE, GROUP_WIDTH, SB_TILE), BF16),
        jax.ShapeDtypeStruct((b, s // FOX_TILE, GROUP_WIDTH, FOX_TILE), BF16),
        jax.ShapeDtypeStruct((b, s // FOX_TILE, GROUP_WIDTH, FOX_TILE), BF16),
        jax.ShapeDtypeStruct((b, nk, kv_w, ts), BF16),
    )
    out_specs = (row(GROUP_WIDTH), row(GROUP_WIDTH), row(GROUP_WIDTH), row(kv_w), row(2 * GROUP_WIDTH),
                 row(LANES), row(LANES), tspec(GROUP_WIDTH, SB_TILE), tspec(GROUP_WIDTH, FOX_TILE),
                 tspec(GROUP_WIDTH, FOX_TILE), tspec(kv_w, ts))
    return pl.pallas_call(
        _inproj_kernel,
        out_shape=out_shape,
        grid=(b, nk),
        in_specs=[
            pl.BlockSpec((None, ts, d), lambda bi, i: (bi, i, 0)),
            pl.BlockSpec((None, None, N_ADA, d), lambda bi, i: (l, bi, 0, 0)),
            pl.BlockSpec((None, 1, d), lambda bi, i: (l, 0, 0)),
            wspec(w_sbq), wspec(w_sbv), wspec(w_swq), wspec(w_swv), wspec(w_fxk), wspec(w_u),
            wspec(wt), wspec(wf),
            pl.BlockSpec((None, 1, LANES), lambda bi, i: (l, 0, 0)),
            pl.BlockSpec(sel.shape, lambda bi, i: (0, 0, 0, 0)),
        ],
        out_specs=out_specs,
        scratch_shapes=[pltpu.VMEM((8, LANES), F32)],
        compiler_params=_params("arbitrary", "arbitrary"),
        name="in_proj",
    )(x, mod, gain, w_sbq, w_sbv, w_swq, w_swv, w_fxk, w_u, wt, wf, forget_b, sel)


def _pair_split(q):
    lane = lax.broadcasted_iota(jnp.int32, q.shape, 1)
    left = lane < HEAD_DIM
    zero = jnp.zeros_like(q)
    return left, (jnp.where(left, q, zero), jnp.where(left, zero, q))


def _sb_kernel(q_ref, kt_ref, v_ref, suf_ref, o_ref):
    t = q_ref.shape[0]
    n_pairs = q_ref.shape[1] // LANES
    qi = pl.program_id(1)
    split = [_pair_split(q_ref[:, p * LANES:(p + 1) * LANES]) for p in range(n_pairs)]
    left = split[0][0]
    row = lax.broadcasted_iota(jnp.int32, (t, t), 0)
    col = lax.broadcasted_iota(jnp.int32, (t, t), 1)
    strict = col < row
    suf = suf_ref[...]

    def tile(ki, carry, diag):
        run, acc = carry
        rows = pl.ds(pl.multiple_of(ki * t, t), t)
        heads = [(p, j) for p in range(n_pairs) for j in range(2)]
        z = [_dot(split[p][1][j], kt_ref[ki, p * LANES:(p + 1) * LANES, :]) for p, j in heads]
        lsz, lk = [], []
        for zh in z:
            sp = jnp.log(1.0 + jnp.exp(-jnp.abs(zh)))
            lszh = jnp.minimum(zh, 0.0) - sp
            lkh = lszh - zh
            lsz.append(lszh)
            lk.append(jnp.where(strict, lkh, 0.0) if diag else lkh)
        within = [_dot(lkh.astype(BF16), suf) for lkh in lk]
        w = []
        for h in range(len(heads)):
            wh = jnp.exp(lsz[h] + within[h] + run[h])
            w.append((jnp.where(strict, wh, 0.0) if diag else wh).astype(BF16))
        pv = [_dot(w[h], v_ref[rows, p * LANES:(p + 1) * LANES]) for h, (p, j) in enumerate(heads)]
        new_run = tuple(run[h] + jnp.sum(lk[h], axis=-1, keepdims=True) for h in range(len(heads)))
        new_acc = tuple(acc[p] + jnp.where(left, pv[2 * p], pv[2 * p + 1]) for p in range(n_pairs))
        return new_run, new_acc

    def alive(run):
        top = run[0]
        for r in run[1:]:
            top = jnp.maximum(top, r)
        return jnp.max(top) > EXP_UNDERFLOW

    zero = jnp.zeros((t, 1), F32)
    state = tile(qi, ((zero,) * (2 * n_pairs), (jnp.zeros((t, LANES), F32),) * n_pairs), True)

    def body(c):
        i, _, st = c
        st = tile(qi - 1 - i, st, False)
        return i + 1, alive(st[0]), st

    _, _, state = lax.while_loop(lambda c: jnp.logical_and(c[0] < qi, c[1]), body,
                                 (jnp.int32(0), alive(state[0]), state))
    for p in range(n_pairs):
        o_ref[:, p * LANES:(p + 1) * LANES] = state[1][p]


def _fox_kernel(qt_ref, k_ref, vt_ref, o_ref, bound_ref):
    t = qt_ref.shape[1]
    nk = vt_ref.shape[0]
    qi = pl.program_id(2)
    qt = qt_ref[...]
    sub = lax.broadcasted_iota(jnp.int32, (LANES, t), 0)
    top = sub < HEAD_DIM
    zero = jnp.zeros_like(qt)
    q2 = qt.astype(F32) * qt.astype(F32)
    qa, qn = [], []
    for j in range(2):
        mine = top if j == 0 else jnp.logical_not(top)
        ones = jnp.where(jnp.logical_and(sub >= FOX_AUG * j, sub < FOX_AUG * j + 3), 1.0, 0.0)
        qa.append(jnp.concatenate([jnp.where(mine, qt, zero), ones.astype(BF16)], axis=0))
        qn.append(jnp.sqrt(jnp.sum(jnp.where(mine, q2, 0.0), axis=0, keepdims=True)))
    key = lax.broadcasted_iota(jnp.int32, (t, t), 0)
    qry = lax.broadcasted_iota(jnp.int32, (t, t), 1)
    causal = key <= qry
    lane = lax.broadcasted_iota(jnp.int32, (t, LANES), 1)

    @pl.when(qi == 0)
    def _():
        def scan(ki, c):
            kk = k_ref[pl.ds(pl.multiple_of(ki * t, t), t), :].astype(F32)
            k2 = kk[:, :LANES] * kk[:, :LANES]
            aug = kk[:, LANES:]
            out_kn, out_nf = [], []
            for j in range(2):
                mine = (lane < HEAD_DIM) if j == 0 else (lane >= HEAD_DIM)
                n2 = jnp.sum(jnp.where(mine, k2, 0.0), axis=1, keepdims=True)
                here = jnp.logical_and(lane >= FOX_AUG * j, lane < FOX_AUG * (j + 1))
                negf = jnp.sum(jnp.where(here, aug, 0.0), axis=1, keepdims=True)
                kn = jnp.maximum(c[j], jnp.sqrt(jnp.max(n2)))
                nf = jnp.maximum(c[2 + j], jnp.max(negf))
                bound_ref[j, ki] = kn
                bound_ref[2 + j, ki] = nf
                out_kn.append(kn)
                out_nf.append(nf)
            return tuple(out_kn + out_nf)

        lax.fori_loop(0, nk, scan, (jnp.float32(0.0),) * 2 + (jnp.float32(-jnp.inf),) * 2)

    def tile(ki, carry, mode):
        m, l, acc = carry
        kk = k_ref[pl.ds(pl.multiple_of(ki * t, t), t), :]
        vt = vt_ref[ki]
        s = [_dot(kk, qa[j]) for j in range(2)]
        if mode == "diag":
            s = [jnp.where(causal, sj, -jnp.inf) for sj in s]
        if mode == "fast":
            new_m = m
        else:
            new_m = tuple(jnp.maximum(m[j], jnp.max(s[j], axis=0, keepdims=True)) for j in range(2))
        e = [jnp.exp(s[j] - new_m[j]) for j in range(2)]
        pv = [_dot(vt, e[j].astype(BF16)) for j in range(2)]
        colsum = [jnp.sum(e[j], axis=0, keepdims=True) for j in range(2)]
        if mode == "fast":
            new_l = tuple(l[j] + colsum[j] for j in range(2))
        else:
            corr = [jnp.exp(m[j] - new_m[j]) for j in range(2)]
            new_l = tuple(l[j] * corr[j] + colsum[j] for j in range(2))
            acc = acc * jnp.where(top, corr[0], corr[1])
        return new_m, new_l, acc + jnp.where(top, pv[0], pv[1])

    def row_gap(m, ki):
        kic = jnp.maximum(ki, 0)
        return tuple(jnp.max(qn[j] * bound_ref[j, kic] - m[j]) for j in range(2))

    def gap(rg, ki):
        kic = jnp.maximum(ki, 0)
        return jnp.maximum(rg[0] + bound_ref[2, kic], rg[1] + bound_ref[3, kic])

    ninf = jnp.full((1, t), -jnp.inf, F32)
    zrow = jnp.zeros((1, t), F32)
    state = tile(qi, ((ninf, ninf), (zrow, zrow), jnp.zeros((LANES, t), F32)), "diag")

    def general(c):
        i, _, st = c
        ki = qi - 1 - i
        st = tile(ki, st, "general")
        return i + 1, gap(row_gap(st[0], ki - 1), ki - 1), st

    i0, _, (m, l, acc) = lax.while_loop(
        lambda c: jnp.logical_and(c[0] < qi, c[1] >= FAST_MARGIN), general,
        (jnp.int32(0), gap(row_gap(state[0], qi - 1), qi - 1), state))
    rg = row_gap(m, qi - 1 - i0)

    def fast(c):
        i, _, l, acc = c
        ki = qi - 1 - i
        _, l, acc = tile(ki, (m, l, acc), "fast")
        return i + 1, gap(rg, ki - 1), l, acc

    _, _, l, acc = lax.while_loop(
        lambda c: jnp.logical_and(c[0] < qi, c[1] > EXP_UNDERFLOW), fast,
        (i0, gap(rg, qi - 1 - i0), l, acc))
    o_ref[...] = (acc / jnp.where(top, l[0], l[1])).T


def _stick_breaking(q, kt, v):
    b, s, width = q.shape
    t = SB_TILE
    nk = s // t
    suffix = jnp.asarray(np.tril(np.ones((t, t), np.float32), -1), BF16)
    once = pl.Buffered(1)
    return pl.pallas_call(
        _sb_kernel,
        out_shape=jax.ShapeDtypeStruct((b, s, width), F32),
        grid=(b, nk),
        in_specs=[
            pl.BlockSpec((None, t, width), lambda bi, i: (bi, i, 0)),
            pl.BlockSpec((None, nk, width, t), lambda bi, i: (bi, 0, 0, 0), pipeline_mode=once),
            pl.BlockSpec((None, s, width), lambda bi, i: (bi, 0, 0), pipeline_mode=once),
            pl.BlockSpec((t, t), lambda bi, i: (0, 0), pipeline_mode=once),
        ],
        out_specs=pl.BlockSpec((None, t, width), lambda bi, i: (bi, i, 0)),
        compiler_params=_params("arbitrary", "arbitrary"),
        name="stick_breaking",
    )(q, kt, v, suffix)


def _forgetting(qt, k, vt):
    b, nq, width, t = qt.shape
    s = nq * t
    return pl.pallas_call(
        _fox_kernel,
        out_shape=jax.ShapeDtypeStruct((b, s, width), F32),
        grid=(b, width // LANES, nq),
        in_specs=[
            pl.BlockSpec((None, None, LANES, t), lambda bi, p, i: (bi, i, p, 0)),
            pl.BlockSpec((None, s, 2 * LANES), lambda bi, p, i: (bi, 0, p)),
            pl.BlockSpec((None, nq, LANES, t), lambda bi, p, i: (bi, 0, p, 0)),
        ],
        out_specs=pl.BlockSpec((None, t, LANES), lambda bi, p, i: (bi, i, p)),
        scratch_shapes=[pltpu.SMEM((4, nq), F32)],
        compiler_params=_params("arbitrary", "arbitrary", "arbitrary"),
        name="forgetting",
    )(qt, k, vt)


def _swa_kernel(q_ref, ktc_ref, ktp_ref, vc_ref, vp_ref, bias_ref, sink_ref, o_ref):
    r = q_ref.shape[0]
    first = pl.program_id(1) == 0
    lane = lax.broadcasted_iota(jnp.int32, (WINDOW, LANES), 1)
    left = lane < HEAD_DIM
    col = lax.broadcasted_iota(jnp.int32, (WINDOW, 2 * WINDOW), 1)
    n_win = r // WINDOW
    kts, vs = [], []
    for w in range(n_win):
        lo = w * WINDOW
        if w == 0:
            kts.append(jnp.concatenate([ktp_ref[0, :, r - WINDOW:], ktc_ref[0, :, :WINDOW]], axis=1))
            vs.append(jnp.concatenate([vp_ref[...], vc_ref[:WINDOW, :]], axis=0))
        else:
            kts.append(ktc_ref[0, :, lo - WINDOW:lo + WINDOW])
            vs.append(vc_ref[lo - WINDOW:lo + WINDOW, :])
    units = [(w, g, kv) for w in range(n_win) for g in range(2) for kv in range(N_KV_SWA)]
    z = []
    for w, g, kv in units:
        qg = q_ref[w * WINDOW:(w + 1) * WINDOW, g * LANES:(g + 1) * LANES]
        zero = jnp.zeros_like(qg)
        qm = jnp.where(left, qg, zero) if kv == 0 else jnp.where(left, zero, qg)
        zu = _dot(qm, kts[w]) + bias_ref[2 * kv + g]
        if w == 0:
            zu = jnp.where(jnp.logical_and(first, col < WINDOW), -jnp.inf, zu)
        z.append(zu)
    p = []
    for (w, g, kv), zu in zip(units, z):
        head = 2 * kv + g
        sink = sink_ref[head:head + 1, 0:1]
        m = jnp.maximum(jnp.max(zu, axis=-1, keepdims=True), sink)
        e = jnp.exp(zu - m)
        den = jnp.sum(e, axis=-1, keepdims=True) + jnp.exp(sink - m)
        p.append((e * (1.0 / den)).astype(BF16))
    pv = {u: _dot(pu, vs[u[0]]) for u, pu in zip(units, p)}
    for w in range(n_win):
        for g in range(2):
            o_ref[w * WINDOW:(w + 1) * WINDOW, g * LANES:(g + 1) * LANES] = jnp.where(
                left, pv[(w, g, 0)], pv[(w, g, 1)])


def _sliding_window(l, q, kt, v, bias, sinks):
    b, s, _ = q.shape
    r = ROW_TILE
    kv_w = N_KV_SWA * HEAD_DIM
    wpt = r // WINDOW
    return pl.pallas_call(
        _swa_kernel,
        out_shape=jax.ShapeDtypeStruct((b, s, GROUP_WIDTH), F32),
        grid=(b, s // r),
        in_specs=[
            pl.BlockSpec((None, r, GROUP_WIDTH), lambda bi, i: (bi, i, 0)),
            pl.BlockSpec((None, 1, kv_w, r), lambda bi, i: (bi, i, 0, 0)),
            pl.BlockSpec((None, 1, kv_w, r), lambda bi, i: (bi, jnp.maximum(i - 1, 0), 0, 0)),
            pl.BlockSpec((None, r, kv_w), lambda bi, i: (bi, i, 0)),
            pl.BlockSpec((None, WINDOW, kv_w), lambda bi, i: (bi, jnp.maximum(i * wpt - 1, 0), 0)),
            pl.BlockSpec(bias.shape, lambda bi, i: (0, 0, 0)),
            pl.BlockSpec((None,) + sinks.shape[1:], lambda bi, i: (l, 0, 0)),
        ],
        out_specs=pl.BlockSpec((None, r, GROUP_WIDTH), lambda bi, i: (bi, i, 0)),
        compiler_params=_params("arbitrary", "arbitrary"),
        name="sliding_window",
    )(q, kt, kt, v, v, bias, sinks)


def _swa_bias(rel_bias):
    i = np.arange(WINDOW)[:, None]
    j = np.arange(2 * WINDOW)[None, :]
    dist = WINDOW + i - j
    in_window = (dist >= 0) & (dist < WINDOW)
    d = np.clip(dist, 0, None)
    max_exact = REL_BUCKETS // 2
    safe = np.maximum(d, 1).astype(np.float32)
    large = max_exact + (np.log(safe / max_exact) / math.log(REL_MAX_DIST / max_exact)
                         * (REL_BUCKETS - max_exact)).astype(np.int32)
    large = np.minimum(large, REL_BUCKETS - 1)
    bucket = np.where(d < max_exact, d, large).astype(np.int32)
    onehot = jnp.asarray(np.eye(REL_BUCKETS, dtype=np.float32)[bucket.reshape(-1)])
    bias = jnp.dot(onehot, rel_bias.astype(F32), precision=lax.Precision.HIGHEST)
    bias = bias.T.reshape(-1, WINDOW, 2 * WINDOW)
    return jnp.where(jnp.asarray(in_window)[None], bias, -jnp.inf)


def _ssm_kernel(u0_ref, u1_ref, avec_ref, bmat_ref, cmat_ref, d_ref, wglu_ref, bglu_ref,
                o0_ref, o1_ref, up_ref, xr_ref, xi_ref, er_ref, ei_ref, sr_ref, si_ref, carry_ref):
    n_slab, ts, _ = xr_ref.shape
    n = ts // SSM_CHUNK
    np_ = n_slab * LANES

    @pl.when(pl.program_id(1) == 0)
    def _():
        carry_ref[...] = jnp.zeros_like(carry_ref)

    def rows(i):
        return pl.ds(i * n, n)

    for k, u_ref in enumerate((u0_ref, u1_ref)):
        for i in range(SSM_CHUNK):
            up_ref[rows(i), k * LANES:(k + 1) * LANES] = u_ref[pl.ds(i, n, stride=SSM_CHUNK), :]
    u = up_ref[...]
    bu = _dot(u.astype(BF16), bmat_ref[...])
    for j in range(n_slab):
        xr_ref[j] = bu[:, j * LANES:(j + 1) * LANES]
        xi_ref[j] = bu[:, np_ + j * LANES:np_ + (j + 1) * LANES]

    def coef(k, j):
        return avec_ref[k:k + 1, j * LANES:(j + 1) * LANES]

    for j in range(n_slab):
        ar, ai = coef(0, j), coef(1, j)
        pr, pi_ = xr_ref[j, rows(0), :], xi_ref[j, rows(0), :]
        for i in range(1, SSM_CHUNK):
            nr = ar * pr - ai * pi_ + xr_ref[j, rows(i), :]
            ni = ar * pi_ + ai * pr + xi_ref[j, rows(i), :]
            xr_ref[j, rows(i), :] = nr
            xi_ref[j, rows(i), :] = ni
            pr, pi_ = nr, ni
        er_ref[j] = pr
        ei_ref[j] = pi_

    a16 = [(coef(2, j), coef(3, j)) for j in range(n_slab)]

    def chunk(c, s):
        out = []
        for j in range(n_slab):
            sr, si = s[j]
            sr_ref[j, pl.ds(c, 1), :] = sr
            si_ref[j, pl.ds(c, 1), :] = si
            er = er_ref[j, pl.ds(c, 1), :]
            ei = ei_ref[j, pl.ds(c, 1), :]
            a16r, a16i = a16[j]
            out.append((a16r * sr - a16i * si + er, a16r * si + a16i * sr + ei))
        return tuple(out)

    init = tuple((carry_ref[0:1, j * LANES:(j + 1) * LANES], carry_ref[1:2, j * LANES:(j + 1) * LANES])
                 for j in range(n_slab))
    fin = lax.fori_loop(0, n, chunk, init)
    for j in range(n_slab):
        carry_ref[0:1, j * LANES:(j + 1) * LANES] = fin[j][0]
        carry_ref[1:2, j * LANES:(j + 1) * LANES] = fin[j][1]

    for j in range(n_slab):
        ar, ai = coef(0, j), coef(1, j)
        pr, pi_ = sr_ref[j], si_ref[j]
        for i in range(SSM_CHUNK):
            pr, pi_ = ar * pr - ai * pi_, ar * pi_ + ai * pr
            xr_ref[j, rows(i), :] = xr_ref[j, rows(i), :] + pr
            xi_ref[j, rows(i), :] = xi_ref[j, rows(i), :] + pi_

    xr = jnp.concatenate([xr_ref[j].astype(BF16) for j in range(n_slab)], axis=1)
    xi = jnp.concatenate([xi_ref[j].astype(BF16) for j in range(n_slab)], axis=1)
    y = _dot(xr, cmat_ref[0:np_, :]) + _dot(xi, cmat_ref[np_:, :]) + d_ref[...] * u
    y = 0.5 * y * (1.0 + jnp.tanh(math.sqrt(2.0 / math.pi) * (y + 0.044715 * (y * y * y))))
    gate = jax.nn.sigmoid(_dot(y.astype(BF16), wglu_ref[...]) + bglu_ref[...])
    out = y * gate
    for k, o_ref in enumerate((o0_ref, o1_ref)):
        for i in range(SSM_CHUNK):
            o_ref[pl.ds(i, n, stride=SSM_CHUNK), :] = out[i * n:(i + 1) * n, k * LANES:(k + 1) * LANES]


def _ssm(l, u0, u1, avec, bmat, cmat, d_skip, w_glu, b_glu):
    b, s, hw = u0.shape
    ts = min(SSM_TILE, s)
    np_ = avec.shape[-1]
    n = ts // SSM_CHUNK
    n_slab = np_ // LANES
    lspec = lambda a: pl.BlockSpec((None,) + a.shape[1:], lambda bi, i: (l, 0, 0))
    half = pl.BlockSpec((None, ts, hw), lambda bi, i: (bi, i, 0))
    return pl.pallas_call(
        _ssm_kernel,
        out_shape=(jax.ShapeDtypeStruct((b, s, hw), F32), jax.ShapeDtypeStruct((b, s, hw), F32)),
        grid=(b, s // ts),
        in_specs=[half, half,
                  lspec(avec), lspec(bmat), lspec(cmat), lspec(d_skip), lspec(w_glu), lspec(b_glu)],
        out_specs=(half, half),
        scratch_shapes=[pltpu.VMEM((ts, 2 * hw), F32),
                        pltpu.VMEM((n_slab, ts, LANES), F32), pltpu.VMEM((n_slab, ts, LANES), F32),
                        pltpu.VMEM((n_slab, n, LANES), F32), pltpu.VMEM((n_slab, n, LANES), F32),
                        pltpu.VMEM((n_slab, n, LANES), F32), pltpu.VMEM((n_slab, n, LANES), F32),
                        pltpu.VMEM((8, np_), F32)],
        compiler_params=_params("arbitrary", "arbitrary"),
        name="s5_ssm",
    )(u0, u1, avec, bmat, cmat, d_skip, w_glu, b_glu)


def _post_kernel(osb_ref, osw_ref, ofx_ref, ossm0_ref, ossm1_ref, x_ref, mod_ref, og_ref, n2_ref,
                 wout_ref, w1_ref, w2_ref, fg_ref, o_ref, *, final):
    mod = mod_ref[...]
    g1, sh2, sc2, g2 = mod[2:3], mod[3:4], mod[4:5], mod[5:6]
    ssm = jnp.concatenate([ossm0_ref[...], ossm1_ref[...]], axis=1)
    mo = None
    for k, o in enumerate((osb_ref[...], osw_ref[...], ofx_ref[...], ssm)):
        lo = k * GROUP_WIDTH
        y = (_rms(o) * og_ref[:, lo:lo + GROUP_WIDTH]).astype(BF16)
        part = _dot(y, wout_ref[lo:lo + GROUP_WIDTH, :])
        mo = part if mo is None else mo + part
    x1 = x_ref[...] + g1 * mo
    h = (_rms(x1) * n2_ref[...] * (1.0 + sc2) + sh2).astype(BF16)
    d_ff = w1_ref.shape[1]
    fc = 1024
    acc = None
    for c in range(d_ff // fc):
        a = _dot(h, w1_ref[:, c * fc:(c + 1) * fc])
        a = jnp.square(jnp.maximum(a, 0.0)).astype(BF16)
        part = _dot(a, w2_ref[c * fc:(c + 1) * fc, :])
        acc = part if acc is None else acc + part
    x2 = x1 + g2 * acc
    if final:
        x2 = _rms(x2) * fg_ref[...]
    o_ref[...] = x2


def _post(l, final, o_sb, o_sw, o_fx, o_ssm0, o_ssm1, x, mod, out_gain, norm2_gain, w_out, w1, w2,
          final_gain):
    b, s, d = x.shape
    tm = ROW_TILE
    grp = pl.BlockSpec((None, tm, GROUP_WIDTH), lambda bi, i: (bi, i, 0))
    half = pl.BlockSpec((None, tm, LANES), lambda bi, i: (bi, i, 0))
    xspec = pl.BlockSpec((None, tm, d), lambda bi, i: (bi, i, 0))
    lspec = lambda a: pl.BlockSpec((None,) + a.shape[1:], lambda bi, i: (l, 0, 0),
                                   pipeline_mode=pl.Buffered(1))
    return pl.pallas_call(
        functools.partial(_post_kernel, final=final),
        out_shape=jax.ShapeDtypeStruct((b, s, d), F32),
        grid=(b, s // tm),
        in_specs=[grp, grp, grp, half, half, xspec,
                  pl.BlockSpec((None, None, N_ADA, d), lambda bi, i: (l, bi, 0, 0)),
                  lspec(out_gain), lspec(norm2_gain), lspec(w_out), lspec(w1), lspec(w2),
                  pl.BlockSpec(final_gain.shape, lambda bi, i: (0, 0))],
        out_specs=xspec,
        compiler_params=_params("arbitrary", "arbitrary"),
        name="post_mlp",
    )(o_sb, o_sw, o_fx, o_ssm0, o_ssm1, x, mod, out_gain, norm2_gain, w_out, w1, w2, final_gain)


def _swa_head_perm(a, axis):
    shape = a.shape
    a = a.reshape(shape[:axis] + (4, HEAD_DIM) + shape[axis + 1:])
    a = jnp.take(a, jnp.asarray([0, 2, 1, 3]), axis=axis)
    return a.reshape(shape)


def kernel(x, c, w_ada, b_ada, norm1_gain, norm2_gain, w_in, rel_bias, sinks, forget_bias, lam_re, lam_im, log_dt, ssm_b_re, ssm_b_im, ssm_c_re, ssm_c_im, ssm_d, w_glu, b_glu, out_gain, w_out, w_mlp_in, w_mlp_out, final_gain):
    b, s, d = x.shape
    depth = w_in.shape[0]
    g, p = lam_re.shape[1:]
    h = ssm_b_re.shape[-1]
    n_fox = forget_bias.shape[-1]
    scale = 1.0 / math.sqrt(HEAD_DIM)
    assert s % SSM_TILE == 0 and s % ROW_TILE == 0 and ROW_TILE % SB_TILE == 0 and ROW_TILE % FOX_TILE == 0

    gw, kvw = GROUP_WIDTH, N_KV_SWA * HEAD_DIM
    sizes = (gw, gw, gw, gw, kvw, kvw, gw, gw, gw, n_fox, gw)
    offs = np.concatenate([[0], np.cumsum(sizes)])
    col = lambda k: w_in[:, :, offs[k]:offs[k + 1]]
    w_sbq = (col(0) * scale).astype(BF16)
    w_sbv = col(2).astype(BF16)
    w_swq = (_swa_head_perm(col(3), 2) * scale).astype(BF16)
    w_swv = col(5).astype(BF16)
    w_fxk = col(7).astype(BF16)
    w_u = col(10).astype(BF16)
    wt = jnp.concatenate([col(1), col(6) * scale, col(8), col(4)], axis=2).transpose(0, 2, 1).astype(BF16)
    wf = jnp.pad(col(9), ((0, 0), (0, 0), (0, LANES - n_fox))).astype(BF16)
    forget_b = jnp.pad(forget_bias.astype(F32), ((0, 0), (0, LANES - n_fox)))[:, None, :]
    in_wts = (w_sbq, w_sbv, w_swq, w_swv, w_fxk, w_u, wt, wf)

    og = out_gain.astype(F32)
    og = jnp.concatenate([og[:, :gw], _swa_head_perm(og[:, gw:2 * gw], 1), og[:, 2 * gw:]], axis=1)
    wo = jnp.concatenate([w_out[:, :gw], _swa_head_perm(w_out[:, gw:2 * gw], 1), w_out[:, 2 * gw:]],
                         axis=1).astype(BF16)
    og = og[:, None, :]
    n1 = norm1_gain.astype(F32)[:, None, :]
    n2 = norm2_gain.astype(F32)[:, None, :]
    w1 = w_mlp_in.astype(BF16)
    w2 = w_mlp_out.astype(BF16)
    fg = final_gain.astype(F32)[None, :]
    bias = _swa_bias(rel_bias)
    sinks_b = jnp.broadcast_to(sinks.astype(F32)[:, :, None], sinks.shape + (LANES,))

    avec, bbr, bbi = _ssm_prep(lam_re.astype(F32), lam_im.astype(F32), log_dt.astype(F32),
                               ssm_b_re.astype(F32), ssm_b_im.astype(F32))
    eye = jnp.eye(g, dtype=F32)
    blockdiag_b = lambda bb: jnp.einsum('lhgp,gk->lghkp', bb.reshape(depth, h, g, p), eye
                                        ).reshape(depth, g * h, g * p)
    bmat = jnp.concatenate([blockdiag_b(bbr), blockdiag_b(bbi)], axis=2).astype(BF16)
    blockdiag_c = lambda cc: jnp.einsum('lghp,gk->lgpkh', cc.astype(F32), eye
                                        ).reshape(depth, g * p, g * h)
    cmat = jnp.concatenate([blockdiag_c(ssm_c_re), -blockdiag_c(ssm_c_im)], axis=1).astype(BF16)
    d_skip = ssm_d.astype(F32).reshape(depth, 1, g * h)
    wglu = w_glu.astype(BF16)
    bglu = b_glu.astype(F32)[:, None, :]

    c_pad = jnp.pad(c.astype(F32), ((0, 8 - b), (0, 0)))
    mod = _ada(c_pad, w_ada.astype(F32), b_ada.astype(F32))[:, :b].reshape(depth, b, N_ADA, d)

    x = x.astype(F32)
    for l in range(depth):
        (sbq, sbv, swq, swv, fxk, u0, u1, sbkt, fxqt, fxvt, swkt) = _inproj(l, x, mod, n1, in_wts, forget_b)
        o_sb = _stick_breaking(sbq, sbkt, sbv)
        o_sw = _sliding_window(l, swq, swkt, swv, bias, sinks_b)
        o_fx = _forgetting(fxqt, fxk, fxvt)
        o_ssm0, o_ssm1 = _ssm(l, u0, u1, avec, bmat, cmat, d_skip, wglu, bglu)
        x = _post(l, l == depth - 1, o_sb, o_sw, o_fx, o_ssm0, o_ssm1, x, mod, og, n2, wo, w1, w2, fg)
    return x
```

```python
import functools
import math

import numpy as np
import jax
import jax.numpy as jnp
from jax import lax
from jax.experimental import pallas as pl
from jax.experimental.pallas import tpu as pltpu

F32 = jnp.float32
BF16 = jnp.bfloat16

HEAD_DIM = 64
LANES = 128
GROUP_WIDTH = 256
N_KV_SWA = 2
WINDOW = 128
REL_BUCKETS = 32
REL_MAX_DIST = 128
N_ADA = 6
NORM_EPS = 1e-6
SSM_H = 16
SSM_P = 64
SSM_CHUNK = 16

ROW_TILE = 512
SB_TILE = 256
FOX_TILE = 512
SSM_TILE = 1024
VMEM_LIMIT_BYTES = 56 * 1024 * 1024
EXP_UNDERFLOW = -105.0
FAST_MARGIN = 60.0
FOX_AUG = 8


def _dot(a, b):
    return jnp.dot(a, b, preferred_element_type=F32)


def _dot_nt(a, b):
    return lax.dot_general(a, b, (((1,), (1,)), ((), ())), preferred_element_type=F32)


def _split_bf16(a):
    hi = a.astype(BF16)
    lo = (a - hi.astype(F32)).astype(BF16)
    return hi, lo


def _rms(x):
    return x * lax.rsqrt(jnp.mean(x * x, axis=-1, keepdims=True) + NORM_EPS)


def _params(*sem):
    return pltpu.CompilerParams(dimension_semantics=sem, vmem_limit_bytes=VMEM_LIMIT_BYTES)


def _ada_kernel(c_ref, w_ref, b_ref, o_ref):
    c = c_ref[...]
    ca = c * jax.nn.sigmoid(c)
    a_hi, a_lo = _split_bf16(ca)
    w_hi, w_lo = _split_bf16(w_ref[...])
    o_ref[...] = _dot(a_hi, w_hi) + _dot(a_hi, w_lo) + _dot(a_lo, w_hi) + b_ref[...]


def _ada(c_pad, w_ada, b_ada):
    depth, d, n = w_ada.shape
    tn = 2048
    return pl.pallas_call(
        _ada_kernel,
        out_shape=jax.ShapeDtypeStruct((depth, c_pad.shape[0], n), F32),
        grid=(depth, n // tn),
        in_specs=[
            pl.BlockSpec(c_pad.shape, lambda l, j: (0, 0)),
            pl.BlockSpec((None, d, tn), lambda l, j: (l, 0, j)),
            pl.BlockSpec((None, 1, tn), lambda l, j: (l, 0, j)),
        ],
        out_specs=pl.BlockSpec((None, c_pad.shape[0], tn), lambda l, j: (l, 0, j)),
        compiler_params=_params("arbitrary", "arbitrary"),
        name="ada_mod",
    )(c_pad, w_ada, b_ada.reshape(depth, 1, n))


def _ssm_prep_kernel(lr_ref, li_ref, ldt_ref, br_ref, bi_ref, avec_ref, bbr_ref, bbi_ref):
    lr = lr_ref[...]
    li = li_ref[...]
    dt = jnp.exp(ldt_ref[...])
    mag = jnp.exp(lr * dt)
    ang = li * dt
    ar = mag * jnp.cos(ang)
    ai = mag * jnp.sin(ang)
    den = lr * lr + li * li
    nr = ar - 1.0
    ni = ai
    cr = (nr * lr + ni * li) / den
    ci = (ni * lr - nr * li) / den
    pr, pi_ = ar, ai
    for _ in range(int(math.log2(SSM_CHUNK))):
        pr, pi_ = pr * pr - pi_ * pi_, 2.0 * pr * pi_
    avec_ref[0:1, :] = ar
    avec_ref[1:2, :] = ai
    avec_ref[2:3, :] = pr
    avec_ref[3:4, :] = pi_
    avec_ref[4:8, :] = jnp.zeros((4, ar.shape[1]), F32)
    br = br_ref[...]
    bi = bi_ref[...]
    bbr_ref[...] = cr * br - ci * bi
    bbi_ref[...] = cr * bi + ci * br


def _ssm_prep(lam_re, lam_im, log_dt, b_re, b_im):
    depth, g, p = lam_re.shape
    h = b_re.shape[-1]
    n = g * p
    row = lambda a: a.reshape(depth, 1, n)
    ldt = jnp.broadcast_to(log_dt[:, :, None], (depth, g, p))
    bt = lambda a: a.transpose(0, 3, 1, 2).reshape(depth, h, n)
    vec = pl.BlockSpec((None, 1, n), lambda l: (l, 0, 0))
    mat = pl.BlockSpec((None, h, n), lambda l: (l, 0, 0))
    return pl.pallas_call(
        _ssm_prep_kernel,
        out_shape=(jax.ShapeDtypeStruct((depth, 8, n), F32),
                   jax.ShapeDtypeStruct((depth, h, n), F32),
                   jax.ShapeDtypeStruct((depth, h, n), F32)),
        grid=(depth,),
        in_specs=[vec, vec, vec, mat, mat],
        out_specs=(pl.BlockSpec((None, 8, n), lambda l: (l, 0, 0)), mat, mat),
        compiler_params=_params("arbitrary"),
        name="ssm_prep",
    )(row(lam_re), row(lam_im), row(ldt), bt(b_re), bt(b_im))


def _inproj_kernel(x_ref, mod_ref, gain_ref, w_sbq, w_sbv, w_swk, w_fxk, w_u,
                   wt_ref, wf_ref, fb_ref, sel_ref,
                   sbq_o, sbv_o, swk_o, fxk_o, u0_o, u1_o, sbkt_o, fxqt_o, fxvt_o, swqt_o, swvt_o,
                   carry_ref):
    ts = x_ref.shape[0]
    mod = mod_ref[...]
    sh1 = mod[0:1]
    sc1 = mod[1:2]
    h = (_rms(x_ref[...]) * gain_ref[...] * (1.0 + sc1) + sh1).astype(BF16)

    f = _dot(h, wf_ref[...]) + fb_ref[...]
    logf = jnp.minimum(f, 0.0) - jnp.log1p(jnp.exp(-jnp.abs(f)))
    step = lax.broadcasted_iota(jnp.int32, logf.shape, 0)
    cum = logf
    k = 1
    while k < ts:
        cum = cum + jnp.where(step >= k, pltpu.roll(cum, k, axis=0), 0.0)
        k *= 2

    @pl.when(pl.program_id(1) == 0)
    def _():
        carry_ref[...] = jnp.zeros_like(carry_ref)

    cum = cum + carry_ref[0:1, :]
    carry_ref[...] = jnp.broadcast_to(cum[ts - 1:ts, :], carry_ref.shape)
    negf = -cum
    hi = negf.astype(BF16)
    r1 = negf - hi.astype(F32)
    mid = r1.astype(BF16)
    lo3 = (r1 - mid.astype(F32)).astype(BF16)

    k_fx = _dot(h, w_fxk[...])
    n_pairs = fxk_o.shape[1] // (2 * LANES)
    for p in range(n_pairs):
        base = 2 * p * LANES
        fxk_o[:, base:base + LANES] = k_fx[:, p * LANES:(p + 1) * LANES].astype(BF16)
        aug = _dot(hi, sel_ref[p, 0]) + _dot(mid, sel_ref[p, 1]) + _dot(lo3, sel_ref[p, 2])
        fxk_o[:, base + LANES:base + 2 * LANES] = aug.astype(BF16)

    sbq_o[...] = _dot(h, w_sbq[...]).astype(BF16)
    sbv_o[...] = _dot(h, w_sbv[...]).astype(BF16)
    swk_o[...] = _dot(h, w_swk[...]).astype(BF16)
    u = _dot(h, w_u[...])
    u0_o[...] = u[:, :LANES]
    u1_o[...] = u[:, LANES:]
    tt = _dot_nt(wt_ref[...], h)
    row0 = 0
    for o in (sbkt_o, fxqt_o, fxvt_o, swqt_o, swvt_o):
        n_sub, width, tile = o.shape
        for c in range(n_sub):
            o[c] = tt[row0:row0 + width, c * tile:(c + 1) * tile].astype(BF16)
        row0 += width


def _fox_aug_select():
    n_pairs = GROUP_WIDTH // LANES
    sel = np.zeros((n_pairs, 3, LANES, LANES), np.float32)
    for p in range(n_pairs):
        for j in range(2):
            for r in range(3):
                sel[p, r, 2 * p + j, FOX_AUG * j + r] = 1.0
    return jnp.asarray(sel, BF16)


def _inproj(l, x, mod, gain, wts, forget_b):
    b, s, d = x.shape
    ts = ROW_TILE
    nk = s // ts
    (w_sbq, w_sbv, w_swk, w_fxk, w_u, wt, wf) = wts
    sel = _fox_aug_select()
    row = lambda width: pl.BlockSpec((None, ts, width), lambda bi, i: (bi, i, 0))
    wspec = lambda w: pl.BlockSpec((None,) + w.shape[1:], lambda bi, i: (l, 0, 0))
    tspec = lambda width, tile: pl.BlockSpec((None, ts // tile, width, tile), lambda bi, i: (bi, i, 0, 0))
    kv_w = N_KV_SWA * HEAD_DIM
    out_shape = (
        jax.ShapeDtypeStruct((b, s, GROUP_WIDTH), BF16),
        jax.ShapeDtypeStruct((b, s, GROUP_WIDTH), BF16),
        jax.ShapeDtypeStruct((b, s, kv_w), BF16),
        jax.ShapeDtypeStruct((b, s, 2 * GROUP_WIDTH), BF16),
        jax.ShapeDtypeStruct((b, s, LANES), F32),
        jax.ShapeDtypeStruct((b, s, LANES), F32),
        jax.ShapeDtypeStruct((b, s // SB_TILE, GROUP_WIDTH, SB_TILE), BF16),
        jax.ShapeDtypeStruct((b, s // FOX_TILE, GROUP_WIDTH, FOX_TILE), BF16),
        jax.ShapeDtypeStruct((b, s // FOX_TILE, GROUP_WIDTH, FOX_TILE), BF16),
        jax.ShapeDtypeStruct((b, nk, GROUP_WIDTH, ts), BF16),
        jax.ShapeDtypeStruct((b, nk, kv_w, ts), BF16),
    )
    out_specs = (row(GROUP_WIDTH), row(GROUP_WIDTH), row(kv_w), row(2 * GROUP_WIDTH),
                 row(LANES), row(LANES), tspec(GROUP_WIDTH, SB_TILE), tspec(GROUP_WIDTH, FOX_TILE),
                 tspec(GROUP_WIDTH, FOX_TILE), tspec(GROUP_WIDTH, ts), tspec(kv_w, ts))
    return pl.pallas_call(
        _inproj_kernel,
        out_shape=out_shape,
        grid=(b, nk),
        in_specs=[
            pl.BlockSpec((None, ts, d), lambda bi, i: (bi, i, 0)),
            pl.BlockSpec((None, None, N_ADA, d), lambda bi, i: (l, bi, 0, 0)),
            pl.BlockSpec((None, 1, d), lambda bi, i: (l, 0, 0)),
            wspec(w_sbq), wspec(w_sbv), wspec(w_swk), wspec(w_fxk), wspec(w_u),
            wspec(wt), wspec(wf),
            pl.BlockSpec((None, 1, LANES), lambda bi, i: (l, 0, 0)),
            pl.BlockSpec(sel.shape, lambda bi, i: (0, 0, 0, 0)),
        ],
        out_specs=out_specs,
        scratch_shapes=[pltpu.VMEM((8, LANES), F32)],
        compiler_params=_params("arbitrary", "arbitrary"),
        name="in_proj",
    )(x, mod, gain, w_sbq, w_sbv, w_swk, w_fxk, w_u, wt, wf, forget_b, sel)


def _pair_split(q):
    lane = lax.broadcasted_iota(jnp.int32, q.shape, 1)
    left = lane < HEAD_DIM
    zero = jnp.zeros_like(q)
    return left, (jnp.where(left, q, zero), jnp.where(left, zero, q))


def _sb_kernel(q_ref, kt_ref, v_ref, suf_ref, o_ref):
    t = q_ref.shape[0]
    n_pairs = q_ref.shape[1] // LANES
    qi = pl.program_id(1)
    split = [_pair_split(q_ref[:, p * LANES:(p + 1) * LANES]) for p in range(n_pairs)]
    left = split[0][0]
    row = lax.broadcasted_iota(jnp.int32, (t, t), 0)
    col = lax.broadcasted_iota(jnp.int32, (t, t), 1)
    strict = col < row
    suf = suf_ref[...]

    def tile(ki, carry, diag):
        run, acc = carry
        rows = pl.ds(pl.multiple_of(ki * t, t), t)
        heads = [(p, j) for p in range(n_pairs) for j in range(2)]
        z = [_dot(split[p][1][j], kt_ref[ki, p * LANES:(p + 1) * LANES, :]) for p, j in heads]
        lsz, lk = [], []
        for zh in z:
            sp = jnp.log(1.0 + jnp.exp(-jnp.abs(zh)))
            lszh = jnp.minimum(zh, 0.0) - sp
            lkh = lszh - zh
            lsz.append(lszh)
            lk.append(jnp.where(strict, lkh, 0.0) if diag else lkh)
        within = [_dot(lkh.astype(BF16), suf) for lkh in lk]
        w = []
        for h in range(len(heads)):
            wh = jnp.exp(lsz[h] + within[h] + run[h])
            w.append((jnp.where(strict, wh, 0.0) if diag else wh).astype(BF16))
        pv = [_dot(w[h], v_ref[rows, p * LANES:(p + 1) * LANES]) for h, (p, j) in enumerate(heads)]
        new_run = tuple(run[h] + jnp.sum(lk[h], axis=-1, keepdims=True) for h in range(len(heads)))
        new_acc = tuple(acc[p] + jnp.where(left, pv[2 * p], pv[2 * p + 1]) for p in range(n_pairs))
        return new_run, new_acc

    def alive(run):
        top = run[0]
        for r in run[1:]:
            top = jnp.maximum(top, r)
        return jnp.max(top) > EXP_UNDERFLOW

    zero = jnp.zeros((t, 1), F32)
    state = tile(qi, ((zero,) * (2 * n_pairs), (jnp.zeros((t, LANES), F32),) * n_pairs), True)

    def body(c):
        i, _, st = c
        st = tile(qi - 1 - i, st, False)
        return i + 1, alive(st[0]), st

    _, _, state = lax.while_loop(lambda c: jnp.logical_and(c[0] < qi, c[1]), body,
                                 (jnp.int32(0), alive(state[0]), state))
    for p in range(n_pairs):
        o_ref[:, p * LANES:(p + 1) * LANES] = state[1][p]


def _fox_kernel(qt_ref, k_ref, vt_ref, o_ref, bound_ref):
    t = qt_ref.shape[1]
    nk = vt_ref.shape[0]
    qi = pl.program_id(2)
    qt = qt_ref[...]
    sub = lax.broadcasted_iota(jnp.int32, (LANES, t), 0)
    top = sub < HEAD_DIM
    zero = jnp.zeros_like(qt)
    q2 = qt.astype(F32) * qt.astype(F32)
    qa, qn = [], []
    for j in range(2):
        mine = top if j == 0 else jnp.logical_not(top)
        ones = jnp.where(jnp.logical_and(sub >= FOX_AUG * j, sub < FOX_AUG * j + 3), 1.0, 0.0)
        qa.append(jnp.concatenate([jnp.where(mine, qt, zero), ones.astype(BF16)], axis=0))
        qn.append(jnp.sqrt(jnp.sum(jnp.where(mine, q2, 0.0), axis=0, keepdims=True)))
    key = lax.broadcasted_iota(jnp.int32, (t, t), 0)
    qry = lax.broadcasted_iota(jnp.int32, (t, t), 1)
    causal = key <= qry
    lane = lax.broadcasted_iota(jnp.int32, (t, LANES), 1)

    @pl.when(qi == 0)
    def _():
        def scan(ki, c):
            kk = k_ref[pl.ds(pl.multiple_of(ki * t, t), t), :].astype(F32)
            k2 = kk[:, :LANES] * kk[:, :LANES]
            aug = kk[:, LANES:]
            out_kn, out_nf = [], []
            for j in range(2):
                mine = (lane < HEAD_DIM) if j == 0 else (lane >= HEAD_DIM)
                n2 = jnp.sum(jnp.where(mine, k2, 0.0), axis=1, keepdims=True)
                here = jnp.logical_and(lane >= FOX_AUG * j, lane < FOX_AUG * (j + 1))
                negf = jnp.sum(jnp.where(here, aug, 0.0), axis=1, keepdims=True)
                kn = jnp.maximum(c[j], jnp.sqrt(jnp.max(n2)))
                nf = jnp.maximum(c[2 + j], jnp.max(negf))
                bound_ref[j, ki] = kn
                bound_ref[2 + j, ki] = nf
                out_kn.append(kn)
                out_nf.append(nf)
            return tuple(out_kn + out_nf)

        lax.fori_loop(0, nk, scan, (jnp.float32(0.0),) * 2 + (jnp.float32(-jnp.inf),) * 2)

    def tile(ki, carry, mode):
        m, l, acc = carry
        kk = k_ref[pl.ds(pl.multiple_of(ki * t, t), t), :]
        vt = vt_ref[ki]
        s = [_dot(kk, qa[j]) for j in range(2)]
        if mode == "diag":
            s = [jnp.where(causal, sj, -jnp.inf) for sj in s]
        if mode == "fast":
            new_m = m
        else:
            new_m = tuple(jnp.maximum(m[j], jnp.max(s[j], axis=0, keepdims=True)) for j in range(2))
        e = [jnp.exp(s[j] - new_m[j]) for j in range(2)]
        pv = [_dot(vt, e[j].astype(BF16)) for j in range(2)]
        colsum = [jnp.sum(e[j], axis=0, keepdims=True) for j in range(2)]
        if mode == "fast":
            new_l = tuple(l[j] + colsum[j] for j in range(2))
        else:
            corr = [jnp.exp(m[j] - new_m[j]) for j in range(2)]
            new_l = tuple(l[j] * corr[j] + colsum[j] for j in range(2))
            acc = acc * jnp.where(top, corr[0], corr[1])
        return new_m, new_l, acc + jnp.where(top, pv[0], pv[1])

    def row_gap(m, ki):
        kic = jnp.maximum(ki, 0)
        return tuple(jnp.max(qn[j] * bound_ref[j, kic] - m[j]) for j in range(2))

    def gap(rg, ki):
        kic = jnp.maximum(ki, 0)
        return jnp.maximum(rg[0] + bound_ref[2, kic], rg[1] + bound_ref[3, kic])

    ninf = jnp.full((1, t), -jnp.inf, F32)
    zrow = jnp.zeros((1, t), F32)
    state = tile(qi, ((ninf, ninf), (zrow, zrow), jnp.zeros((LANES, t), F32)), "diag")

    def general(c):
        i, _, st = c
        ki = qi - 1 - i
        st = tile(ki, st, "general")
        return i + 1, gap(row_gap(st[0], ki - 1), ki - 1), st

    i0, _, (m, l, acc) = lax.while_loop(
        lambda c: jnp.logical_and(c[0] < qi, c[1] >= FAST_MARGIN), general,
        (jnp.int32(0), gap(row_gap(state[0], qi - 1), qi - 1), state))
    rg = row_gap(m, qi - 1 - i0)

    def fast(c):
        i, _, l, acc = c
        ki = qi - 1 - i
        _, l, acc = tile(ki, (m, l, acc), "fast")
        return i + 1, gap(rg, ki - 1), l, acc

    _, _, l, acc = lax.while_loop(
        lambda c: jnp.logical_and(c[0] < qi, c[1] > EXP_UNDERFLOW), fast,
        (i0, gap(rg, qi - 1 - i0), l, acc))
    o_ref[...] = (acc / jnp.where(top, l[0], l[1])).T


def _stick_breaking(q, kt, v):
    b, s, width = q.shape
    t = SB_TILE
    nk = s // t
    suffix = jnp.asarray(np.tril(np.ones((t, t), np.float32), -1), BF16)
    once = pl.Buffered(1)
    return pl.pallas_call(
        _sb_kernel,
        out_shape=jax.ShapeDtypeStruct((b, s, width), F32),
        grid=(b, nk),
        in_specs=[
            pl.BlockSpec((None, t, width), lambda bi, i: (bi, i, 0)),
            pl.BlockSpec((None, nk, width, t), lambda bi, i: (bi, 0, 0, 0), pipeline_mode=once),
            pl.BlockSpec((None, s, width), lambda bi, i: (bi, 0, 0), pipeline_mode=once),
            pl.BlockSpec((t, t), lambda bi, i: (0, 0), pipeline_mode=once),
        ],
        out_specs=pl.BlockSpec((None, t, width), lambda bi, i: (bi, i, 0)),
        compiler_params=_params("arbitrary", "arbitrary"),
        name="stick_breaking",
    )(q, kt, v, suffix)


def _forgetting(qt, k, vt):
    b, nq, width, t = qt.shape
    s = nq * t
    return pl.pallas_call(
        _fox_kernel,
        out_shape=jax.ShapeDtypeStruct((b, s, width), F32),
        grid=(b, width // LANES, nq),
        in_specs=[
            pl.BlockSpec((None, None, LANES, t), lambda bi, p, i: (bi, i, p, 0)),
            pl.BlockSpec((None, s, 2 * LANES), lambda bi, p, i: (bi, 0, p)),
            pl.BlockSpec((None, nq, LANES, t), lambda bi, p, i: (bi, 0, p, 0)),
        ],
        out_specs=pl.BlockSpec((None, t, LANES), lambda bi, p, i: (bi, i, p)),
        scratch_shapes=[pltpu.SMEM((4, nq), F32)],
        compiler_params=_params("arbitrary", "arbitrary", "arbitrary"),
        name="forgetting",
    )(qt, k, vt)


def _swa_kernel(qt_ref, kc_ref, kp_ref, vtc_ref, vtp_ref, bias_ref, sink_ref, o_ref):
    r = kc_ref.shape[0]
    first = pl.program_id(1) == 0
    sub = lax.broadcasted_iota(jnp.int32, (LANES, WINDOW), 0)
    top = sub < HEAD_DIM
    key = lax.broadcasted_iota(jnp.int32, (2 * WINDOW, WINDOW), 0)
    n_win = r // WINDOW
    ks, vts = [], []
    for w in range(n_win):
        lo = w * WINDOW
        if w == 0:
            ks.append(jnp.concatenate([kp_ref[...], kc_ref[:WINDOW, :]], axis=0))
            vts.append(jnp.concatenate([vtp_ref[0, :, r - WINDOW:], vtc_ref[0, :, :WINDOW]], axis=1))
        else:
            ks.append(kc_ref[lo - WINDOW:lo + WINDOW, :])
            vts.append(vtc_ref[0, :, lo - WINDOW:lo + WINDOW])
    units = [(w, g, kv) for w in range(n_win) for g in range(2) for kv in range(N_KV_SWA)]
    z = []
    for w, g, kv in units:
        qg = qt_ref[0, g * LANES:(g + 1) * LANES, w * WINDOW:(w + 1) * WINDOW]
        zero = jnp.zeros_like(qg)
        qm = jnp.where(top, qg, zero) if kv == 0 else jnp.where(top, zero, qg)
        zu = _dot(ks[w], qm) + bias_ref[2 * kv + g]
        if w == 0:
            zu = jnp.where(jnp.logical_and(first, key < WINDOW), -jnp.inf, zu)
        z.append(zu)
    p = []
    for (w, g, kv), zu in zip(units, z):
        head = 2 * kv + g
        sink = sink_ref[head:head + 1, 0:1]
        m = jnp.maximum(jnp.max(zu, axis=0, keepdims=True), sink)
        e = jnp.exp(zu - m)
        den = jnp.sum(e, axis=0, keepdims=True) + jnp.exp(sink - m)
        p.append((e * (1.0 / den)).astype(BF16))
    pv = {u: _dot(vts[u[0]], pu) for u, pu in zip(units, p)}
    for w in range(n_win):
        for g in range(2):
            o_ref[w * WINDOW:(w + 1) * WINDOW, g * LANES:(g + 1) * LANES] = jnp.where(
                top, pv[(w, g, 0)], pv[(w, g, 1)]).T


def _sliding_window(l, qt, k, vt, bias_t, sinks):
    b, s, kv_w = k.shape
    r = ROW_TILE
    wpt = r // WINDOW
    return pl.pallas_call(
        _swa_kernel,
        out_shape=jax.ShapeDtypeStruct((b, s, GROUP_WIDTH), F32),
        grid=(b, s // r),
        in_specs=[
            pl.BlockSpec((None, 1, GROUP_WIDTH, r), lambda bi, i: (bi, i, 0, 0)),
            pl.BlockSpec((None, r, kv_w), lambda bi, i: (bi, i, 0)),
            pl.BlockSpec((None, WINDOW, kv_w), lambda bi, i: (bi, jnp.maximum(i * wpt - 1, 0), 0)),
            pl.BlockSpec((None, 1, kv_w, r), lambda bi, i: (bi, i, 0, 0)),
            pl.BlockSpec((None, 1, kv_w, r), lambda bi, i: (bi, jnp.maximum(i - 1, 0), 0, 0)),
            pl.BlockSpec(bias_t.shape, lambda bi, i: (0, 0, 0)),
            pl.BlockSpec((None,) + sinks.shape[1:], lambda bi, i: (l, 0, 0)),
        ],
        out_specs=pl.BlockSpec((None, r, GROUP_WIDTH), lambda bi, i: (bi, i, 0)),
        compiler_params=_params("arbitrary", "arbitrary"),
        name="sliding_window",
    )(qt, k, k, vt, vt, bias_t, sinks)


def _swa_bias(rel_bias):
    i = np.arange(WINDOW)[:, None]
    j = np.arange(2 * WINDOW)[None, :]
    dist = WINDOW + i - j
    in_window = (dist >= 0) & (dist < WINDOW)
    d = np.clip(dist, 0, None)
    max_exact = REL_BUCKETS // 2
    safe = np.maximum(d, 1).astype(np.float32)
    large = max_exact + (np.log(safe / max_exact) / math.log(REL_MAX_DIST / max_exact)
                         * (REL_BUCKETS - max_exact)).astype(np.int32)
    large = np.minimum(large, REL_BUCKETS - 1)
    bucket = np.where(d < max_exact, d, large).astype(np.int32)
    onehot = jnp.asarray(np.eye(REL_BUCKETS, dtype=np.float32)[bucket.reshape(-1)])
    bias = jnp.dot(onehot, rel_bias.astype(F32), precision=lax.Precision.HIGHEST)
    bias = bias.T.reshape(-1, WINDOW, 2 * WINDOW)
    return jnp.where(jnp.asarray(in_window)[None], bias, -jnp.inf).transpose(0, 2, 1)


def _ssm_kernel(u0_ref, u1_ref, avec_ref, bmat_ref, cmat_ref, d_ref, wglu_ref, bglu_ref,
                o0_ref, o1_ref, up_ref, xr_ref, xi_ref, er_ref, ei_ref, sr_ref, si_ref, carry_ref):
    n_slab, ts, _ = xr_ref.shape
    n = ts // SSM_CHUNK
    np_ = n_slab * LANES

    @pl.when(pl.program_id(1) == 0)
    def _():
        carry_ref[...] = jnp.zeros_like(carry_ref)

    def rows(i):
        return pl.ds(i * n, n)

    for k, u_ref in enumerate((u0_ref, u1_ref)):
        for i in range(SSM_CHUNK):
            up_ref[rows(i), k * LANES:(k + 1) * LANES] = u_ref[pl.ds(i, n, stride=SSM_CHUNK), :]
    u = up_ref[...]
    bu = _dot(u.astype(BF16), bmat_ref[...])
    for j in range(n_slab):
        xr_ref[j] = bu[:, j * LANES:(j + 1) * LANES]
        xi_ref[j] = bu[:, np_ + j * LANES:np_ + (j + 1) * LANES]

    def coef(k, j):
        return avec_ref[k:k + 1, j * LANES:(j + 1) * LANES]

    for j in range(n_slab):
        ar, ai = coef(0, j), coef(1, j)
        pr, pi_ = xr_ref[j, rows(0), :], xi_ref[j, rows(0), :]
        for i in range(1, SSM_CHUNK):
            nr = ar * pr - ai * pi_ + xr_ref[j, rows(i), :]
            ni = ar * pi_ + ai * pr + xi_ref[j, rows(i), :]
            xr_ref[j, rows(i), :] = nr
            xi_ref[j, rows(i), :] = ni
            pr, pi_ = nr, ni
        er_ref[j] = pr
        ei_ref[j] = pi_

    a16 = [(coef(2, j), coef(3, j)) for j in range(n_slab)]

    def chunk(c, s):
        out = []
        for j in range(n_slab):
            sr, si = s[j]
            sr_ref[j, pl.ds(c, 1), :] = sr
            si_ref[j, pl.ds(c, 1), :] = si
            er = er_ref[j, pl.ds(c, 1), :]
            ei = ei_ref[j, pl.ds(c, 1), :]
            a16r, a16i = a16[j]
            out.append((a16r * sr - a16i * si + er, a16r * si + a16i * sr + ei))
        return tuple(out)

    init = tuple((carry_ref[0:1, j * LANES:(j + 1) * LANES], carry_ref[1:2, j * LANES:(j + 1) * LANES])
                 for j in range(n_slab))
    fin = lax.fori_loop(0, n, chunk, init)
    for j in range(n_slab):
        carry_ref[0:1, j * LANES:(j + 1) * LANES] = fin[j][0]
        carry_ref[1:2, j * LANES:(j + 1) * LANES] = fin[j][1]

    for j in range(n_slab):
        ar, ai = coef(0, j), coef(1, j)
        pr, pi_ = sr_ref[j], si_ref[j]
        for i in range(SSM_CHUNK):
            pr, pi_ = ar * pr - ai * pi_, ar * pi_ + ai * pr
            xr_ref[j, rows(i), :] = xr_ref[j, rows(i), :] + pr
            xi_ref[j, rows(i), :] = xi_ref[j, rows(i), :] + pi_

    xr = jnp.concatenate([xr_ref[j].astype(BF16) for j in range(n_slab)], axis=1)
    xi = jnp.concatenate([xi_ref[j].astype(BF16) for j in range(n_slab)], axis=1)
    y = _dot(xr, cmat_ref[0:np_, :]) + _dot(xi, cmat_ref[np_:, :]) + d_ref[...] * u
    y = 0.5 * y * (1.0 + jnp.tanh(math.sqrt(2.0 / math.pi) * (y + 0.044715 * (y * y * y))))
    gate = jax.nn.sigmoid(_dot(y.astype(BF16), wglu_ref[...]) + bglu_ref[...])
    out = y * gate
    for k, o_ref in enumerate((o0_ref, o1_ref)):
        for i in range(SSM_CHUNK):
            o_ref[pl.ds(i, n, stride=SSM_CHUNK), :] = out[i * n:(i + 1) * n, k * LANES:(k + 1) * LANES]


def _ssm(l, u0, u1, avec, bmat, cmat, d_skip, w_glu, b_glu):
    b, s, hw = u0.shape
    ts = min(SSM_TILE, s)
    np_ = avec.shape[-1]
    n = ts // SSM_CHUNK
    n_slab = np_ // LANES
    lspec = lambda a: pl.BlockSpec((None,) + a.shape[1:], lambda bi, i: (l, 0, 0))
    half = pl.BlockSpec((None, ts, hw), lambda bi, i: (bi, i, 0))
    return pl.pallas_call(
        _ssm_kernel,
        out_shape=(jax.ShapeDtypeStruct((b, s, hw), F32), jax.ShapeDtypeStruct((b, s, hw), F32)),
        grid=(b, s // ts),
        in_specs=[half, half,
                  lspec(avec), lspec(bmat), lspec(cmat), lspec(d_skip), lspec(w_glu), lspec(b_glu)],
        out_specs=(half, half),
        scratch_shapes=[pltpu.VMEM((ts, 2 * hw), F32),
                        pltpu.VMEM((n_slab, ts, LANES), F32), pltpu.VMEM((n_slab, ts, LANES), F32),
                        pltpu.VMEM((n_slab, n, LANES), F32), pltpu.VMEM((n_slab, n, LANES), F32),
                        pltpu.VMEM((n_slab, n, LANES), F32), pltpu.VMEM((n_slab, n, LANES), F32),
                        pltpu.VMEM((8, np_), F32)],
        compiler_params=_params("arbitrary", "arbitrary"),
        name="s5_ssm",
    )(u0, u1, avec, bmat, cmat, d_skip, w_glu, b_glu)


def _post_kernel(osb_ref, osw_ref, ofx_ref, ossm0_ref, ossm1_ref, x_ref, mod_ref, og_ref, n2_ref,
                 wout_ref, w1_ref, w2_ref, fg_ref, o_ref, *, final):
    mod = mod_ref[...]
    g1, sh2, sc2, g2 = mod[2:3], mod[3:4], mod[4:5], mod[5:6]
    ssm = jnp.concatenate([ossm0_ref[...], ossm1_ref[...]], axis=1)
    mo = None
    for k, o in enumerate((osb_ref[...], osw_ref[...], ofx_ref[...], ssm)):
        lo = k * GROUP_WIDTH
        y = (_rms(o) * og_ref[:, lo:lo + GROUP_WIDTH]).astype(BF16)
        part = _dot(y, wout_ref[lo:lo + GROUP_WIDTH, :])
        mo = part if mo is None else mo + part
    x1 = x_ref[...] + g1 * mo
    h = (_rms(x1) * n2_ref[...] * (1.0 + sc2) + sh2).astype(BF16)
    d_ff = w1_ref.shape[1]
    fc = 1024
    acc = None
    for c in range(d_ff // fc):
        a = _dot(h, w1_ref[:, c * fc:(c + 1) * fc])
        a = jnp.square(jnp.maximum(a, 0.0)).astype(BF16)
        part = _dot(a, w2_ref[c * fc:(c + 1) * fc, :])
        acc = part if acc is None else acc + part
    x2 = x1 + g2 * acc
    if final:
        x2 = _rms(x2) * fg_ref[...]
    o_ref[...] = x2


def _post(l, final, o_sb, o_sw, o_fx, o_ssm0, o_ssm1, x, mod, out_gain, norm2_gain, w_out, w1, w2,
          final_gain):
    b, s, d = x.shape
    tm = ROW_TILE
    grp = pl.BlockSpec((None, tm, GROUP_WIDTH), lambda bi, i: (bi, i, 0))
    half = pl.BlockSpec((None, tm, LANES), lambda bi, i: (bi, i, 0))
    xspec = pl.BlockSpec((None, tm, d), lambda bi, i: (bi, i, 0))
    lspec = lambda a: pl.BlockSpec((None,) + a.shape[1:], lambda bi, i: (l, 0, 0),
                                   pipeline_mode=pl.Buffered(1))
    return pl.pallas_call(
        functools.partial(_post_kernel, final=final),
        out_shape=jax.ShapeDtypeStruct((b, s, d), F32),
        grid=(b, s // tm),
        in_specs=[grp, grp, grp, half, half, xspec,
                  pl.BlockSpec((None, None, N_ADA, d), lambda bi, i: (l, bi, 0, 0)),
                  lspec(out_gain), lspec(norm2_gain), lspec(w_out), lspec(w1), lspec(w2),
                  pl.BlockSpec(final_gain.shape, lambda bi, i: (0, 0))],
        out_specs=xspec,
        compiler_params=_params("arbitrary", "arbitrary"),
        name="post_mlp",
    )(o_sb, o_sw, o_fx, o_ssm0, o_ssm1, x, mod, out_gain, norm2_gain, w_out, w1, w2, final_gain)


def _swa_head_perm(a, axis):
    shape = a.shape
    a = a.reshape(shape[:axis] + (4, HEAD_DIM) + shape[axis + 1:])
    a = jnp.take(a, jnp.asarray([0, 2, 1, 3]), axis=axis)
    return a.reshape(shape)


def kernel(x, c, w_ada, b_ada, norm1_gain, norm2_gain, w_in, rel_bias, sinks, forget_bias, lam_re, lam_im, log_dt, ssm_b_re, ssm_b_im, ssm_c_re, ssm_c_im, ssm_d, w_glu, b_glu, out_gain, w_out, w_mlp_in, w_mlp_out, final_gain):
    b, s, d = x.shape
    depth = w_in.shape[0]
    g, p = lam_re.shape[1:]
    h = ssm_b_re.shape[-1]
    n_fox = forget_bias.shape[-1]
    scale = 1.0 / math.sqrt(HEAD_DIM)
    assert s % SSM_TILE == 0 and s % ROW_TILE == 0 and ROW_TILE % SB_TILE == 0 and ROW_TILE % FOX_TILE == 0

    gw, kvw = GROUP_WIDTH, N_KV_SWA * HEAD_DIM
    sizes = (gw, gw, gw, gw, kvw, kvw, gw, gw, gw, n_fox, gw)
    offs = np.concatenate([[0], np.cumsum(sizes)])
    col = lambda k: w_in[:, :, offs[k]:offs[k + 1]]
    w_sbq = (col(0) * scale).astype(BF16)
    w_sbv = col(2).astype(BF16)
    w_swk = col(4).astype(BF16)
    w_fxk = col(7).astype(BF16)
    w_u = col(10).astype(BF16)
    wt = jnp.concatenate([col(1), col(6) * scale, col(8), _swa_head_perm(col(3), 2) * scale, col(5)],
                         axis=2).transpose(0, 2, 1).astype(BF16)
    wf = jnp.pad(col(9), ((0, 0), (0, 0), (0, LANES - n_fox))).astype(BF16)
    forget_b = jnp.pad(forget_bias.astype(F32), ((0, 0), (0, LANES - n_fox)))[:, None, :]
    in_wts = (w_sbq, w_sbv, w_swk, w_fxk, w_u, wt, wf)

    og = out_gain.astype(F32)
    og = jnp.concatenate([og[:, :gw], _swa_head_perm(og[:, gw:2 * gw], 1), og[:, 2 * gw:]], axis=1)
    wo = jnp.concatenate([w_out[:, :gw], _swa_head_perm(w_out[:, gw:2 * gw], 1), w_out[:, 2 * gw:]],
                         axis=1).astype(BF16)
    og = og[:, None, :]
    n1 = norm1_gain.astype(F32)[:, None, :]
    n2 = norm2_gain.astype(F32)[:, None, :]
    w1 = w_mlp_in.astype(BF16)
    w2 = w_mlp_out.astype(BF16)
    fg = final_gain.astype(F32)[None, :]
    bias = _swa_bias(rel_bias)
    sinks_b = jnp.broadcast_to(sinks.astype(F32)[:, :, None], sinks.shape + (LANES,))

    avec, bbr, bbi = _ssm_prep(lam_re.astype(F32), lam_im.astype(F32), log_dt.astype(F32),
                               ssm_b_re.astype(F32), ssm_b_im.astype(F32))
    eye = jnp.eye(g, dtype=F32)
    blockdiag_b = lambda bb: jnp.einsum('lhgp,gk->lghkp', bb.reshape(depth, h, g, p), eye
                                        ).reshape(depth, g * h, g * p)
    bmat = jnp.concatenate([blockdiag_b(bbr), blockdiag_b(bbi)], axis=2).astype(BF16)
    blockdiag_c = lambda cc: jnp.einsum('lghp,gk->lgpkh', cc.astype(F32), eye
                                        ).reshape(depth, g * p, g * h)
    cmat = jnp.concatenate([blockdiag_c(ssm_c_re), -blockdiag_c(ssm_c_im)], axis=1).astype(BF16)
    d_skip = ssm_d.astype(F32).reshape(depth, 1, g * h)
    wglu = w_glu.astype(BF16)
    bglu = b_glu.astype(F32)[:, None, :]

    c_pad = jnp.pad(c.astype(F32), ((0, 8 - b), (0, 0)))
    mod = _ada(c_pad, w_ada.astype(F32), b_ada.astype(F32))[:, :b].reshape(depth, b, N_ADA, d)

    x = x.astype(F32)
    for l in range(depth):
        (sbq, sbv, swk, fxk, u0, u1, sbkt, fxqt, fxvt, swqt, swvt) = _inproj(l, x, mod, n1, in_wts, forget_b)
        o_sb = _stick_breaking(sbq, sbkt, sbv)
        o_sw = _sliding_window(l, swqt, swk, swvt, bias, sinks_b)
        o_fx = _forgetting(fxqt, fxk, fxvt)
        o_ssm0, o_ssm1 = _ssm(l, u0, u1, avec, bmat, cmat, d_skip, wglu, bglu)
        x = _post(l, l == depth - 1, o_sb, o_sw, o_fx, o_ssm0, o_ssm1, x, mod, og, n2, wo, w1, w2, fg)
    return x
```

```python
import functools
import math

import numpy as np
import jax
import jax.numpy as jnp
from jax import lax
from jax.experimental import pallas as pl
from jax.experimental.pallas import tpu as pltpu

F32 = jnp.float32
BF16 = jnp.bfloat16

HEAD_DIM = 64
LANES = 128
GROUP_WIDTH = 256
N_KV_SWA = 2
WINDOW = 128
REL_BUCKETS = 32
REL_MAX_DIST = 128
N_ADA = 6
NORM_EPS = 1e-6
SSM_H = 16
SSM_P = 64
SSM_CHUNK = 16

ROW_TILE = 512
INPROJ_SPLIT = 2
SB_TILE = 256
FOX_TILE = 512
SSM_TILE = 1024
VMEM_LIMIT_BYTES = 56 * 1024 * 1024
EXP_UNDERFLOW = -105.0
FAST_MARGIN = 60.0
FOX_AUG = 8


def _dot(a, b):
    return jnp.dot(a, b, preferred_element_type=F32)


def _dot_nt(a, b):
    return lax.dot_general(a, b, (((1,), (1,)), ((), ())), preferred_element_type=F32)


def _split_bf16(a):
    hi = a.astype(BF16)
    lo = (a - hi.astype(F32)).astype(BF16)
    return hi, lo


def _rms(x):
    return x * lax.rsqrt(jnp.mean(x * x, axis=-1, keepdims=True) + NORM_EPS)


def _params(*sem):
    return pltpu.CompilerParams(dimension_semantics=sem, vmem_limit_bytes=VMEM_LIMIT_BYTES)


def _ada_kernel(c_ref, w_ref, b_ref, o_ref):
    c = c_ref[...]
    ca = c * jax.nn.sigmoid(c)
    a_hi, a_lo = _split_bf16(ca)
    w_hi, w_lo = _split_bf16(w_ref[...])
    o_ref[...] = _dot(a_hi, w_hi) + _dot(a_hi, w_lo) + _dot(a_lo, w_hi) + b_ref[...]


def _ada(c_pad, w_ada, b_ada):
    depth, d, n = w_ada.shape
    tn = 2048
    return pl.pallas_call(
        _ada_kernel,
        out_shape=jax.ShapeDtypeStruct((depth, c_pad.shape[0], n), F32),
        grid=(depth, n // tn),
        in_specs=[
            pl.BlockSpec(c_pad.shape, lambda l, j: (0, 0)),
            pl.BlockSpec((None, d, tn), lambda l, j: (l, 0, j)),
            pl.BlockSpec((None, 1, tn), lambda l, j: (l, 0, j)),
        ],
        out_specs=pl.BlockSpec((None, c_pad.shape[0], tn), lambda l, j: (l, 0, j)),
        compiler_params=_params("arbitrary", "arbitrary"),
        name="ada_mod",
    )(c_pad, w_ada, b_ada.reshape(depth, 1, n))


def _ssm_prep_kernel(lr_ref, li_ref, ldt_ref, br_ref, bi_ref, avec_ref, bbr_ref, bbi_ref):
    lr = lr_ref[...]
    li = li_ref[...]
    dt = jnp.exp(ldt_ref[...])
    mag = jnp.exp(lr * dt)
    ang = li * dt
    ar = mag * jnp.cos(ang)
    ai = mag * jnp.sin(ang)
    den = lr * lr + li * li
    nr = ar - 1.0
    ni = ai
    cr = (nr * lr + ni * li) / den
    ci = (ni * lr - nr * li) / den
    pr, pi_ = ar, ai
    for _ in range(int(math.log2(SSM_CHUNK))):
        pr, pi_ = pr * pr - pi_ * pi_, 2.0 * pr * pi_
    avec_ref[0:1, :] = ar
    avec_ref[1:2, :] = ai
    avec_ref[2:3, :] = pr
    avec_ref[3:4, :] = pi_
    avec_ref[4:8, :] = jnp.zeros((4, ar.shape[1]), F32)
    br = br_ref[...]
    bi = bi_ref[...]
    bbr_ref[...] = cr * br - ci * bi
    bbi_ref[...] = cr * bi + ci * br


def _ssm_prep(lam_re, lam_im, log_dt, b_re, b_im):
    depth, g, p = lam_re.shape
    h = b_re.shape[-1]
    n = g * p
    row = lambda a: a.reshape(depth, 1, n)
    ldt = jnp.broadcast_to(log_dt[:, :, None], (depth, g, p))
    bt = lambda a: a.transpose(0, 3, 1, 2).reshape(depth, h, n)
    vec = pl.BlockSpec((None, 1, n), lambda l: (l, 0, 0))
    mat = pl.BlockSpec((None, h, n), lambda l: (l, 0, 0))
    return pl.pallas_call(
        _ssm_prep_kernel,
        out_shape=(jax.ShapeDtypeStruct((depth, 8, n), F32),
                   jax.ShapeDtypeStruct((depth, h, n), F32),
                   jax.ShapeDtypeStruct((depth, h, n), F32)),
        grid=(depth,),
        in_specs=[vec, vec, vec, mat, mat],
        out_specs=(pl.BlockSpec((None, 8, n), lambda l: (l, 0, 0)), mat, mat),
        compiler_params=_params("arbitrary"),
        name="ssm_prep",
    )(row(lam_re), row(lam_im), row(ldt), bt(b_re), bt(b_im))


def _inproj_kernel(x_ref, mod_ref, gain_ref, w_sbq, w_sbv, w_swk, w_fxk, w_u,
                   wt_ref, wf_ref, fb_ref, sel_ref,
                   sbq_o, sbv_o, swk_o, fxk_o, u0_o, u1_o, sbkt_o, fxqt_o, fxvt_o, swqt_o, swvt_o,
                   carry_ref):
    hr = x_ref.shape[0] // INPROJ_SPLIT

    @pl.when(pl.program_id(1) == 0)
    def _():
        carry_ref[...] = jnp.zeros_like(carry_ref)

    mod = mod_ref[...]
    sh1 = mod[0:1]
    sc1 = mod[1:2]
    gain = gain_ref[...]
    hs = [(_rms(x_ref[r * hr:(r + 1) * hr, :]) * gain * (1.0 + sc1) + sh1).astype(BF16)
          for r in range(INPROJ_SPLIT)]
    step = lax.broadcasted_iota(jnp.int32, (hr, LANES), 0)
    carry = carry_ref[0:1, :]
    for r, h in enumerate(hs):
        rows = slice(r * hr, (r + 1) * hr)
        f = _dot(h, wf_ref[...]) + fb_ref[...]
        logf = jnp.minimum(f, 0.0) - jnp.log1p(jnp.exp(-jnp.abs(f)))
        cum = logf
        k = 1
        while k < hr:
            cum = cum + jnp.where(step >= k, pltpu.roll(cum, k, axis=0), 0.0)
            k *= 2
        cum = cum + carry
        carry = cum[hr - 1:hr, :]
        negf = -cum
        hi = negf.astype(BF16)
        r1 = negf - hi.astype(F32)
        mid = r1.astype(BF16)
        lo3 = (r1 - mid.astype(F32)).astype(BF16)
        aug = _dot(jnp.concatenate([hi, mid, lo3], axis=1), sel_ref[...]).astype(BF16)

        k_fx = _dot(h, w_fxk[...])
        n_pairs = fxk_o.shape[1] // (2 * LANES)
        for p in range(n_pairs):
            base = 2 * p * LANES
            fxk_o[rows, base:base + LANES] = k_fx[:, p * LANES:(p + 1) * LANES].astype(BF16)
            fxk_o[rows, base + LANES:base + 2 * LANES] = aug[:, p * LANES:(p + 1) * LANES]

        sbq_o[rows, :] = _dot(h, w_sbq[...]).astype(BF16)
        sbv_o[rows, :] = _dot(h, w_sbv[...]).astype(BF16)
        swk_o[rows, :] = _dot(h, w_swk[...]).astype(BF16)
        u = _dot(h, w_u[...])
        u0_o[rows, :] = u[:, :LANES]
        u1_o[rows, :] = u[:, LANES:]
        tt = _dot_nt(wt_ref[...], h)
        row0 = 0
        for o in (sbkt_o, fxqt_o, fxvt_o, swqt_o, swvt_o):
            n_sub, width, tile = o.shape
            per_part = n_sub // INPROJ_SPLIT
            for c in range(per_part):
                o[r * per_part + c] = tt[row0:row0 + width, c * tile:(c + 1) * tile].astype(BF16)
            row0 += width
    carry_ref[...] = jnp.broadcast_to(carry, carry_ref.shape)


def _fox_aug_select():
    n_pairs = GROUP_WIDTH // LANES
    sel = np.zeros((3 * LANES, n_pairs * LANES), np.float32)
    for p in range(n_pairs):
        for j in range(2):
            for r in range(3):
                sel[LANES * r + 2 * p + j, LANES * p + FOX_AUG * j + r] = 1.0
    return jnp.asarray(sel, BF16)


def _inproj(l, x, mod, gain, wts, forget_b):
    b, s, d = x.shape
    ts = ROW_TILE * INPROJ_SPLIT
    nk = s // ts
    (w_sbq, w_sbv, w_swk, w_fxk, w_u, wt, wf) = wts
    sel = _fox_aug_select()
    row = lambda width: pl.BlockSpec((None, ts, width), lambda bi, i: (bi, i, 0))
    wspec = lambda w: pl.BlockSpec((None,) + w.shape[1:], lambda bi, i: (l, 0, 0))
    tspec = lambda width, tile: pl.BlockSpec((None, ts // tile, width, tile), lambda bi, i: (bi, i, 0, 0))
    kv_w = N_KV_SWA * HEAD_DIM
    out_shape = (
        jax.ShapeDtypeStruct((b, s, GROUP_WIDTH), BF16),
        jax.ShapeDtypeStruct((b, s, GROUP_WIDTH), BF16),
        jax.ShapeDtypeStruct((b, s, kv_w), BF16),
        jax.ShapeDtypeStruct((b, s, 2 * GROUP_WIDTH), BF16),
        jax.ShapeDtypeStruct((b, s, LANES), F32),
        jax.ShapeDtypeStruct((b, s, LANES), F32),
        jax.ShapeDtypeStruct((b, s // SB_TILE, GROUP_WIDTH, SB_TILE), BF16),
        jax.ShapeDtypeStruct((b, s // FOX_TILE, GROUP_WIDTH, FOX_TILE), BF16),
        jax.ShapeDtypeStruct((b, s // FOX_TILE, GROUP_WIDTH, FOX_TILE), BF16),
        jax.ShapeDtypeStruct((b, s // ROW_TILE, GROUP_WIDTH, ROW_TILE), BF16),
        jax.ShapeDtypeStruct((b, s // ROW_TILE, kv_w, ROW_TILE), BF16),
    )
    out_specs = (row(GROUP_WIDTH), row(GROUP_WIDTH), row(kv_w), row(2 * GROUP_WIDTH),
                 row(LANES), row(LANES), tspec(GROUP_WIDTH, SB_TILE), tspec(GROUP_WIDTH, FOX_TILE),
                 tspec(GROUP_WIDTH, FOX_TILE), tspec(GROUP_WIDTH, ROW_TILE), tspec(kv_w, ROW_TILE))
    return pl.pallas_call(
        _inproj_kernel,
        out_shape=out_shape,
        grid=(b, nk),
        in_specs=[
            pl.BlockSpec((None, ts, d), lambda bi, i: (bi, i, 0)),
            pl.BlockSpec((None, None, N_ADA, d), lambda bi, i: (l, bi, 0, 0)),
            pl.BlockSpec((None, 1, d), lambda bi, i: (l, 0, 0)),
            wspec(w_sbq), wspec(w_sbv), wspec(w_swk), wspec(w_fxk), wspec(w_u),
            wspec(wt), wspec(wf),
            pl.BlockSpec((None, 1, LANES), lambda bi, i: (l, 0, 0)),
            pl.BlockSpec(sel.shape, lambda bi, i: (0, 0)),
        ],
        out_specs=out_specs,
        scratch_shapes=[pltpu.VMEM((8, LANES), F32)],
        compiler_params=_params("arbitrary", "arbitrary"),
        name="in_proj",
    )(x, mod, gain, w_sbq, w_sbv, w_swk, w_fxk, w_u, wt, wf, forget_b, sel)


def _pair_split(q):
    lane = lax.broadcasted_iota(jnp.int32, q.shape, 1)
    left = lane < HEAD_DIM
    zero = jnp.zeros_like(q)
    return left, (jnp.where(left, q, zero), jnp.where(left, zero, q))


def _sb_kernel(q_ref, kt_ref, v_ref, suf_ref, o_ref):
    t = q_ref.shape[0]
    n_pairs = q_ref.shape[1] // LANES
    qi = pl.program_id(1)
    split = [_pair_split(q_ref[:, p * LANES:(p + 1) * LANES]) for p in range(n_pairs)]
    left = split[0][0]
    row = lax.broadcasted_iota(jnp.int32, (t, t), 0)
    col = lax.broadcasted_iota(jnp.int32, (t, t), 1)
    strict = col < row
    suf = suf_ref[...]

    def tile(ki, carry):
        run, acc = carry
        rows = pl.ds(pl.multiple_of(ki * t, t), t)
        heads = [(p, j) for p in range(n_pairs) for j in range(2)]
        z = [_dot(split[p][1][j], kt_ref[ki, p * LANES:(p + 1) * LANES, :]) for p, j in heads]
        lsz, lk = [], []
        for zh in z:
            sp = jnp.log(1.0 + jnp.exp(-jnp.abs(zh)))
            lszh = jnp.minimum(zh, 0.0) - sp
            lkh = lszh - zh
            lsz.append(lszh)
            lk.append(lkh)
        within = [_dot(lkh.astype(BF16), suf) for lkh in lk]
        w = [jnp.exp(lsz[h] + within[h] + run[h]).astype(BF16) for h in range(len(heads))]
        pv = [_dot(w[h], v_ref[rows, p * LANES:(p + 1) * LANES]) for h, (p, j) in enumerate(heads)]
        new_run = tuple(run[h] + jnp.sum(lk[h], axis=-1, keepdims=True) for h in range(len(heads)))
        new_acc = tuple(acc[p] + jnp.where(left, pv[2 * p], pv[2 * p + 1]) for p in range(n_pairs))
        return new_run, new_acc

    def alive(run):
        top = run[0]
        for r in run[1:]:
            top = jnp.maximum(top, r)
        return jnp.max(top) > EXP_UNDERFLOW

    def first_two():
        prev = jnp.maximum(qi - 1, 0)
        has_prev = qi >= 1
        heads = [(p, j) for p in range(n_pairs) for j in range(2)]
        nh = len(heads)
        tiles = (qi, prev)
        z = [[_dot(split[p][1][j], kt_ref[ki, p * LANES:(p + 1) * LANES, :]) for p, j in heads]
             for ki in tiles]
        lsz, lk = [[], []], [[], []]
        for a in range(2):
            for zh in z[a]:
                sp = jnp.log(1.0 + jnp.exp(-jnp.abs(zh)))
                lszh = jnp.minimum(zh, 0.0) - sp
                lkh = lszh - zh
                lsz[a].append(lszh)
                lk[a].append(jnp.where(strict, lkh, 0.0) if a == 0 else jnp.where(has_prev, lkh, 0.0))
        within = [[_dot(lkh.astype(BF16), suf) for lkh in lk[a]] for a in range(2)]
        run_mid = [jnp.sum(lk[0][h], axis=-1, keepdims=True) for h in range(nh)]
        w0 = [jnp.where(strict, jnp.exp(lsz[0][h] + within[0][h]), 0.0).astype(BF16) for h in range(nh)]
        w1 = [jnp.where(has_prev, jnp.exp(lsz[1][h] + within[1][h] + run_mid[h]), 0.0).astype(BF16)
              for h in range(nh)]
        rows0 = pl.ds(pl.multiple_of(qi * t, t), t)
        rows1 = pl.ds(pl.multiple_of(prev * t, t), t)
        pv = [_dot(w0[h], v_ref[rows0, p * LANES:(p + 1) * LANES])
              + _dot(w1[h], v_ref[rows1, p * LANES:(p + 1) * LANES]) for h, (p, j) in enumerate(heads)]
        run = tuple(run_mid[h] + jnp.sum(lk[1][h], axis=-1, keepdims=True) for h in range(nh))
        acc = tuple(jnp.where(left, pv[2 * p], pv[2 * p + 1]) for p in range(n_pairs))
        return run, acc

    state = first_two()

    def body(c):
        i, _, st = c
        st = tile(qi - 1 - i, st)
        return i + 1, alive(st[0]), st

    _, _, state = lax.while_loop(lambda c: jnp.logical_and(c[0] < qi, c[1]), body,
                                 (jnp.int32(1), alive(state[0]), state))
    for p in range(n_pairs):
        o_ref[:, p * LANES:(p + 1) * LANES] = state[1][p]


def _fox_kernel(qt_ref, k_ref, vt_ref, o_ref, bound_ref):
    t = qt_ref.shape[1]
    nk = vt_ref.shape[0]
    qi = pl.program_id(2)
    qt = qt_ref[...]
    sub = lax.broadcasted_iota(jnp.int32, (LANES, t), 0)
    top = sub < HEAD_DIM
    zero = jnp.zeros_like(qt)
    q2 = qt.astype(F32) * qt.astype(F32)
    qa, qn = [], []
    for j in range(2):
        mine = top if j == 0 else jnp.logical_not(top)
        ones = jnp.where(jnp.logical_and(sub >= FOX_AUG * j, sub < FOX_AUG * j + 3), 1.0, 0.0)
        qa.append(jnp.concatenate([jnp.where(mine, qt, zero), ones.astype(BF16)], axis=0))
        qn.append(jnp.sqrt(jnp.sum(jnp.where(mine, q2, 0.0), axis=0, keepdims=True)))
    key = lax.broadcasted_iota(jnp.int32, (t, t), 0)
    qry = lax.broadcasted_iota(jnp.int32, (t, t), 1)
    causal = key <= qry
    lane = lax.broadcasted_iota(jnp.int32, (t, LANES), 1)

    @pl.when(qi == 0)
    def _():
        def scan(ki, c):
            kk = k_ref[pl.ds(pl.multiple_of(ki * t, t), t), :].astype(F32)
            k2 = kk[:, :LANES] * kk[:, :LANES]
            aug = kk[:, LANES:]
            out_kn, out_nf = [], []
            for j in range(2):
                mine = (lane < HEAD_DIM) if j == 0 else (lane >= HEAD_DIM)
                n2 = jnp.sum(jnp.where(mine, k2, 0.0), axis=1, keepdims=True)
                here = jnp.logical_and(lane >= FOX_AUG * j, lane < FOX_AUG * (j + 1))
                negf = jnp.sum(jnp.where(here, aug, 0.0), axis=1, keepdims=True)
                kn = jnp.maximum(c[j], jnp.sqrt(jnp.max(n2)))
                nf = jnp.maximum(c[2 + j], jnp.max(negf))
                bound_ref[j, ki] = kn
                bound_ref[2 + j, ki] = nf
                out_kn.append(kn)
                out_nf.append(nf)
            return tuple(out_kn + out_nf)

        lax.fori_loop(0, nk, scan, (jnp.float32(0.0),) * 2 + (jnp.float32(-jnp.inf),) * 2)

    def tile(ki, carry, mode):
        m, l, acc = carry
        kk = k_ref[pl.ds(pl.multiple_of(ki * t, t), t), :]
        vt = vt_ref[ki]
        s = [_dot(kk, qa[j]) for j in range(2)]
        if mode == "diag":
            s = [jnp.where(causal, sj, -jnp.inf) for sj in s]
        if mode == "fast":
            new_m = m
        else:
            new_m = tuple(jnp.maximum(m[j], jnp.max(s[j], axis=0, keepdims=True)) for j in range(2))
        e = [jnp.exp(s[j] - new_m[j]) for j in range(2)]
        pv = [_dot(vt, e[j].astype(BF16)) for j in range(2)]
        colsum = [jnp.sum(e[j], axis=0, keepdims=True) for j in range(2)]
        if mode == "fast":
            new_l = tuple(l[j] + colsum[j] for j in range(2))
        else:
            corr = [jnp.exp(m[j] - new_m[j]) for j in range(2)]
            new_l = tuple(l[j] * corr[j] + colsum[j] for j in range(2))
            acc = acc * jnp.where(top, corr[0], corr[1])
        return new_m, new_l, acc + jnp.where(top, pv[0], pv[1])

    def row_gap(m, ki):
        kic = jnp.maximum(ki, 0)
        return tuple(jnp.max(qn[j] * bound_ref[j, kic] - m[j]) for j in range(2))

    def gap(rg, ki):
        kic = jnp.maximum(ki, 0)
        return jnp.maximum(rg[0] + bound_ref[2, kic], rg[1] + bound_ref[3, kic])

    ninf = jnp.full((1, t), -jnp.inf, F32)
    zrow = jnp.zeros((1, t), F32)
    state = tile(qi, ((ninf, ninf), (zrow, zrow), jnp.zeros((LANES, t), F32)), "diag")

    def general(c):
        i, _, st = c
        ki = qi - 1 - i
        st = tile(ki, st, "general")
        return i + 1, gap(row_gap(st[0], ki - 1), ki - 1), st

    i0, _, (m, l, acc) = lax.while_loop(
        lambda c: jnp.logical_and(c[0] < qi, c[1] >= FAST_MARGIN), general,
        (jnp.int32(0), gap(row_gap(state[0], qi - 1), qi - 1), state))
    rg = row_gap(m, qi - 1 - i0)

    def fast(c):
        i, _, l, acc = c
        ki = qi - 1 - i
        _, l, acc = tile(ki, (m, l, acc), "fast")
        return i + 1, gap(rg, ki - 1), l, acc

    _, _, l, acc = lax.while_loop(
        lambda c: jnp.logical_and(c[0] < qi, c[1] > EXP_UNDERFLOW), fast,
        (i0, gap(rg, qi - 1 - i0), l, acc))
    o_ref[...] = (acc / jnp.where(top, l[0], l[1])).T


def _stick_breaking(q, kt, v):
    b, s, width = q.shape
    t = SB_TILE
    nk = s // t
    suffix = jnp.asarray(np.tril(np.ones((t, t), np.float32), -1), BF16)
    once = pl.Buffered(1)
    return pl.pallas_call(
        _sb_kernel,
        out_shape=jax.ShapeDtypeStruct((b, s, width), F32),
        grid=(b, nk),
        in_specs=[
            pl.BlockSpec((None, t, width), lambda bi, i: (bi, i, 0)),
            pl.BlockSpec((None, nk, width, t), lambda bi, i: (bi, 0, 0, 0), pipeline_mode=once),
            pl.BlockSpec((None, s, width), lambda bi, i: (bi, 0, 0), pipeline_mode=once),
            pl.BlockSpec((t, t), lambda bi, i: (0, 0), pipeline_mode=once),
        ],
        out_specs=pl.BlockSpec((None, t, width), lambda bi, i: (bi, i, 0)),
        compiler_params=_params("arbitrary", "arbitrary"),
        name="stick_breaking",
    )(q, kt, v, suffix)


def _forgetting(qt, k, vt):
    b, nq, width, t = qt.shape
    s = nq * t
    return pl.pallas_call(
        _fox_kernel,
        out_shape=jax.ShapeDtypeStruct((b, s, width), F32),
        grid=(b, width // LANES, nq),
        in_specs=[
            pl.BlockSpec((None, None, LANES, t), lambda bi, p, i: (bi, i, p, 0)),
            pl.BlockSpec((None, s, 2 * LANES), lambda bi, p, i: (bi, 0, p)),
            pl.BlockSpec((None, nq, LANES, t), lambda bi, p, i: (bi, 0, p, 0)),
        ],
        out_specs=pl.BlockSpec((None, t, LANES), lambda bi, p, i: (bi, i, p)),
        scratch_shapes=[pltpu.SMEM((4, nq), F32)],
        compiler_params=_params("arbitrary", "arbitrary", "arbitrary"),
        name="forgetting",
    )(qt, k, vt)


def _swa_kernel(qt_ref, kc_ref, kp_ref, vtc_ref, vtp_ref, bias_ref, sink_ref, o_ref):
    r = kc_ref.shape[0]
    first = pl.program_id(1) == 0
    sub = lax.broadcasted_iota(jnp.int32, (LANES, WINDOW), 0)
    top = sub < HEAD_DIM
    key = lax.broadcasted_iota(jnp.int32, (2 * WINDOW, WINDOW), 0)
    n_win = r // WINDOW
    ks, vts = [], []
    for w in range(n_win):
        lo = w * WINDOW
        if w == 0:
            ks.append(jnp.concatenate([kp_ref[...], kc_ref[:WINDOW, :]], axis=0))
            vts.append(jnp.concatenate([vtp_ref[0, :, r - WINDOW:], vtc_ref[0, :, :WINDOW]], axis=1))
        else:
            ks.append(kc_ref[lo - WINDOW:lo + WINDOW, :])
            vts.append(vtc_ref[0, :, lo - WINDOW:lo + WINDOW])
    units = [(w, g, kv) for w in range(n_win) for g in range(2) for kv in range(N_KV_SWA)]
    z = []
    for w, g, kv in units:
        qg = qt_ref[0, g * LANES:(g + 1) * LANES, w * WINDOW:(w + 1) * WINDOW]
        zero = jnp.zeros_like(qg)
        qm = jnp.where(top, qg, zero) if kv == 0 else jnp.where(top, zero, qg)
        zu = _dot(ks[w], qm) + bias_ref[2 * kv + g]
        if w == 0:
            zu = jnp.where(jnp.logical_and(first, key < WINDOW), -jnp.inf, zu)
        z.append(zu)
    p = []
    for (w, g, kv), zu in zip(units, z):
        head = 2 * kv + g
        sink = sink_ref[head:head + 1, 0:1]
        m = jnp.maximum(jnp.max(zu, axis=0, keepdims=True), sink)
        e = jnp.exp(zu - m)
        den = jnp.sum(e, axis=0, keepdims=True) + jnp.exp(sink - m)
        p.append((e * (1.0 / den)).astype(BF16))
    pv = {u: _dot(vts[u[0]], pu) for u, pu in zip(units, p)}
    for w in range(n_win):
        for g in range(2):
            o_ref[w * WINDOW:(w + 1) * WINDOW, g * LANES:(g + 1) * LANES] = jnp.where(
                top, pv[(w, g, 0)], pv[(w, g, 1)]).T


def _sliding_window(l, qt, k, vt, bias_t, sinks):
    b, s, kv_w = k.shape
    r = ROW_TILE
    wpt = r // WINDOW
    return pl.pallas_call(
        _swa_kernel,
        out_shape=jax.ShapeDtypeStruct((b, s, GROUP_WIDTH), F32),
        grid=(b, s // r),
        in_specs=[
            pl.BlockSpec((None, 1, GROUP_WIDTH, r), lambda bi, i: (bi, i, 0, 0)),
            pl.BlockSpec((None, r, kv_w), lambda bi, i: (bi, i, 0)),
            pl.BlockSpec((None, WINDOW, kv_w), lambda bi, i: (bi, jnp.maximum(i * wpt - 1, 0), 0)),
            pl.BlockSpec((None, 1, kv_w, r), lambda bi, i: (bi, i, 0, 0)),
            pl.BlockSpec((None, 1, kv_w, r), lambda bi, i: (bi, jnp.maximum(i - 1, 0), 0, 0)),
            pl.BlockSpec(bias_t.shape, lambda bi, i: (0, 0, 0)),
            pl.BlockSpec((None,) + sinks.shape[1:], lambda bi, i: (l, 0, 0)),
        ],
        out_specs=pl.BlockSpec((None, r, GROUP_WIDTH), lambda bi, i: (bi, i, 0)),
        compiler_params=_params("arbitrary", "arbitrary"),
        name="sliding_window",
    )(qt, k, k, vt, vt, bias_t, sinks)


def _swa_bias(rel_bias):
    i = np.arange(WINDOW)[:, None]
    j = np.arange(2 * WINDOW)[None, :]
    dist = WINDOW + i - j
    in_window = (dist >= 0) & (dist < WINDOW)
    d = np.clip(dist, 0, None)
    max_exact = REL_BUCKETS // 2
    safe = np.maximum(d, 1).astype(np.float32)
    large = max_exact + (np.log(safe / max_exact) / math.log(REL_MAX_DIST / max_exact)
                         * (REL_BUCKETS - max_exact)).astype(np.int32)
    large = np.minimum(large, REL_BUCKETS - 1)
    bucket = np.where(d < max_exact, d, large).astype(np.int32)
    onehot = jnp.asarray(np.eye(REL_BUCKETS, dtype=np.float32)[bucket.reshape(-1)])
    bias = jnp.dot(onehot, rel_bias.astype(F32), precision=lax.Precision.HIGHEST)
    bias = bias.T.reshape(-1, WINDOW, 2 * WINDOW)
    return jnp.where(jnp.asarray(in_window)[None], bias, -jnp.inf).transpose(0, 2, 1)


def _ssm_kernel(u0_ref, u1_ref, avec_ref, bmat_ref, cmat_ref, d_ref, wglu_ref, bglu_ref,
                o0_ref, o1_ref, up_ref, xr_ref, xi_ref, er_ref, ei_ref, sr_ref, si_ref, carry_ref):
    n_slab, ts, _ = xr_ref.shape
    n = ts // SSM_CHUNK
    np_ = n_slab * LANES

    @pl.when(pl.program_id(1) == 0)
    def _():
        carry_ref[...] = jnp.zeros_like(carry_ref)

    def rows(i):
        return pl.ds(i * n, n)

    for k, u_ref in enumerate((u0_ref, u1_ref)):
        for i in range(SSM_CHUNK):
            up_ref[rows(i), k * LANES:(k + 1) * LANES] = u_ref[pl.ds(i, n, stride=SSM_CHUNK), :]
    u = up_ref[...]
    bu = _dot(u.astype(BF16), bmat_ref[...])
    for j in range(n_slab):
        xr_ref[j] = bu[:, j * LANES:(j + 1) * LANES]
        xi_ref[j] = bu[:, np_ + j * LANES:np_ + (j + 1) * LANES]

    def coef(k, j):
        return avec_ref[k:k + 1, j * LANES:(j + 1) * LANES]

    for j in range(n_slab):
        ar, ai = coef(0, j), coef(1, j)
        pr, pi_ = xr_ref[j, rows(0), :], xi_ref[j, rows(0), :]
        for i in range(1, SSM_CHUNK):
            nr = ar * pr - ai * pi_ + xr_ref[j, rows(i), :]
            ni = ar * pi_ + ai * pr + xi_ref[j, rows(i), :]
            xr_ref[j, rows(i), :] = nr
            xi_ref[j, rows(i), :] = ni
            pr, pi_ = nr, ni
        er_ref[j] = pr
        ei_ref[j] = pi_

    a16 = [(coef(2, j), coef(3, j)) for j in range(n_slab)]

    def chunk(c, s):
        out = []
        for j in range(n_slab):
            sr, si = s[j]
            sr_ref[j, pl.ds(c, 1), :] = sr
            si_ref[j, pl.ds(c, 1), :] = si
            er = er_ref[j, pl.ds(c, 1), :]
            ei = ei_ref[j, pl.ds(c, 1), :]
            a16r, a16i = a16[j]
            out.append((a16r * sr - a16i * si + er, a16r * si + a16i * sr + ei))
        return tuple(out)

    init = tuple((carry_ref[0:1, j * LANES:(j + 1) * LANES], carry_ref[1:2, j * LANES:(j + 1) * LANES])
                 for j in range(n_slab))
    fin = lax.fori_loop(0, n, chunk, init)
    for j in range(n_slab):
        carry_ref[0:1, j * LANES:(j + 1) * LANES] = fin[j][0]
        carry_ref[1:2, j * LANES:(j + 1) * LANES] = fin[j][1]

    for j in range(n_slab):
        ar, ai = coef(0, j), coef(1, j)
        pr, pi_ = sr_ref[j], si_ref[j]
        for i in range(SSM_CHUNK):
            pr, pi_ = ar * pr - ai * pi_, ar * pi_ + ai * pr
            xr_ref[j, rows(i), :] = xr_ref[j, rows(i), :] + pr
            xi_ref[j, rows(i), :] = xi_ref[j, rows(i), :] + pi_

    xr = jnp.concatenate([xr_ref[j].astype(BF16) for j in range(n_slab)], axis=1)
    xi = jnp.concatenate([xi_ref[j].astype(BF16) for j in range(n_slab)], axis=1)
    y = _dot(xr, cmat_ref[0:np_, :]) + _dot(xi, cmat_ref[np_:, :]) + d_ref[...] * u
    y = 0.5 * y * (1.0 + jnp.tanh(math.sqrt(2.0 / math.pi) * (y + 0.044715 * (y * y * y))))
    gate = jax.nn.sigmoid(_dot(y.astype(BF16), wglu_ref[...]) + bglu_ref[...])
    out = y * gate
    for k, o_ref in enumerate((o0_ref, o1_ref)):
        for i in range(SSM_CHUNK):
            o_ref[pl.ds(i, n, stride=SSM_CHUNK), :] = out[i * n:(i + 1) * n, k * LANES:(k + 1) * LANES]


def _ssm(l, u0, u1, avec, bmat, cmat, d_skip, w_glu, b_glu):
    b, s, hw = u0.shape
    ts = min(SSM_TILE, s)
    np_ = avec.shape[-1]
    n = ts // SSM_CHUNK
    n_slab = np_ // LANES
    lspec = lambda a: pl.BlockSpec((None,) + a.shape[1:], lambda bi, i: (l, 0, 0))
    half = pl.BlockSpec((None, ts, hw), lambda bi, i: (bi, i, 0))
    return pl.pallas_call(
        _ssm_kernel,
        out_shape=(jax.ShapeDtypeStruct((b, s, hw), F32), jax.ShapeDtypeStruct((b, s, hw), F32)),
        grid=(b, s // ts),
        in_specs=[half, half,
                  lspec(avec), lspec(bmat), lspec(cmat), lspec(d_skip), lspec(w_glu), lspec(b_glu)],
        out_specs=(half, half),
        scratch_shapes=[pltpu.VMEM((ts, 2 * hw), F32),
                        pltpu.VMEM((n_slab, ts, LANES), F32), pltpu.VMEM((n_slab, ts, LANES), F32),
                        pltpu.VMEM((n_slab, n, LANES), F32), pltpu.VMEM((n_slab, n, LANES), F32),
                        pltpu.VMEM((n_slab, n, LANES), F32), pltpu.VMEM((n_slab, n, LANES), F32),
                        pltpu.VMEM((8, np_), F32)],
        compiler_params=_params("arbitrary", "arbitrary"),
        name="s5_ssm",
    )(u0, u1, avec, bmat, cmat, d_skip, w_glu, b_glu)


def _post_kernel(osb_ref, osw_ref, ofx_ref, ossm0_ref, ossm1_ref, x_ref, mod_ref, og_ref, n2_ref,
                 wout_ref, w1_ref, w2_ref, fg_ref, o_ref, *, final):
    mod = mod_ref[...]
    g1, sh2, sc2, g2 = mod[2:3], mod[3:4], mod[4:5], mod[5:6]
    ssm = jnp.concatenate([ossm0_ref[...], ossm1_ref[...]], axis=1)
    mo = None
    for k, o in enumerate((osb_ref[...], osw_ref[...], ofx_ref[...], ssm)):
        lo = k * GROUP_WIDTH
        y = (_rms(o) * og_ref[:, lo:lo + GROUP_WIDTH]).astype(BF16)
        part = _dot(y, wout_ref[lo:lo + GROUP_WIDTH, :])
        mo = part if mo is None else mo + part
    x1 = x_ref[...] + g1 * mo
    h = (_rms(x1) * n2_ref[...] * (1.0 + sc2) + sh2).astype(BF16)
    d_ff = w1_ref.shape[1]
    fc = 1024
    acc = None
    for c in range(d_ff // fc):
        a = _dot(h, w1_ref[:, c * fc:(c + 1) * fc])
        a = jnp.square(jnp.maximum(a, 0.0)).astype(BF16)
        part = _dot(a, w2_ref[c * fc:(c + 1) * fc, :])
        acc = part if acc is None else acc + part
    x2 = x1 + g2 * acc
    if final:
        x2 = _rms(x2) * fg_ref[...]
    o_ref[...] = x2


def _post(l, final, o_sb, o_sw, o_fx, o_ssm0, o_ssm1, x, mod, out_gain, norm2_gain, w_out, w1, w2,
          final_gain):
    b, s, d = x.shape
    tm = ROW_TILE
    grp = pl.BlockSpec((None, tm, GROUP_WIDTH), lambda bi, i: (bi, i, 0))
    half = pl.BlockSpec((None, tm, LANES), lambda bi, i: (bi, i, 0))
    xspec = pl.BlockSpec((None, tm, d), lambda bi, i: (bi, i, 0))
    lspec = lambda a: pl.BlockSpec((None,) + a.shape[1:], lambda bi, i: (l, 0, 0),
                                   pipeline_mode=pl.Buffered(1))
    return pl.pallas_call(
        functools.partial(_post_kernel, final=final),
        out_shape=jax.ShapeDtypeStruct((b, s, d), F32),
        grid=(b, s // tm),
        in_specs=[grp, grp, grp, half, half, xspec,
                  pl.BlockSpec((None, None, N_ADA, d), lambda bi, i: (l, bi, 0, 0)),
                  lspec(out_gain), lspec(norm2_gain), lspec(w_out), lspec(w1), lspec(w2),
                  pl.BlockSpec(final_gain.shape, lambda bi, i: (0, 0))],
        out_specs=xspec,
        compiler_params=_params("arbitrary", "arbitrary"),
        name="post_mlp",
    )(o_sb, o_sw, o_fx, o_ssm0, o_ssm1, x, mod, out_gain, norm2_gain, w_out, w1, w2, final_gain)


def _swa_head_perm(a, axis):
    shape = a.shape
    a = a.reshape(shape[:axis] + (4, HEAD_DIM) + shape[axis + 1:])
    a = jnp.take(a, jnp.asarray([0, 2, 1, 3]), axis=axis)
    return a.reshape(shape)


def kernel(x, c, w_ada, b_ada, norm1_gain, norm2_gain, w_in, rel_bias, sinks, forget_bias, lam_re, lam_im, log_dt, ssm_b_re, ssm_b_im, ssm_c_re, ssm_c_im, ssm_d, w_glu, b_glu, out_gain, w_out, w_mlp_in, w_mlp_out, final_gain):
    b, s, d = x.shape
    depth = w_in.shape[0]
    g, p = lam_re.shape[1:]
    h = ssm_b_re.shape[-1]
    n_fox = forget_bias.shape[-1]
    scale = 1.0 / math.sqrt(HEAD_DIM)
    assert s % SSM_TILE == 0 and s % (ROW_TILE * INPROJ_SPLIT) == 0
    assert ROW_TILE % SB_TILE == 0 and ROW_TILE % FOX_TILE == 0

    gw, kvw = GROUP_WIDTH, N_KV_SWA * HEAD_DIM
    sizes = (gw, gw, gw, gw, kvw, kvw, gw, gw, gw, n_fox, gw)
    offs = np.concatenate([[0], np.cumsum(sizes)])
    col = lambda k: w_in[:, :, offs[k]:offs[k + 1]]
    w_sbq = (col(0) * scale).astype(BF16)
    w_sbv = col(2).astype(BF16)
    w_swk = col(4).astype(BF16)
    w_fxk = col(7).astype(BF16)
    w_u = col(10).astype(BF16)
    wt = jnp.concatenate([col(1), col(6) * scale, col(8), _swa_head_perm(col(3), 2) * scale, col(5)],
                         axis=2).transpose(0, 2, 1).astype(BF16)
    wf = jnp.pad(col(9), ((0, 0), (0, 0), (0, LANES - n_fox))).astype(BF16)
    forget_b = jnp.pad(forget_bias.astype(F32), ((0, 0), (0, LANES - n_fox)))[:, None, :]
    in_wts = (w_sbq, w_sbv, w_swk, w_fxk, w_u, wt, wf)

    og = out_gain.astype(F32)
    og = jnp.concatenate([og[:, :gw], _swa_head_perm(og[:, gw:2 * gw], 1), og[:, 2 * gw:]], axis=1)
    wo = jnp.concatenate([w_out[:, :gw], _swa_head_perm(w_out[:, gw:2 * gw], 1), w_out[:, 2 * gw:]],
                         axis=1).astype(BF16)
    og = og[:, None, :]
    n1 = norm1_gain.astype(F32)[:, None, :]
    n2 = norm2_gain.astype(F32)[:, None, :]
    w1 = w_mlp_in.astype(BF16)
    w2 = w_mlp_out.astype(BF16)
    fg = final_gain.astype(F32)[None, :]
    bias = _swa_bias(rel_bias)
    sinks_b = jnp.broadcast_to(sinks.astype(F32)[:, :, None], sinks.shape + (LANES,))

    avec, bbr, bbi = _ssm_prep(lam_re.astype(F32), lam_im.astype(F32), log_dt.astype(F32),
                               ssm_b_re.astype(F32), ssm_b_im.astype(F32))
    eye = jnp.eye(g, dtype=F32)
    blockdiag_b = lambda bb: jnp.einsum('lhgp,gk->lghkp', bb.reshape(depth, h, g, p), eye
                                        ).reshape(depth, g * h, g * p)
    bmat = jnp.concatenate([blockdiag_b(bbr), blockdiag_b(bbi)], axis=2).astype(BF16)
    blockdiag_c = lambda cc: jnp.einsum('lghp,gk->lgpkh', cc.astype(F32), eye
                                        ).reshape(depth, g * p, g * h)
    cmat = jnp.concatenate([blockdiag_c(ssm_c_re), -blockdiag_c(ssm_c_im)], axis=1).astype(BF16)
    d_skip = ssm_d.astype(F32).reshape(depth, 1, g * h)
    wglu = w_glu.astype(BF16)
    bglu = b_glu.astype(F32)[:, None, :]

    c_pad = jnp.pad(c.astype(F32), ((0, 8 - b), (0, 0)))
    mod = _ada(c_pad, w_ada.astype(F32), b_ada.astype(F32))[:, :b].reshape(depth, b, N_ADA, d)

    x = x.astype(F32)
    for l in range(depth):
        (sbq, sbv, swk, fxk, u0, u1, sbkt, fxqt, fxvt, swqt, swvt) = _inproj(l, x, mod, n1, in_wts, forget_b)
        o_sb = _stick_breaking(sbq, sbkt, sbv)
        o_sw = _sliding_window(l, swqt, swk, swvt, bias, sinks_b)
        o_fx = _forgetting(fxqt, fxk, fxvt)
        o_ssm0, o_ssm1 = _ssm(l, u0, u1, avec, bmat, cmat, d_skip, wglu, bglu)
        x = _post(l, l == depth - 1, o_sb, o_sw, o_fx, o_ssm0, o_ssm1, x, mod, og, n2, wo, w1, w2, fg)
    return x
```

```python
import functools
import math

import numpy as np
import jax
import jax.numpy as jnp
from jax import lax
from jax.experimental import pallas as pl
from jax.experimental.pallas import tpu as pltpu

F32 = jnp.float32
BF16 = jnp.bfloat16

HEAD_DIM = 64
LANES = 128
GROUP_WIDTH = 256
N_KV_SWA = 2
WINDOW = 128
REL_BUCKETS = 32
REL_MAX_DIST = 128
N_ADA = 6
NORM_EPS = 1e-6
SSM_H = 16
SSM_P = 64
SSM_CHUNK = 16

ROW_TILE = 512
INPROJ_SPLIT = 2
SB_TILE = 256
FOX_TILE = 512
SSM_TILE = 1024
SSM_SPLIT = 2
VMEM_LIMIT_BYTES = 56 * 1024 * 1024
EXP_UNDERFLOW = -105.0
FAST_MARGIN = 60.0
FOX_AUG = 8


def _dot(a, b):
    return jnp.dot(a, b, preferred_element_type=F32)


def _dot_nt(a, b):
    return lax.dot_general(a, b, (((1,), (1,)), ((), ())), preferred_element_type=F32)


def _split_bf16(a):
    hi = a.astype(BF16)
    lo = (a - hi.astype(F32)).astype(BF16)
    return hi, lo


def _rms(x):
    return x * lax.rsqrt(jnp.mean(x * x, axis=-1, keepdims=True) + NORM_EPS)


def _params(*sem):
    return pltpu.CompilerParams(dimension_semantics=sem, vmem_limit_bytes=VMEM_LIMIT_BYTES)


def _ada_kernel(c_ref, w_ref, b_ref, o_ref):
    c = c_ref[...]
    ca = c * jax.nn.sigmoid(c)
    a_hi, a_lo = _split_bf16(ca)
    w_hi, w_lo = _split_bf16(w_ref[...])
    o_ref[...] = _dot(a_hi, w_hi) + _dot(a_hi, w_lo) + _dot(a_lo, w_hi) + b_ref[...]


def _ada(c_pad, w_ada, b_ada):
    depth, d, n = w_ada.shape
    tn = 2048
    return pl.pallas_call(
        _ada_kernel,
        out_shape=jax.ShapeDtypeStruct((depth, c_pad.shape[0], n), F32),
        grid=(depth, n // tn),
        in_specs=[
            pl.BlockSpec(c_pad.shape, lambda l, j: (0, 0)),
            pl.BlockSpec((None, d, tn), lambda l, j: (l, 0, j)),
            pl.BlockSpec((None, 1, tn), lambda l, j: (l, 0, j)),
        ],
        out_specs=pl.BlockSpec((None, c_pad.shape[0], tn), lambda l, j: (l, 0, j)),
        compiler_params=_params("arbitrary", "arbitrary"),
        name="ada_mod",
    )(c_pad, w_ada, b_ada.reshape(depth, 1, n))


def _ssm_prep_kernel(lr_ref, li_ref, ldt_ref, br_ref, bi_ref, avec_ref, bbr_ref, bbi_ref):
    lr = lr_ref[...]
    li = li_ref[...]
    dt = jnp.exp(ldt_ref[...])
    mag = jnp.exp(lr * dt)
    ang = li * dt
    ar = mag * jnp.cos(ang)
    ai = mag * jnp.sin(ang)
    den = lr * lr + li * li
    nr = ar - 1.0
    ni = ai
    cr = (nr * lr + ni * li) / den
    ci = (ni * lr - nr * li) / den
    pr, pi_ = ar, ai
    for _ in range(int(math.log2(SSM_CHUNK))):
        pr, pi_ = pr * pr - pi_ * pi_, 2.0 * pr * pi_
    avec_ref[0:1, :] = ar
    avec_ref[1:2, :] = ai
    avec_ref[2:3, :] = pr
    avec_ref[3:4, :] = pi_
    avec_ref[4:8, :] = jnp.zeros((4, ar.shape[1]), F32)
    br = br_ref[...]
    bi = bi_ref[...]
    bbr_ref[...] = cr * br - ci * bi
    bbi_ref[...] = cr * bi + ci * br


def _ssm_prep(lam_re, lam_im, log_dt, b_re, b_im):
    depth, g, p = lam_re.shape
    h = b_re.shape[-1]
    n = g * p
    row = lambda a: a.reshape(depth, 1, n)
    ldt = jnp.broadcast_to(log_dt[:, :, None], (depth, g, p))
    bt = lambda a: a.transpose(0, 3, 1, 2).reshape(depth, h, n)
    vec = pl.BlockSpec((None, 1, n), lambda l: (l, 0, 0))
    mat = pl.BlockSpec((None, h, n), lambda l: (l, 0, 0))
    return pl.pallas_call(
        _ssm_prep_kernel,
        out_shape=(jax.ShapeDtypeStruct((depth, 8, n), F32),
                   jax.ShapeDtypeStruct((depth, h, n), F32),
                   jax.ShapeDtypeStruct((depth, h, n), F32)),
        grid=(depth,),
        in_specs=[vec, vec, vec, mat, mat],
        out_specs=(pl.BlockSpec((None, 8, n), lambda l: (l, 0, 0)), mat, mat),
        compiler_params=_params("arbitrary"),
        name="ssm_prep",
    )(row(lam_re), row(lam_im), row(ldt), bt(b_re), bt(b_im))


def _inproj_kernel(x_ref, mod_ref, gain_ref, w_sbq, w_sbv, w_swk, w_fxk, w_u,
                   wt_ref, wf_ref, fb_ref, sel_ref,
                   sbq_o, sbv_o, swk_o, fxk_o, u0_o, u1_o, sbkt_o, fxqt_o, fxvt_o, swqt_o, swvt_o,
                   carry_ref):
    hr = x_ref.shape[0] // INPROJ_SPLIT

    @pl.when(pl.program_id(1) == 0)
    def _():
        carry_ref[...] = jnp.zeros_like(carry_ref)

    mod = mod_ref[...]
    sh1 = mod[0:1]
    sc1 = mod[1:2]
    gain = gain_ref[...]
    hs = [(_rms(x_ref[r * hr:(r + 1) * hr, :]) * gain * (1.0 + sc1) + sh1).astype(BF16)
          for r in range(INPROJ_SPLIT)]
    step = lax.broadcasted_iota(jnp.int32, (hr, LANES), 0)
    carry = carry_ref[0:1, :]
    for r, h in enumerate(hs):
        rows = slice(r * hr, (r + 1) * hr)
        f = _dot(h, wf_ref[...]) + fb_ref[...]
        logf = jnp.minimum(f, 0.0) - jnp.log1p(jnp.exp(-jnp.abs(f)))
        cum = logf
        k = 1
        while k < hr:
            cum = cum + jnp.where(step >= k, pltpu.roll(cum, k, axis=0), 0.0)
            k *= 2
        cum = cum + carry
        carry = cum[hr - 1:hr, :]
        negf = -cum
        hi = negf.astype(BF16)
        r1 = negf - hi.astype(F32)
        mid = r1.astype(BF16)
        lo3 = (r1 - mid.astype(F32)).astype(BF16)
        aug = _dot(jnp.concatenate([hi, mid, lo3], axis=1), sel_ref[...]).astype(BF16)

        k_fx = _dot(h, w_fxk[...])
        n_pairs = fxk_o.shape[1] // (2 * LANES)
        for p in range(n_pairs):
            base = 2 * p * LANES
            fxk_o[rows, base:base + LANES] = k_fx[:, p * LANES:(p + 1) * LANES].astype(BF16)
            fxk_o[rows, base + LANES:base + 2 * LANES] = aug[:, p * LANES:(p + 1) * LANES]

        sbq_o[rows, :] = _dot(h, w_sbq[...]).astype(BF16)
        sbv_o[rows, :] = _dot(h, w_sbv[...]).astype(BF16)
        swk_o[rows, :] = _dot(h, w_swk[...]).astype(BF16)
        u = _dot(h, w_u[...])
        u0_o[rows, :] = u[:, :LANES]
        u1_o[rows, :] = u[:, LANES:]
        tt = _dot_nt(wt_ref[...], h)
        row0 = 0
        for o in (sbkt_o, fxqt_o, fxvt_o, swqt_o, swvt_o):
            n_sub, width, tile = o.shape
            per_part = n_sub // INPROJ_SPLIT
            for c in range(per_part):
                o[r * per_part + c] = tt[row0:row0 + width, c * tile:(c + 1) * tile].astype(BF16)
            row0 += width
    carry_ref[...] = jnp.broadcast_to(carry, carry_ref.shape)


def _fox_aug_select():
    n_pairs = GROUP_WIDTH // LANES
    sel = np.zeros((3 * LANES, n_pairs * LANES), np.float32)
    for p in range(n_pairs):
        for j in range(2):
            for r in range(3):
                sel[LANES * r + 2 * p + j, LANES * p + FOX_AUG * j + r] = 1.0
    return jnp.asarray(sel, BF16)


def _inproj(l, x, mod, gain, wts, forget_b):
    b, s, d = x.shape
    ts = ROW_TILE * INPROJ_SPLIT
    nk = s // ts
    (w_sbq, w_sbv, w_swk, w_fxk, w_u, wt, wf) = wts
    sel = _fox_aug_select()
    row = lambda width: pl.BlockSpec((None, ts, width), lambda bi, i: (bi, i, 0))
    wspec = lambda w: pl.BlockSpec((None,) + w.shape[1:], lambda bi, i: (l, 0, 0))
    tspec = lambda width, tile: pl.BlockSpec((None, ts // tile, width, tile), lambda bi, i: (bi, i, 0, 0))
    kv_w = N_KV_SWA * HEAD_DIM
    out_shape = (
        jax.ShapeDtypeStruct((b, s, GROUP_WIDTH), BF16),
        jax.ShapeDtypeStruct((b, s, GROUP_WIDTH), BF16),
        jax.ShapeDtypeStruct((b, s, kv_w), BF16),
        jax.ShapeDtypeStruct((b, s, 2 * GROUP_WIDTH), BF16),
        jax.ShapeDtypeStruct((b, s, LANES), F32),
        jax.ShapeDtypeStruct((b, s, LANES), F32),
        jax.ShapeDtypeStruct((b, s // SB_TILE, GROUP_WIDTH, SB_TILE), BF16),
        jax.ShapeDtypeStruct((b, s // FOX_TILE, GROUP_WIDTH, FOX_TILE), BF16),
        jax.ShapeDtypeStruct((b, s // FOX_TILE, GROUP_WIDTH, FOX_TILE), BF16),
        jax.ShapeDtypeStruct((b, s // ROW_TILE, GROUP_WIDTH, ROW_TILE), BF16),
        jax.ShapeDtypeStruct((b, s // ROW_TILE, kv_w, ROW_TILE), BF16),
    )
    out_specs = (row(GROUP_WIDTH), row(GROUP_WIDTH), row(kv_w), row(2 * GROUP_WIDTH),
                 row(LANES), row(LANES), tspec(GROUP_WIDTH, SB_TILE), tspec(GROUP_WIDTH, FOX_TILE),
                 tspec(GROUP_WIDTH, FOX_TILE), tspec(GROUP_WIDTH, ROW_TILE), tspec(kv_w, ROW_TILE))
    return pl.pallas_call(
        _inproj_kernel,
        out_shape=out_shape,
        grid=(b, nk),
        in_specs=[
            pl.BlockSpec((None, ts, d), lambda bi, i: (bi, i, 0)),
            pl.BlockSpec((None, None, N_ADA, d), lambda bi, i: (l, bi, 0, 0)),
            pl.BlockSpec((None, 1, d), lambda bi, i: (l, 0, 0)),
            wspec(w_sbq), wspec(w_sbv), wspec(w_swk), wspec(w_fxk), wspec(w_u),
            wspec(wt), wspec(wf),
            pl.BlockSpec((None, 1, LANES), lambda bi, i: (l, 0, 0)),
            pl.BlockSpec(sel.shape, lambda bi, i: (0, 0)),
        ],
        out_specs=out_specs,
        scratch_shapes=[pltpu.VMEM((8, LANES), F32)],
        compiler_params=_params("arbitrary", "arbitrary"),
        name="in_proj",
    )(x, mod, gain, w_sbq, w_sbv, w_swk, w_fxk, w_u, wt, wf, forget_b, sel)


def _pair_split(q):
    lane = lax.broadcasted_iota(jnp.int32, q.shape, 1)
    left = lane < HEAD_DIM
    zero = jnp.zeros_like(q)
    return left, (jnp.where(left, q, zero), jnp.where(left, zero, q))


def _sb_kernel(q_ref, kt_ref, v_ref, suf_ref, o_ref):
    t = q_ref.shape[0]
    n_pairs = q_ref.shape[1] // LANES
    qi = pl.program_id(1)
    split = [_pair_split(q_ref[:, p * LANES:(p + 1) * LANES]) for p in range(n_pairs)]
    left = split[0][0]
    row = lax.broadcasted_iota(jnp.int32, (t, t), 0)
    col = lax.broadcasted_iota(jnp.int32, (t, t), 1)
    strict = col < row
    suf = suf_ref[...]

    def tile(ki, carry):
        run, acc = carry
        rows = pl.ds(pl.multiple_of(ki * t, t), t)
        heads = [(p, j) for p in range(n_pairs) for j in range(2)]
        z = [_dot(split[p][1][j], kt_ref[ki, p * LANES:(p + 1) * LANES, :]) for p, j in heads]
        lsz, lk = [], []
        for zh in z:
            sp = jnp.log(1.0 + jnp.exp(-jnp.abs(zh)))
            lszh = jnp.minimum(zh, 0.0) - sp
            lkh = lszh - zh
            lsz.append(lszh)
            lk.append(lkh)
        within = [_dot(lkh.astype(BF16), suf) for lkh in lk]
        w = [jnp.exp(lsz[h] + within[h] + run[h]).astype(BF16) for h in range(len(heads))]
        pv = [_dot(w[h], v_ref[rows, p * LANES:(p + 1) * LANES]) for h, (p, j) in enumerate(heads)]
        new_run = tuple(run[h] + jnp.sum(lk[h], axis=-1, keepdims=True) for h in range(len(heads)))
        new_acc = tuple(acc[p] + jnp.where(left, pv[2 * p], pv[2 * p + 1]) for p in range(n_pairs))
        return new_run, new_acc

    def alive(run):
        top = run[0]
        for r in run[1:]:
            top = jnp.maximum(top, r)
        return jnp.max(top) > EXP_UNDERFLOW

    def first_two():
        prev = jnp.maximum(qi - 1, 0)
        has_prev = qi >= 1
        heads = [(p, j) for p in range(n_pairs) for j in range(2)]
        nh = len(heads)
        tiles = (qi, prev)
        z = [[_dot(split[p][1][j], kt_ref[ki, p * LANES:(p + 1) * LANES, :]) for p, j in heads]
             for ki in tiles]
        lsz, lk = [[], []], [[], []]
        for a in range(2):
            for zh in z[a]:
                sp = jnp.log(1.0 + jnp.exp(-jnp.abs(zh)))
                lszh = jnp.minimum(zh, 0.0) - sp
                lkh = lszh - zh
                lsz[a].append(lszh)
                lk[a].append(jnp.where(strict, lkh, 0.0) if a == 0 else jnp.where(has_prev, lkh, 0.0))
        within = [[_dot(lkh.astype(BF16), suf) for lkh in lk[a]] for a in range(2)]
        run_mid = [jnp.sum(lk[0][h], axis=-1, keepdims=True) for h in range(nh)]
        w0 = [jnp.where(strict, jnp.exp(lsz[0][h] + within[0][h]), 0.0).astype(BF16) for h in range(nh)]
        w1 = [jnp.where(has_prev, jnp.exp(lsz[1][h] + within[1][h] + run_mid[h]), 0.0).astype(BF16)
              for h in range(nh)]
        rows0 = pl.ds(pl.multiple_of(qi * t, t), t)
        rows1 = pl.ds(pl.multiple_of(prev * t, t), t)
        pv = [_dot(w0[h], v_ref[rows0, p * LANES:(p + 1) * LANES])
              + _dot(w1[h], v_ref[rows1, p * LANES:(p + 1) * LANES]) for h, (p, j) in enumerate(heads)]
        run = tuple(run_mid[h] + jnp.sum(lk[1][h], axis=-1, keepdims=True) for h in range(nh))
        acc = tuple(jnp.where(left, pv[2 * p], pv[2 * p + 1]) for p in range(n_pairs))
        return run, acc

    state = first_two()

    def body(c):
        i, _, st = c
        st = tile(qi - 1 - i, st)
        return i + 1, alive(st[0]), st

    _, _, state = lax.while_loop(lambda c: jnp.logical_and(c[0] < qi, c[1]), body,
                                 (jnp.int32(1), alive(state[0]), state))
    for p in range(n_pairs):
        o_ref[:, p * LANES:(p + 1) * LANES] = state[1][p]


def _fox_kernel(qt_ref, k_ref, vt_ref, o_ref, bound_ref):
    t = qt_ref.shape[1]
    nk = vt_ref.shape[0]
    qi = pl.program_id(2)
    qt = qt_ref[...]
    sub = lax.broadcasted_iota(jnp.int32, (LANES, t), 0)
    top = sub < HEAD_DIM
    zero = jnp.zeros_like(qt)
    q2 = qt.astype(F32) * qt.astype(F32)
    qa, qn = [], []
    for j in range(2):
        mine = top if j == 0 else jnp.logical_not(top)
        ones = jnp.where(jnp.logical_and(sub >= FOX_AUG * j, sub < FOX_AUG * j + 3), 1.0, 0.0)
        qa.append(jnp.concatenate([jnp.where(mine, qt, zero), ones.astype(BF16)], axis=0))
        qn.append(jnp.sqrt(jnp.sum(jnp.where(mine, q2, 0.0), axis=0, keepdims=True)))
    key = lax.broadcasted_iota(jnp.int32, (t, t), 0)
    qry = lax.broadcasted_iota(jnp.int32, (t, t), 1)
    causal = key <= qry
    lane = lax.broadcasted_iota(jnp.int32, (t, LANES), 1)

    @pl.when(qi == 0)
    def _():
        def scan(ki, c):
            kk = k_ref[pl.ds(pl.multiple_of(ki * t, t), t), :].astype(F32)
            k2 = kk[:, :LANES] * kk[:, :LANES]
            aug = kk[:, LANES:]
            out_kn, out_nf = [], []
            for j in range(2):
                mine = (lane < HEAD_DIM) if j == 0 else (lane >= HEAD_DIM)
                n2 = jnp.sum(jnp.where(mine, k2, 0.0), axis=1, keepdims=True)
                here = jnp.logical_and(lane >= FOX_AUG * j, lane < FOX_AUG * (j + 1))
                negf = jnp.sum(jnp.where(here, aug, 0.0), axis=1, keepdims=True)
                kn = jnp.maximum(c[j], jnp.sqrt(jnp.max(n2)))
                nf = jnp.maximum(c[2 + j], jnp.max(negf))
                bound_ref[j, ki] = kn
                bound_ref[2 + j, ki] = nf
                out_kn.append(kn)
                out_nf.append(nf)
            return tuple(out_kn + out_nf)

        lax.fori_loop(0, nk, scan, (jnp.float32(0.0),) * 2 + (jnp.float32(-jnp.inf),) * 2)

    def tile(ki, carry, mode):
        m, l, acc = carry
        kk = k_ref[pl.ds(pl.multiple_of(ki * t, t), t), :]
        vt = vt_ref[ki]
        s = [_dot(kk, qa[j]) for j in range(2)]
        if mode == "diag":
            s = [jnp.where(causal, sj, -jnp.inf) for sj in s]
        if mode == "fast":
            new_m = m
        else:
            new_m = tuple(jnp.maximum(m[j], jnp.max(s[j], axis=0, keepdims=True)) for j in range(2))
        e = [jnp.exp(s[j] - new_m[j]) for j in range(2)]
        pv = [_dot(vt, e[j].astype(BF16)) for j in range(2)]
        colsum = [jnp.sum(e[j], axis=0, keepdims=True) for j in range(2)]
        if mode == "fast":
            new_l = tuple(l[j] + colsum[j] for j in range(2))
        else:
            corr = [jnp.exp(m[j] - new_m[j]) for j in range(2)]
            new_l = tuple(l[j] * corr[j] + colsum[j] for j in range(2))
            acc = acc * jnp.where(top, corr[0], corr[1])
        return new_m, new_l, acc + jnp.where(top, pv[0], pv[1])

    def row_gap(m, ki):
        kic = jnp.maximum(ki, 0)
        return tuple(jnp.max(qn[j] * bound_ref[j, kic] - m[j]) for j in range(2))

    def gap(rg, ki):
        kic = jnp.maximum(ki, 0)
        return jnp.maximum(rg[0] + bound_ref[2, kic], rg[1] + bound_ref[3, kic])

    ninf = jnp.full((1, t), -jnp.inf, F32)
    zrow = jnp.zeros((1, t), F32)
    state = tile(qi, ((ninf, ninf), (zrow, zrow), jnp.zeros((LANES, t), F32)), "diag")

    def general(c):
        i, _, st = c
        ki = qi - 1 - i
        st = tile(ki, st, "general")
        return i + 1, gap(row_gap(st[0], ki - 1), ki - 1), st

    i0, _, (m, l, acc) = lax.while_loop(
        lambda c: jnp.logical_and(c[0] < qi, c[1] >= FAST_MARGIN), general,
        (jnp.int32(0), gap(row_gap(state[0], qi - 1), qi - 1), state))
    rg = row_gap(m, qi - 1 - i0)

    def fast_pair(c):
        i, _, l, acc = c
        ki = qi - 1 - i
        kk = [k_ref[pl.ds(pl.multiple_of((ki - d) * t, t), t), :] for d in range(2)]
        s = [[_dot(kk[d], qa[j]) for j in range(2)] for d in range(2)]
        e = [[jnp.exp(s[d][j] - m[j]) for j in range(2)] for d in range(2)]
        pv = [_dot(vt_ref[ki, :, :], e[0][j].astype(BF16)) + _dot(vt_ref[ki - 1, :, :], e[1][j].astype(BF16))
              for j in range(2)]
        l = tuple(l[j] + jnp.sum(e[0][j], axis=0, keepdims=True) + jnp.sum(e[1][j], axis=0, keepdims=True)
                  for j in range(2))
        return i + 2, gap(rg, ki - 2), l, acc + jnp.where(top, pv[0], pv[1])

    i1, g1, l, acc = lax.while_loop(
        lambda c: jnp.logical_and(c[0] + 1 < qi, c[1] > EXP_UNDERFLOW), fast_pair,
        (i0, gap(rg, qi - 1 - i0), l, acc))

    def fast(c):
        i, _, l, acc = c
        ki = qi - 1 - i
        _, l, acc = tile(ki, (m, l, acc), "fast")
        return i + 1, gap(rg, ki - 1), l, acc

    _, _, l, acc = lax.while_loop(
        lambda c: jnp.logical_and(c[0] < qi, c[1] > EXP_UNDERFLOW), fast, (i1, g1, l, acc))
    o_ref[...] = (acc / jnp.where(top, l[0], l[1])).T


def _stick_breaking(q, kt, v):
    b, s, width = q.shape
    t = SB_TILE
    nk = s // t
    suffix = jnp.asarray(np.tril(np.ones((t, t), np.float32), -1), BF16)
    once = pl.Buffered(1)
    return pl.pallas_call(
        _sb_kernel,
        out_shape=jax.ShapeDtypeStruct((b, s, width), F32),
        grid=(b, nk),
        in_specs=[
            pl.BlockSpec((None, t, width), lambda bi, i: (bi, i, 0)),
            pl.BlockSpec((None, nk, width, t), lambda bi, i: (bi, 0, 0, 0), pipeline_mode=once),
            pl.BlockSpec((None, s, width), lambda bi, i: (bi, 0, 0), pipeline_mode=once),
            pl.BlockSpec((t, t), lambda bi, i: (0, 0), pipeline_mode=once),
        ],
        out_specs=pl.BlockSpec((None, t, width), lambda bi, i: (bi, i, 0)),
        compiler_params=_params("arbitrary", "arbitrary"),
        name="stick_breaking",
    )(q, kt, v, suffix)


def _forgetting(qt, k, vt):
    b, nq, width, t = qt.shape
    s = nq * t
    return pl.pallas_call(
        _fox_kernel,
        out_shape=jax.ShapeDtypeStruct((b, s, width), F32),
        grid=(b, width // LANES, nq),
        in_specs=[
            pl.BlockSpec((None, None, LANES, t), lambda bi, p, i: (bi, i, p, 0)),
            pl.BlockSpec((None, s, 2 * LANES), lambda bi, p, i: (bi, 0, p)),
            pl.BlockSpec((None, nq, LANES, t), lambda bi, p, i: (bi, 0, p, 0)),
        ],
        out_specs=pl.BlockSpec((None, t, LANES), lambda bi, p, i: (bi, i, p)),
        scratch_shapes=[pltpu.SMEM((4, nq), F32)],
        compiler_params=_params("arbitrary", "arbitrary", "arbitrary"),
        name="forgetting",
    )(qt, k, vt)


def _swa_kernel(qt_ref, kc_ref, kp_ref, vtc_ref, vtp_ref, bias_ref, sink_ref, o_ref):
    r = kc_ref.shape[0]
    first = pl.program_id(1) == 0
    sub = lax.broadcasted_iota(jnp.int32, (LANES, WINDOW), 0)
    top = sub < HEAD_DIM
    key = lax.broadcasted_iota(jnp.int32, (2 * WINDOW, WINDOW), 0)
    n_win = r // WINDOW
    ks, vts = [], []
    for w in range(n_win):
        lo = w * WINDOW
        if w == 0:
            ks.append(jnp.concatenate([kp_ref[...], kc_ref[:WINDOW, :]], axis=0))
            vts.append(jnp.concatenate([vtp_ref[0, :, r - WINDOW:], vtc_ref[0, :, :WINDOW]], axis=1))
        else:
            ks.append(kc_ref[lo - WINDOW:lo + WINDOW, :])
            vts.append(vtc_ref[0, :, lo - WINDOW:lo + WINDOW])
    units = [(w, g, kv) for w in range(n_win) for g in range(2) for kv in range(N_KV_SWA)]
    z = []
    for w, g, kv in units:
        qg = qt_ref[0, g * LANES:(g + 1) * LANES, w * WINDOW:(w + 1) * WINDOW]
        zero = jnp.zeros_like(qg)
        qm = jnp.where(top, qg, zero) if kv == 0 else jnp.where(top, zero, qg)
        zu = _dot(ks[w], qm) + bias_ref[2 * kv + g]
        if w == 0:
            zu = jnp.where(jnp.logical_and(first, key < WINDOW), -jnp.inf, zu)
        z.append(zu)
    p = []
    for (w, g, kv), zu in zip(units, z):
        head = 2 * kv + g
        sink = sink_ref[head:head + 1, 0:1]
        m = jnp.maximum(jnp.max(zu, axis=0, keepdims=True), sink)
        e = jnp.exp(zu - m)
        den = jnp.sum(e, axis=0, keepdims=True) + jnp.exp(sink - m)
        p.append((e * (1.0 / den)).astype(BF16))
    pv = {u: _dot(vts[u[0]], pu) for u, pu in zip(units, p)}
    for w in range(n_win):
        for g in range(2):
            o_ref[w * WINDOW:(w + 1) * WINDOW, g * LANES:(g + 1) * LANES] = jnp.where(
                top, pv[(w, g, 0)], pv[(w, g, 1)]).T


def _sliding_window(l, qt, k, vt, bias_t, sinks):
    b, s, kv_w = k.shape
    r = ROW_TILE
    wpt = r // WINDOW
    return pl.pallas_call(
        _swa_kernel,
        out_shape=jax.ShapeDtypeStruct((b, s, GROUP_WIDTH), F32),
        grid=(b, s // r),
        in_specs=[
            pl.BlockSpec((None, 1, GROUP_WIDTH, r), lambda bi, i: (bi, i, 0, 0)),
            pl.BlockSpec((None, r, kv_w), lambda bi, i: (bi, i, 0)),
            pl.BlockSpec((None, WINDOW, kv_w), lambda bi, i: (bi, jnp.maximum(i * wpt - 1, 0), 0)),
            pl.BlockSpec((None, 1, kv_w, r), lambda bi, i: (bi, i, 0, 0)),
            pl.BlockSpec((None, 1, kv_w, r), lambda bi, i: (bi, jnp.maximum(i - 1, 0), 0, 0)),
            pl.BlockSpec(bias_t.shape, lambda bi, i: (0, 0, 0)),
            pl.BlockSpec((None,) + sinks.shape[1:], lambda bi, i: (l, 0, 0)),
        ],
        out_specs=pl.BlockSpec((None, r, GROUP_WIDTH), lambda bi, i: (bi, i, 0)),
        compiler_params=_params("arbitrary", "arbitrary"),
        name="sliding_window",
    )(qt, k, k, vt, vt, bias_t, sinks)


def _swa_bias(rel_bias):
    i = np.arange(WINDOW)[:, None]
    j = np.arange(2 * WINDOW)[None, :]
    dist = WINDOW + i - j
    in_window = (dist >= 0) & (dist < WINDOW)
    d = np.clip(dist, 0, None)
    max_exact = REL_BUCKETS // 2
    safe = np.maximum(d, 1).astype(np.float32)
    large = max_exact + (np.log(safe / max_exact) / math.log(REL_MAX_DIST / max_exact)
                         * (REL_BUCKETS - max_exact)).astype(np.int32)
    large = np.minimum(large, REL_BUCKETS - 1)
    bucket = np.where(d < max_exact, d, large).astype(np.int32)
    onehot = jnp.asarray(np.eye(REL_BUCKETS, dtype=np.float32)[bucket.reshape(-1)])
    bias = jnp.dot(onehot, rel_bias.astype(F32), precision=lax.Precision.HIGHEST)
    bias = bias.T.reshape(-1, WINDOW, 2 * WINDOW)
    return jnp.where(jnp.asarray(in_window)[None], bias, -jnp.inf).transpose(0, 2, 1)


def _ssm_kernel(u0_ref, u1_ref, avec_ref, bmat_ref, cmat_ref, d_ref, wglu_ref, bglu_ref,
                o0_ref, o1_ref, up_ref, xr_ref, xi_ref, xb_ref, er_ref, ei_ref, sr_ref, si_ref, carry_ref):
    n_part, n_slab, tp, _ = xr_ref.shape
    n = tp // SSM_CHUNK
    np_ = n_slab * LANES

    @pl.when(pl.program_id(1) == 0)
    def _():
        carry_ref[...] = jnp.zeros_like(carry_ref)

    def rows(i):
        return pl.ds(i * n, n)

    def coef(k, j):
        return avec_ref[k:k + 1, j * LANES:(j + 1) * LANES]

    for part in range(n_part):
        for k, u_ref in enumerate((u0_ref, u1_ref)):
            for i in range(SSM_CHUNK):
                up_ref[part, rows(i), k * LANES:(k + 1) * LANES] = (
                    u_ref[pl.ds(part * tp + i, n, stride=SSM_CHUNK), :])
        bu = _dot(up_ref[part].astype(BF16), bmat_ref[...])
        for j in range(n_slab):
            ar, ai = coef(0, j), coef(1, j)
            re = slice(j * LANES, (j + 1) * LANES)
            im = slice(np_ + j * LANES, np_ + (j + 1) * LANES)
            pr, pi_ = bu[0:n, re], bu[0:n, im]
            xr_ref[part, j, rows(0), :] = pr
            xi_ref[part, j, rows(0), :] = pi_
            for i in range(1, SSM_CHUNK):
                nr = ar * pr - ai * pi_ + bu[i * n:(i + 1) * n, re]
                ni = ar * pi_ + ai * pr + bu[i * n:(i + 1) * n, im]
                xr_ref[part, j, rows(i), :] = nr
                xi_ref[part, j, rows(i), :] = ni
                pr, pi_ = nr, ni
            er_ref[part, j] = pr
            ei_ref[part, j] = pi_

    a16 = [(coef(2, j), coef(3, j)) for j in range(n_slab)]
    state = tuple((carry_ref[0:1, j * LANES:(j + 1) * LANES], carry_ref[1:2, j * LANES:(j + 1) * LANES])
                  for j in range(n_slab))
    for part in range(n_part):
        def chunk(c, s, part=part):
            out = []
            for j in range(n_slab):
                sr, si = s[j]
                sr_ref[part, j, pl.ds(c, 1), :] = sr
                si_ref[part, j, pl.ds(c, 1), :] = si
                er = er_ref[part, j, pl.ds(c, 1), :]
                ei = ei_ref[part, j, pl.ds(c, 1), :]
                a16r, a16i = a16[j]
                out.append((a16r * sr - a16i * si + er, a16r * si + a16i * sr + ei))
            return tuple(out)

        state = lax.fori_loop(0, n, chunk, state)
    for j in range(n_slab):
        carry_ref[0:1, j * LANES:(j + 1) * LANES] = state[j][0]
        carry_ref[1:2, j * LANES:(j + 1) * LANES] = state[j][1]

    for part in range(n_part):
        for j in range(n_slab):
            ar, ai = coef(0, j), coef(1, j)
            pr, pi_ = sr_ref[part, j], si_ref[part, j]
            for i in range(SSM_CHUNK):
                pr, pi_ = ar * pr - ai * pi_, ar * pi_ + ai * pr
                xb_ref[part, rows(i), j * LANES:(j + 1) * LANES] = (
                    xr_ref[part, j, rows(i), :] + pr).astype(BF16)
                xb_ref[part, rows(i), np_ + j * LANES:np_ + (j + 1) * LANES] = (
                    xi_ref[part, j, rows(i), :] + pi_).astype(BF16)
        y = _dot(xb_ref[part], cmat_ref[...]) + d_ref[...] * up_ref[part]
        y = 0.5 * y * (1.0 + jnp.tanh(math.sqrt(2.0 / math.pi) * (y + 0.044715 * (y * y * y))))
        gate = jax.nn.sigmoid(_dot(y.astype(BF16), wglu_ref[...]) + bglu_ref[...])
        out = y * gate
        for k, o_ref in enumerate((o0_ref, o1_ref)):
            for i in range(SSM_CHUNK):
                o_ref[pl.ds(part * tp + i, n, stride=SSM_CHUNK), :] = (
                    out[i * n:(i + 1) * n, k * LANES:(k + 1) * LANES])


def _ssm(l, u0, u1, avec, bmat, cmat, d_skip, w_glu, b_glu):
    b, s, hw = u0.shape
    tp = SSM_TILE
    ts = tp * SSM_SPLIT
    np_ = avec.shape[-1]
    n = tp // SSM_CHUNK
    n_slab = np_ // LANES
    lspec = lambda a: pl.BlockSpec((None,) + a.shape[1:], lambda bi, i: (l, 0, 0))
    half = pl.BlockSpec((None, ts, hw), lambda bi, i: (bi, i, 0))
    state = pltpu.VMEM((SSM_SPLIT, n_slab, tp, LANES), F32)
    ends = pltpu.VMEM((SSM_SPLIT, n_slab, n, LANES), F32)
    return pl.pallas_call(
        _ssm_kernel,
        out_shape=(jax.ShapeDtypeStruct((b, s, hw), F32), jax.ShapeDtypeStruct((b, s, hw), F32)),
        grid=(b, s // ts),
        in_specs=[half, half,
                  lspec(avec), lspec(bmat), lspec(cmat), lspec(d_skip), lspec(w_glu), lspec(b_glu)],
        out_specs=(half, half),
        scratch_shapes=[pltpu.VMEM((SSM_SPLIT, tp, 2 * hw), F32), state, state,
                        pltpu.VMEM((SSM_SPLIT, tp, 2 * np_), BF16), ends, ends, ends, ends,
                        pltpu.VMEM((8, np_), F32)],
        compiler_params=_params("arbitrary", "arbitrary"),
        name="s5_ssm",
    )(u0, u1, avec, bmat, cmat, d_skip, w_glu, b_glu)


def _post_kernel(osb_ref, osw_ref, ofx_ref, ossm0_ref, ossm1_ref, x_ref, mod_ref, og_ref, n2_ref,
                 wout_ref, w1_ref, w2_ref, fg_ref, o_ref, *, final):
    mod = mod_ref[...]
    g1, sh2, sc2, g2 = mod[2:3], mod[3:4], mod[4:5], mod[5:6]
    ssm = jnp.concatenate([ossm0_ref[...], ossm1_ref[...]], axis=1)
    mo = None
    for k, o in enumerate((osb_ref[...], osw_ref[...], ofx_ref[...], ssm)):
        lo = k * GROUP_WIDTH
        y = (_rms(o) * og_ref[:, lo:lo + GROUP_WIDTH]).astype(BF16)
        part = _dot(y, wout_ref[lo:lo + GROUP_WIDTH, :])
        mo = part if mo is None else mo + part
    x1 = x_ref[...] + g1 * mo
    h = (_rms(x1) * n2_ref[...] * (1.0 + sc2) + sh2).astype(BF16)
    d_ff = w1_ref.shape[1]
    fc = 1024
    acc = None
    for c in range(d_ff // fc):
        a = _dot(h, w1_ref[:, c * fc:(c + 1) * fc])
        a = jnp.square(jnp.maximum(a, 0.0)).astype(BF16)
        part = _dot(a, w2_ref[c * fc:(c + 1) * fc, :])
        acc = part if acc is None else acc + part
    x2 = x1 + g2 * acc
    if final:
        x2 = _rms(x2) * fg_ref[...]
    o_ref[...] = x2


def _post(l, final, o_sb, o_sw, o_fx, o_ssm0, o_ssm1, x, mod, out_gain, norm2_gain, w_out, w1, w2,
          final_gain):
    b, s, d = x.shape
    tm = ROW_TILE
    grp = pl.BlockSpec((None, tm, GROUP_WIDTH), lambda bi, i: (bi, i, 0))
    half = pl.BlockSpec((None, tm, LANES), lambda bi, i: (bi, i, 0))
    xspec = pl.BlockSpec((None, tm, d), lambda bi, i: (bi, i, 0))
    lspec = lambda a: pl.BlockSpec((None,) + a.shape[1:], lambda bi, i: (l, 0, 0),
                                   pipeline_mode=pl.Buffered(1))
    return pl.pallas_call(
        functools.partial(_post_kernel, final=final),
        out_shape=jax.ShapeDtypeStruct((b, s, d), F32),
        grid=(b, s // tm),
        in_specs=[grp, grp, grp, half, half, xspec,
                  pl.BlockSpec((None, None, N_ADA, d), lambda bi, i: (l, bi, 0, 0)),
                  lspec(out_gain), lspec(norm2_gain), lspec(w_out), lspec(w1), lspec(w2),
                  pl.BlockSpec(final_gain.shape, lambda bi, i: (0, 0))],
        out_specs=xspec,
        compiler_params=_params("arbitrary", "arbitrary"),
        name="post_mlp",
    )(o_sb, o_sw, o_fx, o_ssm0, o_ssm1, x, mod, out_gain, norm2_gain, w_out, w1, w2, final_gain)


def _swa_head_perm(a, axis):
    shape = a.shape
    a = a.reshape(shape[:axis] + (4, HEAD_DIM) + shape[axis + 1:])
    a = jnp.take(a, jnp.asarray([0, 2, 1, 3]), axis=axis)
    return a.reshape(shape)


def kernel(x, c, w_ada, b_ada, norm1_gain, norm2_gain, w_in, rel_bias, sinks, forget_bias, lam_re, lam_im, log_dt, ssm_b_re, ssm_b_im, ssm_c_re, ssm_c_im, ssm_d, w_glu, b_glu, out_gain, w_out, w_mlp_in, w_mlp_out, final_gain):
    b, s, d = x.shape
    depth = w_in.shape[0]
    g, p = lam_re.shape[1:]
    h = ssm_b_re.shape[-1]
    n_fox = forget_bias.shape[-1]
    scale = 1.0 / math.sqrt(HEAD_DIM)
    assert s % (SSM_TILE * SSM_SPLIT) == 0 and s % (ROW_TILE * INPROJ_SPLIT) == 0
    assert ROW_TILE % SB_TILE == 0 and ROW_TILE % FOX_TILE == 0

    gw, kvw = GROUP_WIDTH, N_KV_SWA * HEAD_DIM
    sizes = (gw, gw, gw, gw, kvw, kvw, gw, gw, gw, n_fox, gw)
    offs = np.concatenate([[0], np.cumsum(sizes)])
    col = lambda k: w_in[:, :, offs[k]:offs[k + 1]]
    w_sbq = (col(0) * scale).astype(BF16)
    w_sbv = col(2).astype(BF16)
    w_swk = col(4).astype(BF16)
    w_fxk = col(7).astype(BF16)
    w_u = col(10).astype(BF16)
    wt = jnp.concatenate([col(1), col(6) * scale, col(8), _swa_head_perm(col(3), 2) * scale, col(5)],
                         axis=2).transpose(0, 2, 1).astype(BF16)
    wf = jnp.pad(col(9), ((0, 0), (0, 0), (0, LANES - n_fox))).astype(BF16)
    forget_b = jnp.pad(forget_bias.astype(F32), ((0, 0), (0, LANES - n_fox)))[:, None, :]
    in_wts = (w_sbq, w_sbv, w_swk, w_fxk, w_u, wt, wf)

    og = out_gain.astype(F32)
    og = jnp.concatenate([og[:, :gw], _swa_head_perm(og[:, gw:2 * gw], 1), og[:, 2 * gw:]], axis=1)
    wo = jnp.concatenate([w_out[:, :gw], _swa_head_perm(w_out[:, gw:2 * gw], 1), w_out[:, 2 * gw:]],
                         axis=1).astype(BF16)
    og = og[:, None, :]
    n1 = norm1_gain.astype(F32)[:, None, :]
    n2 = norm2_gain.astype(F32)[:, None, :]
    w1 = w_mlp_in.astype(BF16)
    w2 = w_mlp_out.astype(BF16)
    fg = final_gain.astype(F32)[None, :]
    bias = _swa_bias(rel_bias)
    sinks_b = jnp.broadcast_to(sinks.astype(F32)[:, :, None], sinks.shape + (LANES,))

    avec, bbr, bbi = _ssm_prep(lam_re.astype(F32), lam_im.astype(F32), log_dt.astype(F32),
                               ssm_b_re.astype(F32), ssm_b_im.astype(F32))
    eye = jnp.eye(g, dtype=F32)
    blockdiag_b = lambda bb: jnp.einsum('lhgp,gk->lghkp', bb.reshape(depth, h, g, p), eye
                                        ).reshape(depth, g * h, g * p)
    bmat = jnp.concatenate([blockdiag_b(bbr), blockdiag_b(bbi)], axis=2).astype(BF16)
    blockdiag_c = lambda cc: jnp.einsum('lghp,gk->lgpkh', cc.astype(F32), eye
                                        ).reshape(depth, g * p, g * h)
    cmat = jnp.concatenate([blockdiag_c(ssm_c_re), -blockdiag_c(ssm_c_im)], axis=1).astype(BF16)
    d_skip = ssm_d.astype(F32).reshape(depth, 1, g * h)
    wglu = w_glu.astype(BF16)
    bglu = b_glu.astype(F32)[:, None, :]

    c_pad = jnp.pad(c.astype(F32), ((0, 8 - b), (0, 0)))
    mod = _ada(c_pad, w_ada.astype(F32), b_ada.astype(F32))[:, :b].reshape(depth, b, N_ADA, d)

    x = x.astype(F32)
    for l in range(depth):
        (sbq, sbv, swk, fxk, u0, u1, sbkt, fxqt, fxvt, swqt, swvt) = _inproj(l, x, mod, n1, in_wts, forget_b)
        o_sb = _stick_breaking(sbq, sbkt, sbv)
        o_sw = _sliding_window(l, swqt, swk, swvt, bias, sinks_b)
        o_fx = _forgetting(fxqt, fxk, fxvt)
        o_ssm0, o_ssm1 = _ssm(l, u0, u1, avec, bmat, cmat, d_skip, wglu, bglu)
        x = _post(l, l == depth - 1, o_sb, o_sw, o_fx, o_ssm0, o_ssm1, x, mod, og, n2, wo, w1, w2, fg)
    return x
```

```python
import functools
import math

import numpy as np
import jax
import jax.numpy as jnp
from jax import lax
from jax.experimental import pallas as pl
from jax.experimental.pallas import tpu as pltpu

F32 = jnp.float32
BF16 = jnp.bfloat16

HEAD_DIM = 64
LANES = 128
GROUP_WIDTH = 256
N_KV_SWA = 2
WINDOW = 128
REL_BUCKETS = 32
REL_MAX_DIST = 128
N_ADA = 6
NORM_EPS = 1e-6
SSM_H = 16
SSM_P = 64
SSM_CHUNK = 16

ROW_TILE = 512
INPROJ_SPLIT = 2
SB_TILE = 256
FOX_TILE = 512
SSM_TILE = 1024
VMEM_LIMIT_BYTES = 56 * 1024 * 1024
EXP_UNDERFLOW = -105.0
FAST_MARGIN = 60.0
FOX_AUG = 8


def _dot(a, b):
    return jnp.dot(a, b, preferred_element_type=F32)


def _dot_nt(a, b):
    return lax.dot_general(a, b, (((1,), (1,)), ((), ())), preferred_element_type=F32)


def _split_bf16(a):
    hi = a.astype(BF16)
    lo = (a - hi.astype(F32)).astype(BF16)
    return hi, lo


def _rms(x):
    return x * lax.rsqrt(jnp.mean(x * x, axis=-1, keepdims=True) + NORM_EPS)


def _params(*sem):
    return pltpu.CompilerParams(dimension_semantics=sem, vmem_limit_bytes=VMEM_LIMIT_BYTES)


def _ada_kernel(c_ref, w_ref, b_ref, o_ref):
    c = c_ref[...]
    ca = c * jax.nn.sigmoid(c)
    a_hi, a_lo = _split_bf16(ca)
    w_hi, w_lo = _split_bf16(w_ref[...])
    o_ref[...] = _dot(a_hi, w_hi) + _dot(a_hi, w_lo) + _dot(a_lo, w_hi) + b_ref[...]


def _ada(c_pad, w_ada, b_ada):
    depth, d, n = w_ada.shape
    tn = 2048
    return pl.pallas_call(
        _ada_kernel,
        out_shape=jax.ShapeDtypeStruct((depth, c_pad.shape[0], n), F32),
        grid=(depth, n // tn),
        in_specs=[
            pl.BlockSpec(c_pad.shape, lambda l, j: (0, 0)),
            pl.BlockSpec((None, d, tn), lambda l, j: (l, 0, j)),
            pl.BlockSpec((None, 1, tn), lambda l, j: (l, 0, j)),
        ],
        out_specs=pl.BlockSpec((None, c_pad.shape[0], tn), lambda l, j: (l, 0, j)),
        compiler_params=_params("arbitrary", "arbitrary"),
        name="ada_mod",
    )(c_pad, w_ada, b_ada.reshape(depth, 1, n))


def _ssm_prep_kernel(lr_ref, li_ref, ldt_ref, br_ref, bi_ref, avec_ref, bbr_ref, bbi_ref):
    lr = lr_ref[...]
    li = li_ref[...]
    dt = jnp.exp(ldt_ref[...])
    mag = jnp.exp(lr * dt)
    ang = li * dt
    ar = mag * jnp.cos(ang)
    ai = mag * jnp.sin(ang)
    den = lr * lr + li * li
    nr = ar - 1.0
    ni = ai
    cr = (nr * lr + ni * li) / den
    ci = (ni * lr - nr * li) / den
    pr, pi_ = ar, ai
    for _ in range(int(math.log2(SSM_CHUNK))):
        pr, pi_ = pr * pr - pi_ * pi_, 2.0 * pr * pi_
    avec_ref[0:1, :] = ar
    avec_ref[1:2, :] = ai
    avec_ref[2:3, :] = pr
    avec_ref[3:4, :] = pi_
    avec_ref[4:8, :] = jnp.zeros((4, ar.shape[1]), F32)
    br = br_ref[...]
    bi = bi_ref[...]
    bbr_ref[...] = cr * br - ci * bi
    bbi_ref[...] = cr * bi + ci * br


def _ssm_prep(lam_re, lam_im, log_dt, b_re, b_im):
    depth, g, p = lam_re.shape
    h = b_re.shape[-1]
    n = g * p
    row = lambda a: a.reshape(depth, 1, n)
    ldt = jnp.broadcast_to(log_dt[:, :, None], (depth, g, p))
    bt = lambda a: a.transpose(0, 3, 1, 2).reshape(depth, h, n)
    vec = pl.BlockSpec((None, 1, n), lambda l: (l, 0, 0))
    mat = pl.BlockSpec((None, h, n), lambda l: (l, 0, 0))
    return pl.pallas_call(
        _ssm_prep_kernel,
        out_shape=(jax.ShapeDtypeStruct((depth, 8, n), F32),
                   jax.ShapeDtypeStruct((depth, h, n), F32),
                   jax.ShapeDtypeStruct((depth, h, n), F32)),
        grid=(depth,),
        in_specs=[vec, vec, vec, mat, mat],
        out_specs=(pl.BlockSpec((None, 8, n), lambda l: (l, 0, 0)), mat, mat),
        compiler_params=_params("arbitrary"),
        name="ssm_prep",
    )(row(lam_re), row(lam_im), row(ldt), bt(b_re), bt(b_im))


def _inproj_kernel(x_ref, mod_ref, gain_ref, w_sbq, w_sbv, w_swk, w_fxk, w_u,
                   wt_ref, wf_ref, fb_ref, sel_ref,
                   sbq_o, sbv_o, swk_o, fxk_o, u0_o, u1_o, sbkt_o, fxqt_o, fxvt_o, swqt_o, swvt_o,
                   carry_ref):
    hr = x_ref.shape[0] // INPROJ_SPLIT

    @pl.when(pl.program_id(1) == 0)
    def _():
        carry_ref[...] = jnp.zeros_like(carry_ref)

    mod = mod_ref[...]
    sh1 = mod[0:1]
    sc1 = mod[1:2]
    gain = gain_ref[...]
    hs = [(_rms(x_ref[r * hr:(r + 1) * hr, :]) * gain * (1.0 + sc1) + sh1).astype(BF16)
          for r in range(INPROJ_SPLIT)]
    step = lax.broadcasted_iota(jnp.int32, (hr, LANES), 0)
    carry = carry_ref[0:1, :]
    for r, h in enumerate(hs):
        rows = slice(r * hr, (r + 1) * hr)
        f = _dot(h, wf_ref[...]) + fb_ref[...]
        logf = jnp.minimum(f, 0.0) - jnp.log1p(jnp.exp(-jnp.abs(f)))
        cum = logf
        k = 1
        while k < hr:
            cum = cum + jnp.where(step >= k, pltpu.roll(cum, k, axis=0), 0.0)
            k *= 2
        cum = cum + carry
        carry = cum[hr - 1:hr, :]
        negf = -cum
        hi = negf.astype(BF16)
        r1 = negf - hi.astype(F32)
        mid = r1.astype(BF16)
        lo3 = (r1 - mid.astype(F32)).astype(BF16)
        aug = _dot(jnp.concatenate([hi, mid, lo3], axis=1), sel_ref[...]).astype(BF16)

        k_fx = _dot(h, w_fxk[...])
        n_pairs = fxk_o.shape[1] // (2 * LANES)
        for p in range(n_pairs):
            base = 2 * p * LANES
            fxk_o[rows, base:base + LANES] = k_fx[:, p * LANES:(p + 1) * LANES].astype(BF16)
            fxk_o[rows, base + LANES:base + 2 * LANES] = aug[:, p * LANES:(p + 1) * LANES]

        sbq_o[rows, :] = _dot(h, w_sbq[...]).astype(BF16)
        sbv_o[rows, :] = _dot(h, w_sbv[...]).astype(BF16)
        swk_o[rows, :] = _dot(h, w_swk[...]).astype(BF16)
        u = _dot(h, w_u[...])
        u0_o[rows, :] = u[:, :LANES]
        u1_o[rows, :] = u[:, LANES:]
        tt = _dot_nt(wt_ref[...], h)
        row0 = 0
        for o in (sbkt_o, fxqt_o, fxvt_o, swqt_o, swvt_o):
            n_sub, width, tile = o.shape
            per_part = n_sub // INPROJ_SPLIT
            for c in range(per_part):
                o[r * per_part + c] = tt[row0:row0 + width, c * tile:(c + 1) * tile].astype(BF16)
            row0 += width
    carry_ref[...] = jnp.broadcast_to(carry, carry_ref.shape)


def _fox_aug_select():
    n_pairs = GROUP_WIDTH // LANES
    sel = np.zeros((3 * LANES, n_pairs * LANES), np.float32)
    for p in range(n_pairs):
        for j in range(2):
            for r in range(3):
                sel[LANES * r + 2 * p + j, LANES * p + FOX_AUG * j + r] = 1.0
    return jnp.asarray(sel, BF16)


def _inproj(l, x, mod, gain, wts, forget_b):
    b, s, d = x.shape
    ts = ROW_TILE * INPROJ_SPLIT
    nk = s // ts
    (w_sbq, w_sbv, w_swk, w_fxk, w_u, wt, wf) = wts
    sel = _fox_aug_select()
    row = lambda width: pl.BlockSpec((None, ts, width), lambda bi, i: (bi, i, 0))
    wspec = lambda w: pl.BlockSpec((None,) + w.shape[1:], lambda bi, i: (l, 0, 0))
    tspec = lambda width, tile: pl.BlockSpec((None, ts // tile, width, tile), lambda bi, i: (bi, i, 0, 0))
    kv_w = N_KV_SWA * HEAD_DIM
    out_shape = (
        jax.ShapeDtypeStruct((b, s, GROUP_WIDTH), BF16),
        jax.ShapeDtypeStruct((b, s, GROUP_WIDTH), BF16),
        jax.ShapeDtypeStruct((b, s, kv_w), BF16),
        jax.ShapeDtypeStruct((b, s, 2 * GROUP_WIDTH), BF16),
        jax.ShapeDtypeStruct((b, s, LANES), F32),
        jax.ShapeDtypeStruct((b, s, LANES), F32),
        jax.ShapeDtypeStruct((b, s // SB_TILE, GROUP_WIDTH, SB_TILE), BF16),
        jax.ShapeDtypeStruct((b, s // FOX_TILE, GROUP_WIDTH, FOX_TILE), BF16),
        jax.ShapeDtypeStruct((b, s // FOX_TILE, GROUP_WIDTH, FOX_TILE), BF16),
        jax.ShapeDtypeStruct((b, s // ROW_TILE, GROUP_WIDTH, ROW_TILE), BF16),
        jax.ShapeDtypeStruct((b, s // ROW_TILE, kv_w, ROW_TILE), BF16),
    )
    out_specs = (row(GROUP_WIDTH), row(GROUP_WIDTH), row(kv_w), row(2 * GROUP_WIDTH),
                 row(LANES), row(LANES), tspec(GROUP_WIDTH, SB_TILE), tspec(GROUP_WIDTH, FOX_TILE),
                 tspec(GROUP_WIDTH, FOX_TILE), tspec(GROUP_WIDTH, ROW_TILE), tspec(kv_w, ROW_TILE))
    return pl.pallas_call(
        _inproj_kernel,
        out_shape=out_shape,
        grid=(b, nk),
        in_specs=[
            pl.BlockSpec((None, ts, d), lambda bi, i: (bi, i, 0)),
            pl.BlockSpec((None, None, N_ADA, d), lambda bi, i: (l, bi, 0, 0)),
            pl.BlockSpec((None, 1, d), lambda bi, i: (l, 0, 0)),
            wspec(w_sbq), wspec(w_sbv), wspec(w_swk), wspec(w_fxk), wspec(w_u),
            wspec(wt), wspec(wf),
            pl.BlockSpec((None, 1, LANES), lambda bi, i: (l, 0, 0)),
            pl.BlockSpec(sel.shape, lambda bi, i: (0, 0)),
        ],
        out_specs=out_specs,
        scratch_shapes=[pltpu.VMEM((8, LANES), F32)],
        compiler_params=_params("arbitrary", "arbitrary"),
        name="in_proj",
    )(x, mod, gain, w_sbq, w_sbv, w_swk, w_fxk, w_u, wt, wf, forget_b, sel)


def _pair_split(q):
    lane = lax.broadcasted_iota(jnp.int32, q.shape, 1)
    left = lane < HEAD_DIM
    zero = jnp.zeros_like(q)
    return left, (jnp.where(left, q, zero), jnp.where(left, zero, q))


def _sb_kernel(q_ref, kt_ref, v_ref, suf_ref, o_ref):
    t = q_ref.shape[0]
    n_pairs = q_ref.shape[1] // LANES
    qi = pl.program_id(1)
    split = [_pair_split(q_ref[:, p * LANES:(p + 1) * LANES]) for p in range(n_pairs)]
    left = split[0][0]
    row = lax.broadcasted_iota(jnp.int32, (t, t), 0)
    col = lax.broadcasted_iota(jnp.int32, (t, t), 1)
    strict = col < row
    suf = suf_ref[...]

    def tile(ki, carry):
        run, acc = carry
        rows = pl.ds(pl.multiple_of(ki * t, t), t)
        heads = [(p, j) for p in range(n_pairs) for j in range(2)]
        z = [_dot(split[p][1][j], kt_ref[ki, p * LANES:(p + 1) * LANES, :]) for p, j in heads]
        lsz, lk = [], []
        for zh in z:
            sp = jnp.log(1.0 + jnp.exp(-jnp.abs(zh)))
            lszh = jnp.minimum(zh, 0.0) - sp
            lkh = lszh - zh
            lsz.append(lszh)
            lk.append(lkh)
        within = [_dot(lkh.astype(BF16), suf) for lkh in lk]
        w = [jnp.exp(lsz[h] + within[h] + run[h]).astype(BF16) for h in range(len(heads))]
        pv = [_dot(w[h], v_ref[rows, p * LANES:(p + 1) * LANES]) for h, (p, j) in enumerate(heads)]
        new_run = tuple(run[h] + jnp.sum(lk[h], axis=-1, keepdims=True) for h in range(len(heads)))
        new_acc = tuple(acc[p] + jnp.where(left, pv[2 * p], pv[2 * p + 1]) for p in range(n_pairs))
        return new_run, new_acc

    def alive(run):
        top = run[0]
        for r in run[1:]:
            top = jnp.maximum(top, r)
        return jnp.max(top) > EXP_UNDERFLOW

    def first_two():
        prev = jnp.maximum(qi - 1, 0)
        has_prev = qi >= 1
        heads = [(p, j) for p in range(n_pairs) for j in range(2)]
        nh = len(heads)
        tiles = (qi, prev)
        z = [[_dot(split[p][1][j], kt_ref[ki, p * LANES:(p + 1) * LANES, :]) for p, j in heads]
             for ki in tiles]
        lsz, lk = [[], []], [[], []]
        for a in range(2):
            for zh in z[a]:
                sp = jnp.log(1.0 + jnp.exp(-jnp.abs(zh)))
                lszh = jnp.minimum(zh, 0.0) - sp
                lkh = lszh - zh
                lsz[a].append(lszh)
                lk[a].append(jnp.where(strict, lkh, 0.0) if a == 0 else jnp.where(has_prev, lkh, 0.0))
        within = [[_dot(lkh.astype(BF16), suf) for lkh in lk[a]] for a in range(2)]
        run_mid = [jnp.sum(lk[0][h], axis=-1, keepdims=True) for h in range(nh)]
        w0 = [jnp.where(strict, jnp.exp(lsz[0][h] + within[0][h]), 0.0).astype(BF16) for h in range(nh)]
        w1 = [jnp.where(has_prev, jnp.exp(lsz[1][h] + within[1][h] + run_mid[h]), 0.0).astype(BF16)
              for h in range(nh)]
        rows0 = pl.ds(pl.multiple_of(qi * t, t), t)
        rows1 = pl.ds(pl.multiple_of(prev * t, t), t)
        pv = [_dot(w0[h], v_ref[rows0, p * LANES:(p + 1) * LANES])
              + _dot(w1[h], v_ref[rows1, p * LANES:(p + 1) * LANES]) for h, (p, j) in enumerate(heads)]
        run = tuple(run_mid[h] + jnp.sum(lk[1][h], axis=-1, keepdims=True) for h in range(nh))
        acc = tuple(jnp.where(left, pv[2 * p], pv[2 * p + 1]) for p in range(n_pairs))
        return run, acc

    state = first_two()

    def body(c):
        i, _, st = c
        st = tile(qi - 1 - i, st)
        return i + 1, alive(st[0]), st

    _, _, state = lax.while_loop(lambda c: jnp.logical_and(c[0] < qi, c[1]), body,
                                 (jnp.int32(1), alive(state[0]), state))
    for p in range(n_pairs):
        o_ref[:, p * LANES:(p + 1) * LANES] = state[1][p]


def _fox_kernel(qt_ref, k_ref, vt_ref, o_ref, bound_ref):
    t = qt_ref.shape[1]
    nk = vt_ref.shape[0]
    qi = pl.program_id(2)
    qt = qt_ref[...]
    sub = lax.broadcasted_iota(jnp.int32, (LANES, t), 0)
    top = sub < HEAD_DIM
    zero = jnp.zeros_like(qt)
    q2 = qt.astype(F32) * qt.astype(F32)
    qa, qn = [], []
    for j in range(2):
        mine = top if j == 0 else jnp.logical_not(top)
        ones = jnp.where(jnp.logical_and(sub >= FOX_AUG * j, sub < FOX_AUG * j + 3), 1.0, 0.0)
        qa.append(jnp.concatenate([jnp.where(mine, qt, zero), ones.astype(BF16)], axis=0))
        qn.append(jnp.sqrt(jnp.sum(jnp.where(mine, q2, 0.0), axis=0, keepdims=True)))
    key = lax.broadcasted_iota(jnp.int32, (t, t), 0)
    qry = lax.broadcasted_iota(jnp.int32, (t, t), 1)
    causal = key <= qry
    lane = lax.broadcasted_iota(jnp.int32, (t, LANES), 1)

    @pl.when(qi == 0)
    def _():
        def scan(ki, c):
            kk = k_ref[pl.ds(pl.multiple_of(ki * t, t), t), :].astype(F32)
            k2 = kk[:, :LANES] * kk[:, :LANES]
            aug = kk[:, LANES:]
            out_kn, out_nf = [], []
            for j in range(2):
                mine = (lane < HEAD_DIM) if j == 0 else (lane >= HEAD_DIM)
                n2 = jnp.sum(jnp.where(mine, k2, 0.0), axis=1, keepdims=True)
                here = jnp.logical_and(lane >= FOX_AUG * j, lane < FOX_AUG * (j + 1))
                negf = jnp.sum(jnp.where(here, aug, 0.0), axis=1, keepdims=True)
                kn = jnp.maximum(c[j], jnp.sqrt(jnp.max(n2)))
                nf = jnp.maximum(c[2 + j], jnp.max(negf))
                bound_ref[j, ki] = kn
                bound_ref[2 + j, ki] = nf
                out_kn.append(kn)
                out_nf.append(nf)
            return tuple(out_kn + out_nf)

        lax.fori_loop(0, nk, scan, (jnp.float32(0.0),) * 2 + (jnp.float32(-jnp.inf),) * 2)

    def tile(ki, carry, mode):
        m, l, acc = carry
        kk = k_ref[pl.ds(pl.multiple_of(ki * t, t), t), :]
        vt = vt_ref[ki]
        s = [_dot(kk, qa[j]) for j in range(2)]
        if mode == "diag":
            s = [jnp.where(causal, sj, -jnp.inf) for sj in s]
        if mode == "fast":
            new_m = m
        else:
            new_m = tuple(jnp.maximum(m[j], jnp.max(s[j], axis=0, keepdims=True)) for j in range(2))
        e = [jnp.exp(s[j] - new_m[j]) for j in range(2)]
        pv = [_dot(vt, e[j].astype(BF16)) for j in range(2)]
        colsum = [jnp.sum(e[j], axis=0, keepdims=True) for j in range(2)]
        if mode == "fast":
            new_l = tuple(l[j] + colsum[j] for j in range(2))
        else:
            corr = [jnp.exp(m[j] - new_m[j]) for j in range(2)]
            new_l = tuple(l[j] * corr[j] + colsum[j] for j in range(2))
            acc = acc * jnp.where(top, corr[0], corr[1])
        return new_m, new_l, acc + jnp.where(top, pv[0], pv[1])

    def row_gap(m, ki):
        kic = jnp.maximum(ki, 0)
        return tuple(jnp.max(qn[j] * bound_ref[j, kic] - m[j]) for j in range(2))

    def gap(rg, ki):
        kic = jnp.maximum(ki, 0)
        return jnp.maximum(rg[0] + bound_ref[2, kic], rg[1] + bound_ref[3, kic])

    ninf = jnp.full((1, t), -jnp.inf, F32)
    zrow = jnp.zeros((1, t), F32)
    state = tile(qi, ((ninf, ninf), (zrow, zrow), jnp.zeros((LANES, t), F32)), "diag")

    def general(c):
        i, _, st = c
        ki = qi - 1 - i
        st = tile(ki, st, "general")
        return i + 1, gap(row_gap(st[0], ki - 1), ki - 1), st

    i0, _, (m, l, acc) = lax.while_loop(
        lambda c: jnp.logical_and(c[0] < qi, c[1] >= FAST_MARGIN), general,
        (jnp.int32(0), gap(row_gap(state[0], qi - 1), qi - 1), state))
    rg = row_gap(m, qi - 1 - i0)

    def fast_pair(c):
        i, _, l, acc = c
        ki = qi - 1 - i
        kk = [k_ref[pl.ds(pl.multiple_of((ki - d) * t, t), t), :] for d in range(2)]
        s = [[_dot(kk[d], qa[j]) for j in range(2)] for d in range(2)]
        e = [[jnp.exp(s[d][j] - m[j]) for j in range(2)] for d in range(2)]
        pv = [_dot(vt_ref[ki, :, :], e[0][j].astype(BF16)) + _dot(vt_ref[ki - 1, :, :], e[1][j].astype(BF16))
              for j in range(2)]
        l = tuple(l[j] + jnp.sum(e[0][j], axis=0, keepdims=True) + jnp.sum(e[1][j], axis=0, keepdims=True)
                  for j in range(2))
        return i + 2, gap(rg, ki - 2), l, acc + jnp.where(top, pv[0], pv[1])

    i1, g1, l, acc = lax.while_loop(
        lambda c: jnp.logical_and(c[0] + 1 < qi, c[1] > EXP_UNDERFLOW), fast_pair,
        (i0, gap(rg, qi - 1 - i0), l, acc))

    def fast(c):
        i, _, l, acc = c
        ki = qi - 1 - i
        _, l, acc = tile(ki, (m, l, acc), "fast")
        return i + 1, gap(rg, ki - 1), l, acc

    _, _, l, acc = lax.while_loop(
        lambda c: jnp.logical_and(c[0] < qi, c[1] > EXP_UNDERFLOW), fast, (i1, g1, l, acc))
    o_ref[...] = (acc / jnp.where(top, l[0], l[1])).T


def _stick_breaking(q, kt, v):
    b, s, width = q.shape
    t = SB_TILE
    nk = s // t
    suffix = jnp.asarray(np.tril(np.ones((t, t), np.float32), -1), BF16)
    once = pl.Buffered(1)
    return pl.pallas_call(
        _sb_kernel,
        out_shape=jax.ShapeDtypeStruct((b, s, width), F32),
        grid=(b, nk),
        in_specs=[
            pl.BlockSpec((None, t, width), lambda bi, i: (bi, i, 0)),
            pl.BlockSpec((None, nk, width, t), lambda bi, i: (bi, 0, 0, 0), pipeline_mode=once),
            pl.BlockSpec((None, s, width), lambda bi, i: (bi, 0, 0), pipeline_mode=once),
            pl.BlockSpec((t, t), lambda bi, i: (0, 0), pipeline_mode=once),
        ],
        out_specs=pl.BlockSpec((None, t, width), lambda bi, i: (bi, i, 0)),
        compiler_params=_params("arbitrary", "arbitrary"),
        name="stick_breaking",
    )(q, kt, v, suffix)


def _forgetting(qt, k, vt):
    b, nq, width, t = qt.shape
    s = nq * t
    return pl.pallas_call(
        _fox_kernel,
        out_shape=jax.ShapeDtypeStruct((b, s, width), F32),
        grid=(b, width // LANES, nq),
        in_specs=[
            pl.BlockSpec((None, None, LANES, t), lambda bi, p, i: (bi, i, p, 0)),
            pl.BlockSpec((None, s, 2 * LANES), lambda bi, p, i: (bi, 0, p)),
            pl.BlockSpec((None, nq, LANES, t), lambda bi, p, i: (bi, 0, p, 0)),
        ],
        out_specs=pl.BlockSpec((None, t, LANES), lambda bi, p, i: (bi, i, p)),
        scratch_shapes=[pltpu.SMEM((4, nq), F32)],
        compiler_params=_params("arbitrary", "arbitrary", "arbitrary"),
        name="forgetting",
    )(qt, k, vt)


def _swa_kernel(qt_ref, kc_ref, kp_ref, vtc_ref, vtp_ref, bias_ref, sink_ref, o_ref):
    r = kc_ref.shape[0]
    first = pl.program_id(1) == 0
    sub = lax.broadcasted_iota(jnp.int32, (LANES, WINDOW), 0)
    top = sub < HEAD_DIM
    key = lax.broadcasted_iota(jnp.int32, (2 * WINDOW, WINDOW), 0)
    n_win = r // WINDOW
    ks, vts = [], []
    for w in range(n_win):
        lo = w * WINDOW
        if w == 0:
            ks.append(jnp.concatenate([kp_ref[...], kc_ref[:WINDOW, :]], axis=0))
            vts.append(jnp.concatenate([vtp_ref[0, :, r - WINDOW:], vtc_ref[0, :, :WINDOW]], axis=1))
        else:
            ks.append(kc_ref[lo - WINDOW:lo + WINDOW, :])
            vts.append(vtc_ref[0, :, lo - WINDOW:lo + WINDOW])
    units = [(w, g, kv) for w in range(n_win) for g in range(2) for kv in range(N_KV_SWA)]
    z = []
    for w, g, kv in units:
        qg = qt_ref[0, g * LANES:(g + 1) * LANES, w * WINDOW:(w + 1) * WINDOW]
        zero = jnp.zeros_like(qg)
        qm = jnp.where(top, qg, zero) if kv == 0 else jnp.where(top, zero, qg)
        zu = _dot(ks[w], qm) + bias_ref[2 * kv + g]
        if w == 0:
            zu = jnp.where(jnp.logical_and(first, key < WINDOW), -jnp.inf, zu)
        z.append(zu)
    p = []
    for (w, g, kv), zu in zip(units, z):
        head = 2 * kv + g
        sink = sink_ref[head:head + 1, 0:1]
        m = jnp.maximum(jnp.max(zu, axis=0, keepdims=True), sink)
        e = jnp.exp(zu - m)
        den = jnp.sum(e, axis=0, keepdims=True) + jnp.exp(sink - m)
        p.append((e * (1.0 / den)).astype(BF16))
    pv = {u: _dot(vts[u[0]], pu) for u, pu in zip(units, p)}
    for w in range(n_win):
        for g in range(2):
            o_ref[w * WINDOW:(w + 1) * WINDOW, g * LANES:(g + 1) * LANES] = jnp.where(
                top, pv[(w, g, 0)], pv[(w, g, 1)]).T


def _sliding_window(l, qt, k, vt, bias_t, sinks):
    b, s, kv_w = k.shape
    r = ROW_TILE
    wpt = r // WINDOW
    return pl.pallas_call(
        _swa_kernel,
        out_shape=jax.ShapeDtypeStruct((b, s, GROUP_WIDTH), F32),
        grid=(b, s // r),
        in_specs=[
            pl.BlockSpec((None, 1, GROUP_WIDTH, r), lambda bi, i: (bi, i, 0, 0)),
            pl.BlockSpec((None, r, kv_w), lambda bi, i: (bi, i, 0)),
            pl.BlockSpec((None, WINDOW, kv_w), lambda bi, i: (bi, jnp.maximum(i * wpt - 1, 0), 0)),
            pl.BlockSpec((None, 1, kv_w, r), lambda bi, i: (bi, i, 0, 0)),
            pl.BlockSpec((None, 1, kv_w, r), lambda bi, i: (bi, jnp.maximum(i - 1, 0), 0, 0)),
            pl.BlockSpec(bias_t.shape, lambda bi, i: (0, 0, 0)),
            pl.BlockSpec((None,) + sinks.shape[1:], lambda bi, i: (l, 0, 0)),
        ],
        out_specs=pl.BlockSpec((None, r, GROUP_WIDTH), lambda bi, i: (bi, i, 0)),
        compiler_params=_params("arbitrary", "arbitrary"),
        name="sliding_window",
    )(qt, k, k, vt, vt, bias_t, sinks)


def _swa_bias(rel_bias):
    i = np.arange(WINDOW)[:, None]
    j = np.arange(2 * WINDOW)[None, :]
    dist = WINDOW + i - j
    in_window = (dist >= 0) & (dist < WINDOW)
    d = np.clip(dist, 0, None)
    max_exact = REL_BUCKETS // 2
    safe = np.maximum(d, 1).astype(np.float32)
    large = max_exact + (np.log(safe / max_exact) / math.log(REL_MAX_DIST / max_exact)
                         * (REL_BUCKETS - max_exact)).astype(np.int32)
    large = np.minimum(large, REL_BUCKETS - 1)
    bucket = np.where(d < max_exact, d, large).astype(np.int32)
    onehot = jnp.asarray(np.eye(REL_BUCKETS, dtype=np.float32)[bucket.reshape(-1)])
    bias = jnp.dot(onehot, rel_bias.astype(F32), precision=lax.Precision.HIGHEST)
    bias = bias.T.reshape(-1, WINDOW, 2 * WINDOW)
    return jnp.where(jnp.asarray(in_window)[None], bias, -jnp.inf).transpose(0, 2, 1)


def _ssm_kernel(u0_ref, u1_ref, avec_ref, bmat_ref, cmat_ref, d_ref, wglu_ref, bglu_ref,
                o0_ref, o1_ref, up_ref, xr_ref, xi_ref, er_ref, ei_ref, sr_ref, si_ref, carry_ref):
    n_slab, ts, _ = xr_ref.shape
    n = ts // SSM_CHUNK
    np_ = n_slab * LANES

    @pl.when(pl.program_id(1) == 0)
    def _():
        carry_ref[...] = jnp.zeros_like(carry_ref)

    def rows(i):
        return pl.ds(i * n, n)

    for k, u_ref in enumerate((u0_ref, u1_ref)):
        for i in range(SSM_CHUNK):
            up_ref[rows(i), k * LANES:(k + 1) * LANES] = u_ref[pl.ds(i, n, stride=SSM_CHUNK), :]
    u = up_ref[...]
    bu = _dot(u.astype(BF16), bmat_ref[...])
    for j in range(n_slab):
        xr_ref[j] = bu[:, j * LANES:(j + 1) * LANES]
        xi_ref[j] = bu[:, np_ + j * LANES:np_ + (j + 1) * LANES]

    def coef(k, j):
        return avec_ref[k:k + 1, j * LANES:(j + 1) * LANES]

    for j in range(n_slab):
        ar, ai = coef(0, j), coef(1, j)
        pr, pi_ = xr_ref[j, rows(0), :], xi_ref[j, rows(0), :]
        for i in range(1, SSM_CHUNK):
            nr = ar * pr - ai * pi_ + xr_ref[j, rows(i), :]
            ni = ar * pi_ + ai * pr + xi_ref[j, rows(i), :]
            xr_ref[j, rows(i), :] = nr
            xi_ref[j, rows(i), :] = ni
            pr, pi_ = nr, ni
        er_ref[j] = pr
        ei_ref[j] = pi_

    a16 = [(coef(2, j), coef(3, j)) for j in range(n_slab)]

    def chunk(c, s):
        out = []
        for j in range(n_slab):
            sr, si = s[j]
            sr_ref[j, pl.ds(c, 1), :] = sr
            si_ref[j, pl.ds(c, 1), :] = si
            er = er_ref[j, pl.ds(c, 1), :]
            ei = ei_ref[j, pl.ds(c, 1), :]
            a16r, a16i = a16[j]
            out.append((a16r * sr - a16i * si + er, a16r * si + a16i * sr + ei))
        return tuple(out)

    init = tuple((carry_ref[0:1, j * LANES:(j + 1) * LANES], carry_ref[1:2, j * LANES:(j + 1) * LANES])
                 for j in range(n_slab))
    fin = lax.fori_loop(0, n, chunk, init)
    for j in range(n_slab):
        carry_ref[0:1, j * LANES:(j + 1) * LANES] = fin[j][0]
        carry_ref[1:2, j * LANES:(j + 1) * LANES] = fin[j][1]

    for j in range(n_slab):
        ar, ai = coef(0, j), coef(1, j)
        pr, pi_ = sr_ref[j], si_ref[j]
        for i in range(SSM_CHUNK):
            pr, pi_ = ar * pr - ai * pi_, ar * pi_ + ai * pr
            xr_ref[j, rows(i), :] = xr_ref[j, rows(i), :] + pr
            xi_ref[j, rows(i), :] = xi_ref[j, rows(i), :] + pi_

    xr = jnp.concatenate([xr_ref[j].astype(BF16) for j in range(n_slab)], axis=1)
    xi = jnp.concatenate([xi_ref[j].astype(BF16) for j in range(n_slab)], axis=1)
    y = _dot(xr, cmat_ref[0:np_, :]) + _dot(xi, cmat_ref[np_:, :]) + d_ref[...] * u
    y = 0.5 * y * (1.0 + jnp.tanh(math.sqrt(2.0 / math.pi) * (y + 0.044715 * (y * y * y))))
    gate = jax.nn.sigmoid(_dot(y.astype(BF16), wglu_ref[...]) + bglu_ref[...])
    out = y * gate
    for k, o_ref in enumerate((o0_ref, o1_ref)):
        for i in range(SSM_CHUNK):
            o_ref[pl.ds(i, n, stride=SSM_CHUNK), :] = out[i * n:(i + 1) * n, k * LANES:(k + 1) * LANES]


def _ssm(l, u0, u1, avec, bmat, cmat, d_skip, w_glu, b_glu):
    b, s, hw = u0.shape
    ts = min(SSM_TILE, s)
    np_ = avec.shape[-1]
    n = ts // SSM_CHUNK
    n_slab = np_ // LANES
    lspec = lambda a: pl.BlockSpec((None,) + a.shape[1:], lambda bi, i: (l, 0, 0))
    half = pl.BlockSpec((None, ts, hw), lambda bi, i: (bi, i, 0))
    return pl.pallas_call(
        _ssm_kernel,
        out_shape=(jax.ShapeDtypeStruct((b, s, hw), F32), jax.ShapeDtypeStruct((b, s, hw), F32)),
        grid=(b, s // ts),
        in_specs=[half, half,
                  lspec(avec), lspec(bmat), lspec(cmat), lspec(d_skip), lspec(w_glu), lspec(b_glu)],
        out_specs=(half, half),
        scratch_shapes=[pltpu.VMEM((ts, 2 * hw), F32),
                        pltpu.VMEM((n_slab, ts, LANES), F32), pltpu.VMEM((n_slab, ts, LANES), F32),
                        pltpu.VMEM((n_slab, n, LANES), F32), pltpu.VMEM((n_slab, n, LANES), F32),
                        pltpu.VMEM((n_slab, n, LANES), F32), pltpu.VMEM((n_slab, n, LANES), F32),
                        pltpu.VMEM((8, np_), F32)],
        compiler_params=_params("arbitrary", "arbitrary"),
        name="s5_ssm",
    )(u0, u1, avec, bmat, cmat, d_skip, w_glu, b_glu)


def _post_kernel(osb_ref, osw_ref, ofx_ref, ossm0_ref, ossm1_ref, x_ref, mod_ref, og_ref, n2_ref,
                 wout_ref, w1_ref, w2_ref, fg_ref, o_ref, *, final):
    mod = mod_ref[...]
    g1, sh2, sc2, g2 = mod[2:3], mod[3:4], mod[4:5], mod[5:6]
    ssm = jnp.concatenate([ossm0_ref[...], ossm1_ref[...]], axis=1)
    mo = None
    for k, o in enumerate((osb_ref[...], osw_ref[...], ofx_ref[...], ssm)):
        lo = k * GROUP_WIDTH
        y = (_rms(o) * og_ref[:, lo:lo + GROUP_WIDTH]).astype(BF16)
        part = _dot(y, wout_ref[lo:lo + GROUP_WIDTH, :])
        mo = part if mo is None else mo + part
    x1 = x_ref[...] + g1 * mo
    h = (_rms(x1) * n2_ref[...] * (1.0 + sc2) + sh2).astype(BF16)
    d_ff = w1_ref.shape[1]
    fc = 1024
    acc = None
    for c in range(d_ff // fc):
        a = _dot(h, w1_ref[:, c * fc:(c + 1) * fc])
        a = jnp.square(jnp.maximum(a, 0.0)).astype(BF16)
        part = _dot(a, w2_ref[c * fc:(c + 1) * fc, :])
        acc = part if acc is None else acc + part
    x2 = x1 + g2 * acc
    if final:
        x2 = _rms(x2) * fg_ref[...]
    o_ref[...] = x2


def _post(l, final, o_sb, o_sw, o_fx, o_ssm0, o_ssm1, x, mod, out_gain, norm2_gain, w_out, w1, w2,
          final_gain):
    b, s, d = x.shape
    tm = ROW_TILE
    grp = pl.BlockSpec((None, tm, GROUP_WIDTH), lambda bi, i: (bi, i, 0))
    half = pl.BlockSpec((None, tm, LANES), lambda bi, i: (bi, i, 0))
    xspec = pl.BlockSpec((None, tm, d), lambda bi, i: (bi, i, 0))
    lspec = lambda a: pl.BlockSpec((None,) + a.shape[1:], lambda bi, i: (l, 0, 0),
                                   pipeline_mode=pl.Buffered(1))
    return pl.pallas_call(
        functools.partial(_post_kernel, final=final),
        out_shape=jax.ShapeDtypeStruct((b, s, d), F32),
        grid=(b, s // tm),
        in_specs=[grp, grp, grp, half, half, xspec,
                  pl.BlockSpec((None, None, N_ADA, d), lambda bi, i: (l, bi, 0, 0)),
                  lspec(out_gain), lspec(norm2_gain), lspec(w_out), lspec(w1), lspec(w2),
                  pl.BlockSpec(final_gain.shape, lambda bi, i: (0, 0))],
        out_specs=xspec,
        compiler_params=_params("arbitrary", "arbitrary"),
        name="post_mlp",
    )(o_sb, o_sw, o_fx, o_ssm0, o_ssm1, x, mod, out_gain, norm2_gain, w_out, w1, w2, final_gain)


def _swa_head_perm(a, axis):
    shape = a.shape
    a = a.reshape(shape[:axis] + (4, HEAD_DIM) + shape[axis + 1:])
    a = jnp.take(a, jnp.asarray([0, 2, 1, 3]), axis=axis)
    return a.reshape(shape)


def kernel(x, c, w_ada, b_ada, norm1_gain, norm2_gain, w_in, rel_bias, sinks, forget_bias, lam_re, lam_im, log_dt, ssm_b_re, ssm_b_im, ssm_c_re, ssm_c_im, ssm_d, w_glu, b_glu, out_gain, w_out, w_mlp_in, w_mlp_out, final_gain):
    b, s, d = x.shape
    depth = w_in.shape[0]
    g, p = lam_re.shape[1:]
    h = ssm_b_re.shape[-1]
    n_fox = forget_bias.shape[-1]
    scale = 1.0 / math.sqrt(HEAD_DIM)
    assert s % SSM_TILE == 0 and s % (ROW_TILE * INPROJ_SPLIT) == 0
    assert ROW_TILE % SB_TILE == 0 and ROW_TILE % FOX_TILE == 0

    gw, kvw = GROUP_WIDTH, N_KV_SWA * HEAD_DIM
    sizes = (gw, gw, gw, gw, kvw, kvw, gw, gw, gw, n_fox, gw)
    offs = np.concatenate([[0], np.cumsum(sizes)])
    col = lambda k: w_in[:, :, offs[k]:offs[k + 1]]
    w_sbq = (col(0) * scale).astype(BF16)
    w_sbv = col(2).astype(BF16)
    w_swk = col(4).astype(BF16)
    w_fxk = col(7).astype(BF16)
    w_u = col(10).astype(BF16)
    wt = jnp.concatenate([col(1), col(6) * scale, col(8), _swa_head_perm(col(3), 2) * scale, col(5)],
                         axis=2).transpose(0, 2, 1).astype(BF16)
    wf = jnp.pad(col(9), ((0, 0), (0, 0), (0, LANES - n_fox))).astype(BF16)
    forget_b = jnp.pad(forget_bias.astype(F32), ((0, 0), (0, LANES - n_fox)))[:, None, :]
    in_wts = (w_sbq, w_sbv, w_swk, w_fxk, w_u, wt, wf)

    og = out_gain.astype(F32)
    og = jnp.concatenate([og[:, :gw], _swa_head_perm(og[:, gw:2 * gw], 1), og[:, 2 * gw:]], axis=1)
    wo = jnp.concatenate([w_out[:, :gw], _swa_head_perm(w_out[:, gw:2 * gw], 1), w_out[:, 2 * gw:]],
                         axis=1).astype(BF16)
    og = og[:, None, :]
    n1 = norm1_gain.astype(F32)[:, None, :]
    n2 = norm2_gain.astype(F32)[:, None, :]
    w1 = w_mlp_in.astype(BF16)
    w2 = w_mlp_out.astype(BF16)
    fg = final_gain.astype(F32)[None, :]
    bias = _swa_bias(rel_bias)
    sinks_b = jnp.broadcast_to(sinks.astype(F32)[:, :, None], sinks.shape + (LANES,))

    avec, bbr, bbi = _ssm_prep(lam_re.astype(F32), lam_im.astype(F32), log_dt.astype(F32),
                               ssm_b_re.astype(F32), ssm_b_im.astype(F32))
    eye = jnp.eye(g, dtype=F32)
    blockdiag_b = lambda bb: jnp.einsum('lhgp,gk->lghkp', bb.reshape(depth, h, g, p), eye
                                        ).reshape(depth, g * h, g * p)
    bmat = jnp.concatenate([blockdiag_b(bbr), blockdiag_b(bbi)], axis=2).astype(BF16)
    blockdiag_c = lambda cc: jnp.einsum('lghp,gk->lgpkh', cc.astype(F32), eye
                                        ).reshape(depth, g * p, g * h)
    cmat = jnp.concatenate([blockdiag_c(ssm_c_re), -blockdiag_c(ssm_c_im)], axis=1).astype(BF16)
    d_skip = ssm_d.astype(F32).reshape(depth, 1, g * h)
    wglu = w_glu.astype(BF16)
    bglu = b_glu.astype(F32)[:, None, :]

    c_pad = jnp.pad(c.astype(F32), ((0, 8 - b), (0, 0)))
    mod = _ada(c_pad, w_ada.astype(F32), b_ada.astype(F32))[:, :b].reshape(depth, b, N_ADA, d)

    x = x.astype(F32)
    for l in range(depth):
        (sbq, sbv, swk, fxk, u0, u1, sbkt, fxqt, fxvt, swqt, swvt) = _inproj(l, x, mod, n1, in_wts, forget_b)
        o_sb = _stick_breaking(sbq, sbkt, sbv)
        o_sw = _sliding_window(l, swqt, swk, swvt, bias, sinks_b)
        o_fx = _forgetting(fxqt, fxk, fxvt)
        o_ssm0, o_ssm1 = _ssm(l, u0, u1, avec, bmat, cmat, d_skip, wglu, bglu)
        x = _post(l, l == depth - 1, o_sb, o_sw, o_fx, o_ssm0, o_ssm1, x, mod, og, n2, wo, w1, w2, fg)
    return x
```

```python
import functools
import math

import numpy as np
import jax
import jax.numpy as jnp
from jax import lax
from jax.experimental import pallas as pl
from jax.experimental.pallas import tpu as pltpu

F32 = jnp.float32
BF16 = jnp.bfloat16

HEAD_DIM = 64
LANES = 128
GROUP_WIDTH = 256
N_KV_SWA = 2
WINDOW = 128
REL_BUCKETS = 32
REL_MAX_DIST = 128
N_ADA = 6
NORM_EPS = 1e-6
SSM_H = 16
SSM_P = 64
SSM_CHUNK = 16

ROW_TILE = 512
INPROJ_SPLIT = 2
SB_TILE = 256
FOX_TILE = 512
SSM_TILE = 1024
VMEM_LIMIT_BYTES = 56 * 1024 * 1024
EXP_UNDERFLOW = -105.0
FAST_MARGIN = 60.0
FOX_AUG = 8


def _dot(a, b):
    return jnp.dot(a, b, preferred_element_type=F32)


def _dot_nt(a, b):
    return lax.dot_general(a, b, (((1,), (1,)), ((), ())), preferred_element_type=F32)


def _split_bf16(a):
    hi = a.astype(BF16)
    lo = (a - hi.astype(F32)).astype(BF16)
    return hi, lo


def _rms(x):
    return x * lax.rsqrt(jnp.mean(x * x, axis=-1, keepdims=True) + NORM_EPS)


def _params(*sem):
    return pltpu.CompilerParams(dimension_semantics=sem, vmem_limit_bytes=VMEM_LIMIT_BYTES)


def _ada_kernel(c_ref, w_ref, b_ref, o_ref):
    c = c_ref[...]
    ca = c * jax.nn.sigmoid(c)
    a_hi, a_lo = _split_bf16(ca)
    w_hi, w_lo = _split_bf16(w_ref[...])
    o_ref[...] = _dot(a_hi, w_hi) + _dot(a_hi, w_lo) + _dot(a_lo, w_hi) + b_ref[...]


def _ada(c_pad, w_ada, b_ada):
    depth, d, n = w_ada.shape
    tn = 2048
    return pl.pallas_call(
        _ada_kernel,
        out_shape=jax.ShapeDtypeStruct((depth, c_pad.shape[0], n), F32),
        grid=(depth, n // tn),
        in_specs=[
            pl.BlockSpec(c_pad.shape, lambda l, j: (0, 0)),
            pl.BlockSpec((None, d, tn), lambda l, j: (l, 0, j)),
            pl.BlockSpec((None, 1, tn), lambda l, j: (l, 0, j)),
        ],
        out_specs=pl.BlockSpec((None, c_pad.shape[0], tn), lambda l, j: (l, 0, j)),
        compiler_params=_params("arbitrary", "arbitrary"),
        name="ada_mod",
    )(c_pad, w_ada, b_ada.reshape(depth, 1, n))


def _ssm_prep_kernel(lr_ref, li_ref, ldt_ref, br_ref, bi_ref, avec_ref, bbr_ref, bbi_ref):
    lr = lr_ref[...]
    li = li_ref[...]
    dt = jnp.exp(ldt_ref[...])
    mag = jnp.exp(lr * dt)
    ang = li * dt
    ar = mag * jnp.cos(ang)
    ai = mag * jnp.sin(ang)
    den = lr * lr + li * li
    nr = ar - 1.0
    ni = ai
    cr = (nr * lr + ni * li) / den
    ci = (ni * lr - nr * li) / den
    pr, pi_ = ar, ai
    for _ in range(int(math.log2(SSM_CHUNK))):
        pr, pi_ = pr * pr - pi_ * pi_, 2.0 * pr * pi_
    avec_ref[0:1, :] = ar
    avec_ref[1:2, :] = ai
    avec_ref[2:3, :] = pr
    avec_ref[3:4, :] = pi_
    avec_ref[4:8, :] = jnp.zeros((4, ar.shape[1]), F32)
    br = br_ref[...]
    bi = bi_ref[...]
    bbr_ref[...] = cr * br - ci * bi
    bbi_ref[...] = cr * bi + ci * br


def _ssm_prep(lam_re, lam_im, log_dt, b_re, b_im):
    depth, g, p = lam_re.shape
    h = b_re.shape[-1]
    n = g * p
    row = lambda a: a.reshape(depth, 1, n)
    ldt = jnp.broadcast_to(log_dt[:, :, None], (depth, g, p))
    bt = lambda a: a.transpose(0, 3, 1, 2).reshape(depth, h, n)
    vec = pl.BlockSpec((None, 1, n), lambda l: (l, 0, 0))
    mat = pl.BlockSpec((None, h, n), lambda l: (l, 0, 0))
    return pl.pallas_call(
        _ssm_prep_kernel,
        out_shape=(jax.ShapeDtypeStruct((depth, 8, n), F32),
                   jax.ShapeDtypeStruct((depth, h, n), F32),
                   jax.ShapeDtypeStruct((depth, h, n), F32)),
        grid=(depth,),
        in_specs=[vec, vec, vec, mat, mat],
        out_specs=(pl.BlockSpec((None, 8, n), lambda l: (l, 0, 0)), mat, mat),
        compiler_params=_params("arbitrary"),
        name="ssm_prep",
    )(row(lam_re), row(lam_im), row(ldt), bt(b_re), bt(b_im))


def _inproj_kernel(x_ref, mod_ref, gain_ref, w_sbq, w_sbv, w_swk, w_fxk, w_u,
                   wt_ref, wf_ref, fb_ref, sel_ref,
                   sbq_o, sbv_o, swk_o, fxk_o, u0_o, u1_o, sbkt_o, fxqt_o, fxvt_o, swqt_o, swvt_o,
                   carry_ref):
    hr = x_ref.shape[0] // INPROJ_SPLIT

    @pl.when(pl.program_id(1) == 0)
    def _():
        carry_ref[...] = jnp.zeros_like(carry_ref)

    mod = mod_ref[...]
    sh1 = mod[0:1]
    sc1 = mod[1:2]
    gain = gain_ref[...]
    hs = [(_rms(x_ref[r * hr:(r + 1) * hr, :]) * gain * (1.0 + sc1) + sh1).astype(BF16)
          for r in range(INPROJ_SPLIT)]
    step = lax.broadcasted_iota(jnp.int32, (hr, LANES), 0)
    carry = carry_ref[0:1, :]
    for r, h in enumerate(hs):
        rows = slice(r * hr, (r + 1) * hr)
        f = _dot(h, wf_ref[...]) + fb_ref[...]
        logf = jnp.minimum(f, 0.0) - jnp.log1p(jnp.exp(-jnp.abs(f)))
        cum = logf
        k = 1
        while k < hr:
            cum = cum + jnp.where(step >= k, pltpu.roll(cum, k, axis=0), 0.0)
            k *= 2
        cum = cum + carry
        carry = cum[hr - 1:hr, :]
        negf = -cum
        hi = negf.astype(BF16)
        r1 = negf - hi.astype(F32)
        mid = r1.astype(BF16)
        lo3 = (r1 - mid.astype(F32)).astype(BF16)
        aug = _dot(jnp.concatenate([hi, mid, lo3], axis=1), sel_ref[...]).astype(BF16)

        k_fx = _dot(h, w_fxk[...])
        n_pairs = fxk_o.shape[1] // (2 * LANES)
        for p in range(n_pairs):
            base = 2 * p * LANES
            fxk_o[rows, base:base + LANES] = k_fx[:, p * LANES:(p + 1) * LANES].astype(BF16)
            fxk_o[rows, base + LANES:base + 2 * LANES] = aug[:, p * LANES:(p + 1) * LANES]

        sbq_o[rows, :] = _dot(h, w_sbq[...]).astype(BF16)
        sbv_o[rows, :] = _dot(h, w_sbv[...]).astype(BF16)
        swk_o[rows, :] = _dot(h, w_swk[...]).astype(BF16)
        u = _dot(h, w_u[...])
        u0_o[rows, :] = u[:, :LANES]
        u1_o[rows, :] = u[:, LANES:]
        tt = _dot_nt(wt_ref[...], h)
        row0 = 0
        for o in (sbkt_o, fxqt_o, fxvt_o, swqt_o, swvt_o):
            n_sub, width, tile = o.shape
            per_part = n_sub // INPROJ_SPLIT
            for c in range(per_part):
                o[r * per_part + c] = tt[row0:row0 + width, c * tile:(c + 1) * tile].astype(BF16)
            row0 += width
    carry_ref[...] = jnp.broadcast_to(carry, carry_ref.shape)


def _fox_aug_select():
    n_pairs = GROUP_WIDTH // LANES
    sel = np.zeros((3 * LANES, n_pairs * LANES), np.float32)
    for p in range(n_pairs):
        for j in range(2):
            for r in range(3):
                sel[LANES * r + 2 * p + j, LANES * p + FOX_AUG * j + r] = 1.0
    return jnp.asarray(sel, BF16)


def _inproj(l, x, mod, gain, wts, forget_b):
    b, s, d = x.shape
    ts = ROW_TILE * INPROJ_SPLIT
    nk = s // ts
    (w_sbq, w_sbv, w_swk, w_fxk, w_u, wt, wf) = wts
    sel = _fox_aug_select()
    row = lambda width: pl.BlockSpec((None, ts, width), lambda bi, i: (bi, i, 0))
    wspec = lambda w: pl.BlockSpec((None,) + w.shape[1:], lambda bi, i: (l, 0, 0))
    tspec = lambda width, tile: pl.BlockSpec((None, ts // tile, width, tile), lambda bi, i: (bi, i, 0, 0))
    kv_w = N_KV_SWA * HEAD_DIM
    out_shape = (
        jax.ShapeDtypeStruct((b, s, GROUP_WIDTH), BF16),
        jax.ShapeDtypeStruct((b, s, GROUP_WIDTH), BF16),
        jax.ShapeDtypeStruct((b, s, kv_w), BF16),
        jax.ShapeDtypeStruct((b, s, 2 * GROUP_WIDTH), BF16),
        jax.ShapeDtypeStruct((b, s, LANES), F32),
        jax.ShapeDtypeStruct((b, s, LANES), F32),
        jax.ShapeDtypeStruct((b, s // SB_TILE, GROUP_WIDTH, SB_TILE), BF16),
        jax.ShapeDtypeStruct((b, s // FOX_TILE, GROUP_WIDTH, FOX_TILE), BF16),
        jax.ShapeDtypeStruct((b, s // FOX_TILE, GROUP_WIDTH, FOX_TILE), BF16),
        jax.ShapeDtypeStruct((b, s // ROW_TILE, GROUP_WIDTH, ROW_TILE), BF16),
        jax.ShapeDtypeStruct((b, s // ROW_TILE, kv_w, ROW_TILE), BF16),
    )
    out_specs = (row(GROUP_WIDTH), row(GROUP_WIDTH), row(kv_w), row(2 * GROUP_WIDTH),
                 row(LANES), row(LANES), tspec(GROUP_WIDTH, SB_TILE), tspec(GROUP_WIDTH, FOX_TILE),
                 tspec(GROUP_WIDTH, FOX_TILE), tspec(GROUP_WIDTH, ROW_TILE), tspec(kv_w, ROW_TILE))
    return pl.pallas_call(
        _inproj_kernel,
        out_shape=out_shape,
        grid=(b, nk),
        in_specs=[
            pl.BlockSpec((None, ts, d), lambda bi, i: (bi, i, 0)),
            pl.BlockSpec((None, None, N_ADA, d), lambda bi, i: (l, bi, 0, 0)),
            pl.BlockSpec((None, 1, d), lambda bi, i: (l, 0, 0)),
            wspec(w_sbq), wspec(w_sbv), wspec(w_swk), wspec(w_fxk), wspec(w_u),
            wspec(wt), wspec(wf),
            pl.BlockSpec((None, 1, LANES), lambda bi, i: (l, 0, 0)),
            pl.BlockSpec(sel.shape, lambda bi, i: (0, 0)),
        ],
        out_specs=out_specs,
        scratch_shapes=[pltpu.VMEM((8, LANES), F32)],
        compiler_params=_params("arbitrary", "arbitrary"),
        name="in_proj",
    )(x, mod, gain, w_sbq, w_sbv, w_swk, w_fxk, w_u, wt, wf, forget_b, sel)


def _pair_split(q):
    lane = lax.broadcasted_iota(jnp.int32, q.shape, 1)
    left = lane < HEAD_DIM
    zero = jnp.zeros_like(q)
    return left, (jnp.where(left, q, zero), jnp.where(left, zero, q))


def _sb_kernel(q_ref, kt_ref, v_ref, suf_ref, o_ref):
    t = q_ref.shape[0]
    n_pairs = q_ref.shape[1] // LANES
    qi = pl.program_id(1)
    split = [_pair_split(q_ref[:, p * LANES:(p + 1) * LANES]) for p in range(n_pairs)]
    left = split[0][0]
    row = lax.broadcasted_iota(jnp.int32, (t, t), 0)
    col = lax.broadcasted_iota(jnp.int32, (t, t), 1)
    strict = col < row
    suf = suf_ref[...]

    def tile(ki, carry):
        run, acc = carry
        rows = pl.ds(pl.multiple_of(ki * t, t), t)
        heads = [(p, j) for p in range(n_pairs) for j in range(2)]
        z = [_dot(split[p][1][j], kt_ref[ki, p * LANES:(p + 1) * LANES, :]) for p, j in heads]
        lsz, lk = [], []
        for zh in z:
            sp = jnp.log(1.0 + jnp.exp(-jnp.abs(zh)))
            lszh = jnp.minimum(zh, 0.0) - sp
            lkh = lszh - zh
            lsz.append(lszh)
            lk.append(lkh)
        within = [_dot(lkh.astype(BF16), suf) for lkh in lk]
        w = [jnp.exp(lsz[h] + within[h] + run[h]).astype(BF16) for h in range(len(heads))]
        pv = [_dot(w[h], v_ref[rows, p * LANES:(p + 1) * LANES]) for h, (p, j) in enumerate(heads)]
        new_run = tuple(run[h] + jnp.sum(lk[h], axis=-1, keepdims=True) for h in range(len(heads)))
        new_acc = tuple(acc[p] + jnp.where(left, pv[2 * p], pv[2 * p + 1]) for p in range(n_pairs))
        return new_run, new_acc

    def alive(run):
        top = run[0]
        for r in run[1:]:
            top = jnp.maximum(top, r)
        return jnp.max(top) > EXP_UNDERFLOW

    def first_two():
        prev = jnp.maximum(qi - 1, 0)
        has_prev = qi >= 1
        heads = [(p, j) for p in range(n_pairs) for j in range(2)]
        nh = len(heads)
        tiles = (qi, prev)
        z = [[_dot(split[p][1][j], kt_ref[ki, p * LANES:(p + 1) * LANES, :]) for p, j in heads]
             for ki in tiles]
        lsz, lk = [[], []], [[], []]
        for a in range(2):
            for zh in z[a]:
                sp = jnp.log(1.0 + jnp.exp(-jnp.abs(zh)))
                lszh = jnp.minimum(zh, 0.0) - sp
                lkh = lszh - zh
                lsz[a].append(lszh)
                lk[a].append(jnp.where(strict, lkh, 0.0) if a == 0 else jnp.where(has_prev, lkh, 0.0))
        within = [[_dot(lkh.astype(BF16), suf) for lkh in lk[a]] for a in range(2)]
        run_mid = [jnp.sum(lk[0][h], axis=-1, keepdims=True) for h in range(nh)]
        w0 = [jnp.where(strict, jnp.exp(lsz[0][h] + within[0][h]), 0.0).astype(BF16) for h in range(nh)]
        w1 = [jnp.where(has_prev, jnp.exp(lsz[1][h] + within[1][h] + run_mid[h]), 0.0).astype(BF16)
              for h in range(nh)]
        rows0 = pl.ds(pl.multiple_of(qi * t, t), t)
        rows1 = pl.ds(pl.multiple_of(prev * t, t), t)
        pv = [_dot(w0[h], v_ref[rows0, p * LANES:(p + 1) * LANES])
              + _dot(w1[h], v_ref[rows1, p * LANES:(p + 1) * LANES]) for h, (p, j) in enumerate(heads)]
        run = tuple(run_mid[h] + jnp.sum(lk[1][h], axis=-1, keepdims=True) for h in range(nh))
        acc = tuple(jnp.where(left, pv[2 * p], pv[2 * p + 1]) for p in range(n_pairs))
        return run, acc

    state = first_two()

    def body(c):
        i, _, st = c
        st = tile(qi - 1 - i, st)
        return i + 1, alive(st[0]), st

    _, _, state = lax.while_loop(lambda c: jnp.logical_and(c[0] < qi, c[1]), body,
                                 (jnp.int32(1), alive(state[0]), state))
    for p in range(n_pairs):
        o_ref[:, p * LANES:(p + 1) * LANES] = state[1][p]


def _fox_kernel(qt_ref, k_ref, vt_ref, o_ref, bound_ref):
    t = qt_ref.shape[1]
    nk = vt_ref.shape[0]
    qi = pl.program_id(2)
    qt = qt_ref[...]
    sub = lax.broadcasted_iota(jnp.int32, (LANES, t), 0)
    top = sub < HEAD_DIM
    zero = jnp.zeros_like(qt)
    q2 = qt.astype(F32) * qt.astype(F32)
    qa, qn = [], []
    for j in range(2):
        mine = top if j == 0 else jnp.logical_not(top)
        ones = jnp.where(jnp.logical_and(sub >= FOX_AUG * j, sub < FOX_AUG * j + 3), 1.0, 0.0)
        qa.append(jnp.concatenate([jnp.where(mine, qt, zero), ones.astype(BF16)], axis=0))
        qn.append(jnp.sqrt(jnp.sum(jnp.where(mine, q2, 0.0), axis=0, keepdims=True)))
    key = lax.broadcasted_iota(jnp.int32, (t, t), 0)
    qry = lax.broadcasted_iota(jnp.int32, (t, t), 1)
    causal = key <= qry
    lane = lax.broadcasted_iota(jnp.int32, (t, LANES), 1)

    @pl.when(qi == 0)
    def _():
        def scan(ki, c):
            kk = k_ref[pl.ds(pl.multiple_of(ki * t, t), t), :].astype(F32)
            k2 = kk[:, :LANES] * kk[:, :LANES]
            aug = kk[:, LANES:]
            out_kn, out_nf = [], []
            for j in range(2):
                mine = (lane < HEAD_DIM) if j == 0 else (lane >= HEAD_DIM)
                n2 = jnp.sum(jnp.where(mine, k2, 0.0), axis=1, keepdims=True)
                here = jnp.logical_and(lane >= FOX_AUG * j, lane < FOX_AUG * (j + 1))
                negf = jnp.sum(jnp.where(here, aug, 0.0), axis=1, keepdims=True)
                kn = jnp.maximum(c[j], jnp.sqrt(jnp.max(n2)))
                nf = jnp.maximum(c[2 + j], jnp.max(negf))
                bound_ref[j, ki] = kn
                bound_ref[2 + j, ki] = nf
                out_kn.append(kn)
                out_nf.append(nf)
            return tuple(out_kn + out_nf)

        lax.fori_loop(0, nk, scan, (jnp.float32(0.0),) * 2 + (jnp.float32(-jnp.inf),) * 2)

    def tile(ki, carry, mode):
        m, l, acc = carry
        kk = k_ref[pl.ds(pl.multiple_of(ki * t, t), t), :]
        vt = vt_ref[ki]
        s = [_dot(kk, qa[j]) for j in range(2)]
        if mode == "diag":
            s = [jnp.where(causal, sj, -jnp.inf) for sj in s]
        if mode == "fast":
            new_m = m
        else:
            new_m = tuple(jnp.maximum(m[j], jnp.max(s[j], axis=0, keepdims=True)) for j in range(2))
        e = [jnp.exp(s[j] - new_m[j]) for j in range(2)]
        pv = [_dot(vt, e[j].astype(BF16)) for j in range(2)]
        colsum = [jnp.sum(e[j], axis=0, keepdims=True) for j in range(2)]
        if mode == "fast":
            new_l = tuple(l[j] + colsum[j] for j in range(2))
        else:
            corr = [jnp.exp(m[j] - new_m[j]) for j in range(2)]
            new_l = tuple(l[j] * corr[j] + colsum[j] for j in range(2))
            acc = acc * jnp.where(top, corr[0], corr[1])
        return new_m, new_l, acc + jnp.where(top, pv[0], pv[1])

    def row_gap(m, ki):
        kic = jnp.maximum(ki, 0)
        return tuple(jnp.max(qn[j] * bound_ref[j, kic] - m[j]) for j in range(2))

    def gap(rg, ki):
        kic = jnp.maximum(ki, 0)
        return jnp.maximum(rg[0] + bound_ref[2, kic], rg[1] + bound_ref[3, kic])

    ninf = jnp.full((1, t), -jnp.inf, F32)
    zrow = jnp.zeros((1, t), F32)
    state = tile(qi, ((ninf, ninf), (zrow, zrow), jnp.zeros((LANES, t), F32)), "diag")

    def general(c):
        i, _, st = c
        ki = qi - 1 - i
        st = tile(ki, st, "general")
        return i + 1, gap(row_gap(st[0], ki - 1), ki - 1), st

    i0, _, (m, l, acc) = lax.while_loop(
        lambda c: jnp.logical_and(c[0] < qi, c[1] >= FAST_MARGIN), general,
        (jnp.int32(0), gap(row_gap(state[0], qi - 1), qi - 1), state))
    rg = row_gap(m, qi - 1 - i0)

    def fast_pair(c):
        i, _, l, acc = c
        ki = qi - 1 - i
        kk = [k_ref[pl.ds(pl.multiple_of((ki - d) * t, t), t), :] for d in range(2)]
        s = [[_dot(kk[d], qa[j]) for j in range(2)] for d in range(2)]
        e = [[jnp.exp(s[d][j] - m[j]) for j in range(2)] for d in range(2)]
        pv = [_dot(vt_ref[ki, :, :], e[0][j].astype(BF16)) + _dot(vt_ref[ki - 1, :, :], e[1][j].astype(BF16))
              for j in range(2)]
        l = tuple(l[j] + jnp.sum(e[0][j], axis=0, keepdims=True) + jnp.sum(e[1][j], axis=0, keepdims=True)
                  for j in range(2))
        return i + 2, gap(rg, ki - 3), l, acc + jnp.where(top, pv[0], pv[1])

    i1, _, l, acc = lax.while_loop(
        lambda c: jnp.logical_and(c[0] + 1 < qi, c[1] > EXP_UNDERFLOW), fast_pair,
        (i0, gap(rg, qi - 2 - i0), l, acc))

    def fast(c):
        i, _, l, acc = c
        ki = qi - 1 - i
        _, l, acc = tile(ki, (m, l, acc), "fast")
        return i + 1, gap(rg, ki - 1), l, acc

    _, _, l, acc = lax.while_loop(
        lambda c: jnp.logical_and(c[0] < qi, c[1] > EXP_UNDERFLOW), fast,
        (i1, gap(rg, qi - 1 - i1), l, acc))
    o_ref[...] = (acc / jnp.where(top, l[0], l[1])).T


def _stick_breaking(q, kt, v):
    b, s, width = q.shape
    t = SB_TILE
    nk = s // t
    suffix = jnp.asarray(np.tril(np.ones((t, t), np.float32), -1), BF16)
    once = pl.Buffered(1)
    return pl.pallas_call(
        _sb_kernel,
        out_shape=jax.ShapeDtypeStruct((b, s, width), F32),
        grid=(b, nk),
        in_specs=[
            pl.BlockSpec((None, t, width), lambda bi, i: (bi, i, 0)),
            pl.BlockSpec((None, nk, width, t), lambda bi, i: (bi, 0, 0, 0), pipeline_mode=once),
            pl.BlockSpec((None, s, width), lambda bi, i: (bi, 0, 0), pipeline_mode=once),
            pl.BlockSpec((t, t), lambda bi, i: (0, 0), pipeline_mode=once),
        ],
        out_specs=pl.BlockSpec((None, t, width), lambda bi, i: (bi, i, 0)),
        compiler_params=_params("arbitrary", "arbitrary"),
        name="stick_breaking",
    )(q, kt, v, suffix)


def _forgetting(qt, k, vt):
    b, nq, width, t = qt.shape
    s = nq * t
    return pl.pallas_call(
        _fox_kernel,
        out_shape=jax.ShapeDtypeStruct((b, s, width), F32),
        grid=(b, width // LANES, nq),
        in_specs=[
            pl.BlockSpec((None, None, LANES, t), lambda bi, p, i: (bi, i, p, 0)),
            pl.BlockSpec((None, s, 2 * LANES), lambda bi, p, i: (bi, 0, p)),
            pl.BlockSpec((None, nq, LANES, t), lambda bi, p, i: (bi, 0, p, 0)),
        ],
        out_specs=pl.BlockSpec((None, t, LANES), lambda bi, p, i: (bi, i, p)),
        scratch_shapes=[pltpu.SMEM((4, nq), F32)],
        compiler_params=_params("arbitrary", "arbitrary", "arbitrary"),
        name="forgetting",
    )(qt, k, vt)


def _swa_kernel(qt_ref, kc_ref, kp_ref, vtc_ref, vtp_ref, bias_ref, sink_ref, o_ref):
    r = kc_ref.shape[0]
    first = pl.program_id(1) == 0
    sub = lax.broadcasted_iota(jnp.int32, (LANES, WINDOW), 0)
    top = sub < HEAD_DIM
    key = lax.broadcasted_iota(jnp.int32, (2 * WINDOW, WINDOW), 0)
    n_win = r // WINDOW
    ks, vts = [], []
    for w in range(n_win):
        lo = w * WINDOW
        if w == 0:
            ks.append(jnp.concatenate([kp_ref[...], kc_ref[:WINDOW, :]], axis=0))
            vts.append(jnp.concatenate([vtp_ref[0, :, r - WINDOW:], vtc_ref[0, :, :WINDOW]], axis=1))
        else:
            ks.append(kc_ref[lo - WINDOW:lo + WINDOW, :])
            vts.append(vtc_ref[0, :, lo - WINDOW:lo + WINDOW])
    units = [(w, g, kv) for w in range(n_win) for g in range(2) for kv in range(N_KV_SWA)]
    z = []
    for w, g, kv in units:
        qg = qt_ref[0, g * LANES:(g + 1) * LANES, w * WINDOW:(w + 1) * WINDOW]
        zero = jnp.zeros_like(qg)
        qm = jnp.where(top, qg, zero) if kv == 0 else jnp.where(top, zero, qg)
        zu = _dot(ks[w], qm) + bias_ref[2 * kv + g]
        if w == 0:
            zu = jnp.where(jnp.logical_and(first, key < WINDOW), -jnp.inf, zu)
        z.append(zu)
    p = []
    for (w, g, kv), zu in zip(units, z):
        head = 2 * kv + g
        sink = sink_ref[head:head + 1, 0:1]
        m = jnp.maximum(jnp.max(zu, axis=0, keepdims=True), sink)
        e = jnp.exp(zu - m)
        den = jnp.sum(e, axis=0, keepdims=True) + jnp.exp(sink - m)
        p.append((e * (1.0 / den)).astype(BF16))
    pv = {u: _dot(vts[u[0]], pu) for u, pu in zip(units, p)}
    for w in range(n_win):
        for g in range(2):
            o_ref[w * WINDOW:(w + 1) * WINDOW, g * LANES:(g + 1) * LANES] = jnp.where(
                top, pv[(w, g, 0)], pv[(w, g, 1)]).T


def _sliding_window(l, qt, k, vt, bias_t, sinks):
    b, s, kv_w = k.shape
    r = ROW_TILE
    wpt = r // WINDOW
    return pl.pallas_call(
        _swa_kernel,
        out_shape=jax.ShapeDtypeStruct((b, s, GROUP_WIDTH), F32),
        grid=(b, s // r),
        in_specs=[
            pl.BlockSpec((None, 1, GROUP_WIDTH, r), lambda bi, i: (bi, i, 0, 0)),
            pl.BlockSpec((None, r, kv_w), lambda bi, i: (bi, i, 0)),
            pl.BlockSpec((None, WINDOW, kv_w), lambda bi, i: (bi, jnp.maximum(i * wpt - 1, 0), 0)),
            pl.BlockSpec((None, 1, kv_w, r), lambda bi, i: (bi, i, 0, 0)),
            pl.BlockSpec((None, 1, kv_w, r), lambda bi, i: (bi, jnp.maximum(i - 1, 0), 0, 0)),
            pl.BlockSpec(bias_t.shape, lambda bi, i: (0, 0, 0)),
            pl.BlockSpec((None,) + sinks.shape[1:], lambda bi, i: (l, 0, 0)),
        ],
        out_specs=pl.BlockSpec((None, r, GROUP_WIDTH), lambda bi, i: (bi, i, 0)),
        compiler_params=_params("arbitrary", "arbitrary"),
        name="sliding_window",
    )(qt, k, k, vt, vt, bias_t, sinks)


def _swa_bias(rel_bias):
    i = np.arange(WINDOW)[:, None]
    j = np.arange(2 * WINDOW)[None, :]
    dist = WINDOW + i - j
    in_window = (dist >= 0) & (dist < WINDOW)
    d = np.clip(dist, 0, None)
    max_exact = REL_BUCKETS // 2
    safe = np.maximum(d, 1).astype(np.float32)
    large = max_exact + (np.log(safe / max_exact) / math.log(REL_MAX_DIST / max_exact)
                         * (REL_BUCKETS - max_exact)).astype(np.int32)
    large = np.minimum(large, REL_BUCKETS - 1)
    bucket = np.where(d < max_exact, d, large).astype(np.int32)
    onehot = jnp.asarray(np.eye(REL_BUCKETS, dtype=np.float32)[bucket.reshape(-1)])
    bias = jnp.dot(onehot, rel_bias.astype(F32), precision=lax.Precision.HIGHEST)
    bias = bias.T.reshape(-1, WINDOW, 2 * WINDOW)
    return jnp.where(jnp.asarray(in_window)[None], bias, -jnp.inf).transpose(0, 2, 1)


def _ssm_kernel(u0_ref, u1_ref, avec_ref, bmat_ref, cmat_ref, d_ref, wglu_ref, bglu_ref,
                o0_ref, o1_ref, up_ref, xr_ref, xi_ref, er_ref, ei_ref, sr_ref, si_ref, carry_ref):
    n_slab, ts, _ = xr_ref.shape
    n = ts // SSM_CHUNK
    np_ = n_slab * LANES

    @pl.when(pl.program_id(1) == 0)
    def _():
        carry_ref[...] = jnp.zeros_like(carry_ref)

    def rows(i):
        return pl.ds(i * n, n)

    for k, u_ref in enumerate((u0_ref, u1_ref)):
        for i in range(SSM_CHUNK):
            up_ref[rows(i), k * LANES:(k + 1) * LANES] = u_ref[pl.ds(i, n, stride=SSM_CHUNK), :]
    u = up_ref[...]
    bu = _dot(u.astype(BF16), bmat_ref[...])
    for j in range(n_slab):
        xr_ref[j] = bu[:, j * LANES:(j + 1) * LANES]
        xi_ref[j] = bu[:, np_ + j * LANES:np_ + (j + 1) * LANES]

    def coef(k, j):
        return avec_ref[k:k + 1, j * LANES:(j + 1) * LANES]

    for j in range(n_slab):
        ar, ai = coef(0, j), coef(1, j)
        pr, pi_ = xr_ref[j, rows(0), :], xi_ref[j, rows(0), :]
        for i in range(1, SSM_CHUNK):
            nr = ar * pr - ai * pi_ + xr_ref[j, rows(i), :]
            ni = ar * pi_ + ai * pr + xi_ref[j, rows(i), :]
            xr_ref[j, rows(i), :] = nr
            xi_ref[j, rows(i), :] = ni
            pr, pi_ = nr, ni
        er_ref[j] = pr
        ei_ref[j] = pi_

    a16 = [(coef(2, j), coef(3, j)) for j in range(n_slab)]

    def chunk(c, s):
        out = []
        for j in range(n_slab):
            sr, si = s[j]
            sr_ref[j, pl.ds(c, 1), :] = sr
            si_ref[j, pl.ds(c, 1), :] = si
            er = er_ref[j, pl.ds(c, 1), :]
            ei = ei_ref[j, pl.ds(c, 1), :]
            a16r, a16i = a16[j]
            out.append((a16r * sr - a16i * si + er, a16r * si + a16i * sr + ei))
        return tuple(out)

    init = tuple((carry_ref[0:1, j * LANES:(j + 1) * LANES], carry_ref[1:2, j * LANES:(j + 1) * LANES])
                 for j in range(n_slab))
    fin = lax.fori_loop(0, n, chunk, init)
    for j in range(n_slab):
        carry_ref[0:1, j * LANES:(j + 1) * LANES] = fin[j][0]
        carry_ref[1:2, j * LANES:(j + 1) * LANES] = fin[j][1]

    for j in range(n_slab):
        ar, ai = coef(0, j), coef(1, j)
        pr, pi_ = sr_ref[j], si_ref[j]
        for i in range(SSM_CHUNK):
            pr, pi_ = ar * pr - ai * pi_, ar * pi_ + ai * pr
            xr_ref[j, rows(i), :] = xr_ref[j, rows(i), :] + pr
            xi_ref[j, rows(i), :] = xi_ref[j, rows(i), :] + pi_

    xr = jnp.concatenate([xr_ref[j].astype(BF16) for j in range(n_slab)], axis=1)
    xi = jnp.concatenate([xi_ref[j].astype(BF16) for j in range(n_slab)], axis=1)
    y = _dot(xr, cmat_ref[0:np_, :]) + _dot(xi, cmat_ref[np_:, :]) + d_ref[...] * u
    y = 0.5 * y * (1.0 + jnp.tanh(math.sqrt(2.0 / math.pi) * (y + 0.044715 * (y * y * y))))
    gate = jax.nn.sigmoid(_dot(y.astype(BF16), wglu_ref[...]) + bglu_ref[...])
    out = y * gate
    for k, o_ref in enumerate((o0_ref, o1_ref)):
        for i in range(SSM_CHUNK):
            o_ref[pl.ds(i, n, stride=SSM_CHUNK), :] = out[i * n:(i + 1) * n, k * LANES:(k + 1) * LANES]


def _ssm(l, u0, u1, avec, bmat, cmat, d_skip, w_glu, b_glu):
    b, s, hw = u0.shape
    ts = min(SSM_TILE, s)
    np_ = avec.shape[-1]
    n = ts // SSM_CHUNK
    n_slab = np_ // LANES
    lspec = lambda a: pl.BlockSpec((None,) + a.shape[1:], lambda bi, i: (l, 0, 0))
    half = pl.BlockSpec((None, ts, hw), lambda bi, i: (bi, i, 0))
    return pl.pallas_call(
        _ssm_kernel,
        out_shape=(jax.ShapeDtypeStruct((b, s, hw), F32), jax.ShapeDtypeStruct((b, s, hw), F32)),
        grid=(b, s // ts),
        in_specs=[half, half,
                  lspec(avec), lspec(bmat), lspec(cmat), lspec(d_skip), lspec(w_glu), lspec(b_glu)],
        out_specs=(half, half),
        scratch_shapes=[pltpu.VMEM((ts, 2 * hw), F32),
                        pltpu.VMEM((n_slab, ts, LANES), F32), pltpu.VMEM((n_slab, ts, LANES), F32),
                        pltpu.VMEM((n_slab, n, LANES), F32), pltpu.VMEM((n_slab, n, LANES), F32),
                        pltpu.VMEM((n_slab, n, LANES), F32), pltpu.VMEM((n_slab, n, LANES), F32),
                        pltpu.VMEM((8, np_), F32)],
        compiler_params=_params("arbitrary", "arbitrary"),
        name="s5_ssm",
    )(u0, u1, avec, bmat, cmat, d_skip, w_glu, b_glu)


def _post_kernel(osb_ref, osw_ref, ofx_ref, ossm0_ref, ossm1_ref, x_ref, mod_ref, og_ref, n2_ref,
                 wout_ref, w1_ref, w2_ref, fg_ref, o_ref, *, final):
    mod = mod_ref[...]
    g1, sh2, sc2, g2 = mod[2:3], mod[3:4], mod[4:5], mod[5:6]
    ssm = jnp.concatenate([ossm0_ref[...], ossm1_ref[...]], axis=1)
    mo = None
    for k, o in enumerate((osb_ref[...], osw_ref[...], ofx_ref[...], ssm)):
        lo = k * GROUP_WIDTH
        y = (_rms(o) * og_ref[:, lo:lo + GROUP_WIDTH]).astype(BF16)
        part = _dot(y, wout_ref[lo:lo + GROUP_WIDTH, :])
        mo = part if mo is None else mo + part
    x1 = x_ref[...] + g1 * mo
    h = (_rms(x1) * n2_ref[...] * (1.0 + sc2) + sh2).astype(BF16)
    d_ff = w1_ref.shape[1]
    fc = 1024
    acc = None
    for c in range(d_ff // fc):
        a = _dot(h, w1_ref[:, c * fc:(c + 1) * fc])
        a = jnp.square(jnp.maximum(a, 0.0)).astype(BF16)
        part = _dot(a, w2_ref[c * fc:(c + 1) * fc, :])
        acc = part if acc is None else acc + part
    x2 = x1 + g2 * acc
    if final:
        x2 = _rms(x2) * fg_ref[...]
    o_ref[...] = x2


def _post(l, final, o_sb, o_sw, o_fx, o_ssm0, o_ssm1, x, mod, out_gain, norm2_gain, w_out, w1, w2,
          final_gain):
    b, s, d = x.shape
    tm = ROW_TILE
    grp = pl.BlockSpec((None, tm, GROUP_WIDTH), lambda bi, i: (bi, i, 0))
    half = pl.BlockSpec((None, tm, LANES), lambda bi, i: (bi, i, 0))
    xspec = pl.BlockSpec((None, tm, d), lambda bi, i: (bi, i, 0))
    lspec = lambda a: pl.BlockSpec((None,) + a.shape[1:], lambda bi, i: (l, 0, 0),
                                   pipeline_mode=pl.Buffered(1))
    return pl.pallas_call(
        functools.partial(_post_kernel, final=final),
        out_shape=jax.ShapeDtypeStruct((b, s, d), F32),
        grid=(b, s // tm),
        in_specs=[grp, grp, grp, half, half, xspec,
                  pl.BlockSpec((None, None, N_ADA, d), lambda bi, i: (l, bi, 0, 0)),
                  lspec(out_gain), lspec(norm2_gain), lspec(w_out), lspec(w1), lspec(w2),
                  pl.BlockSpec(final_gain.shape, lambda bi, i: (0, 0))],
        out_specs=xspec,
        compiler_params=_params("arbitrary", "arbitrary"),
        name="post_mlp",
    )(o_sb, o_sw, o_fx, o_ssm0, o_ssm1, x, mod, out_gain, norm2_gain, w_out, w1, w2, final_gain)


def _swa_head_perm(a, axis):
    shape = a.shape
    a = a.reshape(shape[:axis] + (4, HEAD_DIM) + shape[axis + 1:])
    a = jnp.take(a, jnp.asarray([0, 2, 1, 3]), axis=axis)
    return a.reshape(shape)


def kernel(x, c, w_ada, b_ada, norm1_gain, norm2_gain, w_in, rel_bias, sinks, forget_bias, lam_re, lam_im, log_dt, ssm_b_re, ssm_b_im, ssm_c_re, ssm_c_im, ssm_d, w_glu, b_glu, out_gain, w_out, w_mlp_in, w_mlp_out, final_gain):
    b, s, d = x.shape
    depth = w_in.shape[0]
    g, p = lam_re.shape[1:]
    h = ssm_b_re.shape[-1]
    n_fox = forget_bias.shape[-1]
    scale = 1.0 / math.sqrt(HEAD_DIM)
    assert s % SSM_TILE == 0 and s % (ROW_TILE * INPROJ_SPLIT) == 0
    assert ROW_TILE % SB_TILE == 0 and ROW_TILE % FOX_TILE == 0

    gw, kvw = GROUP_WIDTH, N_KV_SWA * HEAD_DIM
    sizes = (gw, gw, gw, gw, kvw, kvw, gw, gw, gw, n_fox, gw)
    offs = np.concatenate([[0], np.cumsum(sizes)])
    col = lambda k: w_in[:, :, offs[k]:offs[k + 1]]
    w_sbq = (col(0) * scale).astype(BF16)
    w_sbv = col(2).astype(BF16)
    w_swk = col(4).astype(BF16)
    w_fxk = col(7).astype(BF16)
    w_u = col(10).astype(BF16)
    wt = jnp.concatenate([col(1), col(6) * scale, col(8), _swa_head_perm(col(3), 2) * scale, col(5)],
                         axis=2).transpose(0, 2, 1).astype(BF16)
    wf = jnp.pad(col(9), ((0, 0), (0, 0), (0, LANES - n_fox))).astype(BF16)
    forget_b = jnp.pad(forget_bias.astype(F32), ((0, 0), (0, LANES - n_fox)))[:, None, :]
    in_wts = (w_sbq, w_sbv, w_swk, w_fxk, w_u, wt, wf)

    og = out_gain.astype(F32)
    og = jnp.concatenate([og[:, :gw], _swa_head_perm(og[:, gw:2 * gw], 1), og[:, 2 * gw:]], axis=1)
    wo = jnp.concatenate([w_out[:, :gw], _swa_head_perm(w_out[:, gw:2 * gw], 1), w_out[:, 2 * gw:]],
                         axis=1).astype(BF16)
    og = og[:, None, :]
    n1 = norm1_gain.astype(F32)[:, None, :]
    n2 = norm2_gain.astype(F32)[:, None, :]
    w1 = w_mlp_in.astype(BF16)
    w2 = w_mlp_out.astype(BF16)
    fg = final_gain.astype(F32)[None, :]
    bias = _swa_bias(rel_bias)
    sinks_b = jnp.broadcast_to(sinks.astype(F32)[:, :, None], sinks.shape + (LANES,))

    avec, bbr, bbi = _ssm_prep(lam_re.astype(F32), lam_im.astype(F32), log_dt.astype(F32),
                               ssm_b_re.astype(F32), ssm_b_im.astype(F32))
    eye = jnp.eye(g, dtype=F32)
    blockdiag_b = lambda bb: jnp.einsum('lhgp,gk->lghkp', bb.reshape(depth, h, g, p), eye
                                        ).reshape(depth, g * h, g * p)
    bmat = jnp.concatenate([blockdiag_b(bbr), blockdiag_b(bbi)], axis=2).astype(BF16)
    blockdiag_c = lambda cc: jnp.einsum('lghp,gk->lgpkh', cc.astype(F32), eye
                                        ).reshape(depth, g * p, g * h)
    cmat = jnp.concatenate([blockdiag_c(ssm_c_re), -blockdiag_c(ssm_c_im)], axis=1).astype(BF16)
    d_skip = ssm_d.astype(F32).reshape(depth, 1, g * h)
    wglu = w_glu.astype(BF16)
    bglu = b_glu.astype(F32)[:, None, :]

    c_pad = jnp.pad(c.astype(F32), ((0, 8 - b), (0, 0)))
    mod = _ada(c_pad, w_ada.astype(F32), b_ada.astype(F32))[:, :b].reshape(depth, b, N_ADA, d)

    x = x.astype(F32)
    for l in range(depth):
        (sbq, sbv, swk, fxk, u0, u1, sbkt, fxqt, fxvt, swqt, swvt) = _inproj(l, x, mod, n1, in_wts, forget_b)
        o_sb = _stick_breaking(sbq, sbkt, sbv)
        o_sw = _sliding_window(l, swqt, swk, swvt, bias, sinks_b)
        o_fx = _forgetting(fxqt, fxk, fxvt)
        o_ssm0, o_ssm1 = _ssm(l, u0, u1, avec, bmat, cmat, d_skip, wglu, bglu)
        x = _post(l, l == depth - 1, o_sb, o_sw, o_fx, o_ssm0, o_ssm1, x, mod, og, n2, wo, w1, w2, fg)
    return x
```

```python
import functools
import math

import numpy as np
import jax
import jax.numpy as jnp
from jax import lax
from jax.experimental import pallas as pl
from jax.experimental.pallas import tpu as pltpu

F32 = jnp.float32
BF16 = jnp.bfloat16

HEAD_DIM = 64
LANES = 128
GROUP_WIDTH = 256
N_KV_SWA = 2
WINDOW = 128
REL_BUCKETS = 32
REL_MAX_DIST = 128
N_ADA = 6
NORM_EPS = 1e-6
SSM_H = 16
SSM_P = 64
SSM_CHUNK = 16

ROW_TILE = 512
INPROJ_SPLIT = 2
SB_TILE = 256
FOX_TILE = 512
SSM_TILE = 1024
VMEM_LIMIT_BYTES = 56 * 1024 * 1024
EXP_UNDERFLOW = -105.0
FAST_MARGIN = 60.0
FOX_AUG = 8


def _dot(a, b):
    return jnp.dot(a, b, preferred_element_type=F32)


def _dot_nt(a, b):
    return lax.dot_general(a, b, (((1,), (1,)), ((), ())), preferred_element_type=F32)


def _split_bf16(a):
    hi = a.astype(BF16)
    lo = (a - hi.astype(F32)).astype(BF16)
    return hi, lo


def _rms(x):
    return x * lax.rsqrt(jnp.mean(x * x, axis=-1, keepdims=True) + NORM_EPS)


def _params(*sem):
    return pltpu.CompilerParams(dimension_semantics=sem, vmem_limit_bytes=VMEM_LIMIT_BYTES)


def _ada_kernel(c_ref, w_ref, b_ref, o_ref):
    c = c_ref[...]
    ca = c * jax.nn.sigmoid(c)
    a_hi, a_lo = _split_bf16(ca)
    w_hi, w_lo = _split_bf16(w_ref[...])
    o_ref[...] = _dot(a_hi, w_hi) + _dot(a_hi, w_lo) + _dot(a_lo, w_hi) + b_ref[...]


def _ada(c_pad, w_ada, b_ada):
    depth, d, n = w_ada.shape
    tn = 2048
    return pl.pallas_call(
        _ada_kernel,
        out_shape=jax.ShapeDtypeStruct((depth, c_pad.shape[0], n), F32),
        grid=(depth, n // tn),
        in_specs=[
            pl.BlockSpec(c_pad.shape, lambda l, j: (0, 0)),
            pl.BlockSpec((None, d, tn), lambda l, j: (l, 0, j)),
            pl.BlockSpec((None, 1, tn), lambda l, j: (l, 0, j)),
        ],
        out_specs=pl.BlockSpec((None, c_pad.shape[0], tn), lambda l, j: (l, 0, j)),
        compiler_params=_params("arbitrary", "arbitrary"),
        name="ada_mod",
    )(c_pad, w_ada, b_ada.reshape(depth, 1, n))


def _ssm_prep_kernel(lr_ref, li_ref, ldt_ref, br_ref, bi_ref, avec_ref, bbr_ref, bbi_ref):
    lr = lr_ref[...]
    li = li_ref[...]
    dt = jnp.exp(ldt_ref[...])
    mag = jnp.exp(lr * dt)
    ang = li * dt
    ar = mag * jnp.cos(ang)
    ai = mag * jnp.sin(ang)
    den = lr * lr + li * li
    nr = ar - 1.0
    ni = ai
    cr = (nr * lr + ni * li) / den
    ci = (ni * lr - nr * li) / den
    pr, pi_ = ar, ai
    for _ in range(int(math.log2(SSM_CHUNK))):
        pr, pi_ = pr * pr - pi_ * pi_, 2.0 * pr * pi_
    avec_ref[0:1, :] = ar
    avec_ref[1:2, :] = ai
    avec_ref[2:3, :] = pr
    avec_ref[3:4, :] = pi_
    avec_ref[4:8, :] = jnp.zeros((4, ar.shape[1]), F32)
    br = br_ref[...]
    bi = bi_ref[...]
    bbr_ref[...] = cr * br - ci * bi
    bbi_ref[...] = cr * bi + ci * br


def _ssm_prep(lam_re, lam_im, log_dt, b_re, b_im):
    depth, g, p = lam_re.shape
    h = b_re.shape[-1]
    n = g * p
    row = lambda a: a.reshape(depth, 1, n)
    ldt = jnp.broadcast_to(log_dt[:, :, None], (depth, g, p))
    bt = lambda a: a.transpose(0, 3, 1, 2).reshape(depth, h, n)
    vec = pl.BlockSpec((None, 1, n), lambda l: (l, 0, 0))
    mat = pl.BlockSpec((None, h, n), lambda l: (l, 0, 0))
    return pl.pallas_call(
        _ssm_prep_kernel,
        out_shape=(jax.ShapeDtypeStruct((depth, 8, n), F32),
                   jax.ShapeDtypeStruct((depth, h, n), F32),
                   jax.ShapeDtypeStruct((depth, h, n), F32)),
        grid=(depth,),
        in_specs=[vec, vec, vec, mat, mat],
        out_specs=(pl.BlockSpec((None, 8, n), lambda l: (l, 0, 0)), mat, mat),
        compiler_params=_params("arbitrary"),
        name="ssm_prep",
    )(row(lam_re), row(lam_im), row(ldt), bt(b_re), bt(b_im))


def _inproj_kernel(x_ref, mod_ref, gain_ref, w_sbq, w_sbv, w_swk, w_fxk, w_u,
                   wt_ref, wf_ref, fb_ref, sel_ref,
                   sbq_o, sbv_o, swk_o, fxk_o, u0_o, u1_o, sbkt_o, fxqt_o, fxvt_o, swqt_o, swvt_o,
                   carry_ref):
    hr = x_ref.shape[0] // INPROJ_SPLIT

    @pl.when(pl.program_id(1) == 0)
    def _():
        carry_ref[...] = jnp.zeros_like(carry_ref)

    mod = mod_ref[...]
    sh1 = mod[0:1]
    sc1 = mod[1:2]
    gain = gain_ref[...]
    hs = [(_rms(x_ref[r * hr:(r + 1) * hr, :]) * gain * (1.0 + sc1) + sh1).astype(BF16)
          for r in range(INPROJ_SPLIT)]
    step = lax.broadcasted_iota(jnp.int32, (hr, LANES), 0)
    carry = carry_ref[0:1, :]
    for r, h in enumerate(hs):
        rows = slice(r * hr, (r + 1) * hr)
        f = _dot(h, wf_ref[...]) + fb_ref[...]
        logf = jnp.minimum(f, 0.0) - jnp.log1p(jnp.exp(-jnp.abs(f)))
        cum = logf
        k = 1
        while k < hr:
            cum = cum + jnp.where(step >= k, pltpu.roll(cum, k, axis=0), 0.0)
            k *= 2
        cum = cum + carry
        carry = cum[hr - 1:hr, :]
        negf = -cum
        hi = negf.astype(BF16)
        r1 = negf - hi.astype(F32)
        mid = r1.astype(BF16)
        lo3 = (r1 - mid.astype(F32)).astype(BF16)
        aug = _dot(jnp.concatenate([hi, mid, lo3], axis=1), sel_ref[...]).astype(BF16)

        k_fx = _dot(h, w_fxk[...])
        n_pairs = fxk_o.shape[1] // (2 * LANES)
        for p in range(n_pairs):
            base = 2 * p * LANES
            fxk_o[rows, base:base + LANES] = k_fx[:, p * LANES:(p + 1) * LANES].astype(BF16)
            fxk_o[rows, base + LANES:base + 2 * LANES] = aug[:, p * LANES:(p + 1) * LANES]

        sbq_o[rows, :] = _dot(h, w_sbq[...]).astype(BF16)
        sbv_o[rows, :] = _dot(h, w_sbv[...]).astype(BF16)
        swk_o[rows, :] = _dot(h, w_swk[...]).astype(BF16)
        u = _dot(h, w_u[...])
        u0_o[rows, :] = u[:, :LANES]
        u1_o[rows, :] = u[:, LANES:]
        tt = _dot_nt(wt_ref[...], h)
        row0 = 0
        for o in (sbkt_o, fxqt_o, fxvt_o, swqt_o, swvt_o):
            n_sub, width, tile = o.shape
            per_part = n_sub // INPROJ_SPLIT
            for c in range(per_part):
                o[r * per_part + c] = tt[row0:row0 + width, c * tile:(c + 1) * tile].astype(BF16)
            row0 += width
    carry_ref[...] = jnp.broadcast_to(carry, carry_ref.shape)


def _fox_aug_select():
    n_pairs = GROUP_WIDTH // LANES
    sel = np.zeros((3 * LANES, n_pairs * LANES), np.float32)
    for p in range(n_pairs):
        for j in range(2):
            for r in range(3):
                sel[LANES * r + 2 * p + j, LANES * p + FOX_AUG * j + r] = 1.0
    return jnp.asarray(sel, BF16)


def _inproj(l, x, mod, gain, wts, forget_b):
    b, s, d = x.shape
    ts = ROW_TILE * INPROJ_SPLIT
    nk = s // ts
    (w_sbq, w_sbv, w_swk, w_fxk, w_u, wt, wf) = wts
    sel = _fox_aug_select()
    row = lambda width: pl.BlockSpec((None, ts, width), lambda bi, i: (bi, i, 0))
    wspec = lambda w: pl.BlockSpec((None,) + w.shape[1:], lambda bi, i: (l, 0, 0))
    tspec = lambda width, tile: pl.BlockSpec((None, ts // tile, width, tile), lambda bi, i: (bi, i, 0, 0))
    kv_w = N_KV_SWA * HEAD_DIM
    out_shape = (
        jax.ShapeDtypeStruct((b, s, GROUP_WIDTH), BF16),
        jax.ShapeDtypeStruct((b, s, GROUP_WIDTH), BF16),
        jax.ShapeDtypeStruct((b, s, kv_w), BF16),
        jax.ShapeDtypeStruct((b, s, 2 * GROUP_WIDTH), BF16),
        jax.ShapeDtypeStruct((b, s, LANES), F32),
        jax.ShapeDtypeStruct((b, s, LANES), F32),
        jax.ShapeDtypeStruct((b, s // SB_TILE, GROUP_WIDTH, SB_TILE), BF16),
        jax.ShapeDtypeStruct((b, s // FOX_TILE, GROUP_WIDTH, FOX_TILE), BF16),
        jax.ShapeDtypeStruct((b, s // FOX_TILE, GROUP_WIDTH, FOX_TILE), BF16),
        jax.ShapeDtypeStruct((b, s // ROW_TILE, GROUP_WIDTH, ROW_TILE), BF16),
        jax.ShapeDtypeStruct((b, s // ROW_TILE, kv_w, ROW_TILE), BF16),
    )
    out_specs = (row(GROUP_WIDTH), row(GROUP_WIDTH), row(kv_w), row(2 * GROUP_WIDTH),
                 row(LANES), row(LANES), tspec(GROUP_WIDTH, SB_TILE), tspec(GROUP_WIDTH, FOX_TILE),
                 tspec(GROUP_WIDTH, FOX_TILE), tspec(GROUP_WIDTH, ROW_TILE), tspec(kv_w, ROW_TILE))
    return pl.pallas_call(
        _inproj_kernel,
        out_shape=out_shape,
        grid=(b, nk),
        in_specs=[
            pl.BlockSpec((None, ts, d), lambda bi, i: (bi, i, 0)),
            pl.BlockSpec((None, None, N_ADA, d), lambda bi, i: (l, bi, 0, 0)),
            pl.BlockSpec((None, 1, d), lambda bi, i: (l, 0, 0)),
            wspec(w_sbq), wspec(w_sbv), wspec(w_swk), wspec(w_fxk), wspec(w_u),
            wspec(wt), wspec(wf),
            pl.BlockSpec((None, 1, LANES), lambda bi, i: (l, 0, 0)),
            pl.BlockSpec(sel.shape, lambda bi, i: (0, 0)),
        ],
        out_specs=out_specs,
        scratch_shapes=[pltpu.VMEM((8, LANES), F32)],
        compiler_params=_params("arbitrary", "arbitrary"),
        name="in_proj",
    )(x, mod, gain, w_sbq, w_sbv, w_swk, w_fxk, w_u, wt, wf, forget_b, sel)


def _pair_split(q):
    lane = lax.broadcasted_iota(jnp.int32, q.shape, 1)
    left = lane < HEAD_DIM
    zero = jnp.zeros_like(q)
    return left, (jnp.where(left, q, zero), jnp.where(left, zero, q))


def _sb_kernel(q_ref, kt_ref, v_ref, suf_ref, o_ref):
    t = q_ref.shape[0]
    n_pairs = q_ref.shape[1] // LANES
    qi = pl.program_id(1)
    split = [_pair_split(q_ref[:, p * LANES:(p + 1) * LANES]) for p in range(n_pairs)]
    left = split[0][0]
    row = lax.broadcasted_iota(jnp.int32, (t, t), 0)
    col = lax.broadcasted_iota(jnp.int32, (t, t), 1)
    strict = col < row
    suf = suf_ref[...]

    def tile(ki, carry):
        run, acc = carry
        rows = pl.ds(pl.multiple_of(ki * t, t), t)
        heads = [(p, j) for p in range(n_pairs) for j in range(2)]
        z = [_dot(split[p][1][j], kt_ref[ki, p * LANES:(p + 1) * LANES, :]) for p, j in heads]
        lsz, lk = [], []
        for zh in z:
            sp = jnp.log(1.0 + jnp.exp(-jnp.abs(zh)))
            lszh = jnp.minimum(zh, 0.0) - sp
            lkh = lszh - zh
            lsz.append(lszh)
            lk.append(lkh)
        within = [_dot(lkh.astype(BF16), suf) for lkh in lk]
        w = [jnp.exp(lsz[h] + within[h] + run[h]).astype(BF16) for h in range(len(heads))]
        pv = [_dot(w[h], v_ref[rows, p * LANES:(p + 1) * LANES]) for h, (p, j) in enumerate(heads)]
        new_run = tuple(run[h] + jnp.sum(lk[h], axis=-1, keepdims=True) for h in range(len(heads)))
        new_acc = tuple(acc[p] + jnp.where(left, pv[2 * p], pv[2 * p + 1]) for p in range(n_pairs))
        return new_run, new_acc

    def alive(run):
        top = run[0]
        for r in run[1:]:
            top = jnp.maximum(top, r)
        return jnp.max(top) > EXP_UNDERFLOW

    def first_two():
        prev = jnp.maximum(qi - 1, 0)
        has_prev = qi >= 1
        heads = [(p, j) for p in range(n_pairs) for j in range(2)]
        nh = len(heads)
        tiles = (qi, prev)
        z = [[_dot(split[p][1][j], kt_ref[ki, p * LANES:(p + 1) * LANES, :]) for p, j in heads]
             for ki in tiles]
        lsz, lk = [[], []], [[], []]
        for a in range(2):
            for zh in z[a]:
                sp = jnp.log(1.0 + jnp.exp(-jnp.abs(zh)))
                lszh = jnp.minimum(zh, 0.0) - sp
                lkh = lszh - zh
                lsz[a].append(lszh)
                lk[a].append(jnp.where(strict, lkh, 0.0) if a == 0 else jnp.where(has_prev, lkh, 0.0))
        within = [[_dot(lkh.astype(BF16), suf) for lkh in lk[a]] for a in range(2)]
        run_mid = [jnp.sum(lk[0][h], axis=-1, keepdims=True) for h in range(nh)]
        w0 = [jnp.where(strict, jnp.exp(lsz[0][h] + within[0][h]), 0.0).astype(BF16) for h in range(nh)]
        w1 = [jnp.where(has_prev, jnp.exp(lsz[1][h] + within[1][h] + run_mid[h]), 0.0).astype(BF16)
              for h in range(nh)]
        rows0 = pl.ds(pl.multiple_of(qi * t, t), t)
        rows1 = pl.ds(pl.multiple_of(prev * t, t), t)
        pv = [_dot(w0[h], v_ref[rows0, p * LANES:(p + 1) * LANES])
              + _dot(w1[h], v_ref[rows1, p * LANES:(p + 1) * LANES]) for h, (p, j) in enumerate(heads)]
        run = tuple(run_mid[h] + jnp.sum(lk[1][h], axis=-1, keepdims=True) for h in range(nh))
        acc = tuple(jnp.where(left, pv[2 * p], pv[2 * p + 1]) for p in range(n_pairs))
        return run, acc

    state = first_two()

    def body(c):
        i, _, st = c
        st = tile(qi - 1 - i, st)
        return i + 1, alive(st[0]), st

    _, _, state = lax.while_loop(lambda c: jnp.logical_and(c[0] < qi, c[1]), body,
                                 (jnp.int32(1), alive(state[0]), state))
    for p in range(n_pairs):
        o_ref[:, p * LANES:(p + 1) * LANES] = state[1][p]


def _fox_kernel(qt_ref, k_ref, vt_ref, o_ref, bound_ref):
    t = qt_ref.shape[1]
    nk = vt_ref.shape[0]
    qi = pl.program_id(2)
    qt = qt_ref[...]
    sub = lax.broadcasted_iota(jnp.int32, (LANES, t), 0)
    top = sub < HEAD_DIM
    zero = jnp.zeros_like(qt)
    q2 = qt.astype(F32) * qt.astype(F32)
    qa, qn = [], []
    for j in range(2):
        mine = top if j == 0 else jnp.logical_not(top)
        ones = jnp.where(jnp.logical_and(sub >= FOX_AUG * j, sub < FOX_AUG * j + 3), 1.0, 0.0)
        qa.append(jnp.concatenate([jnp.where(mine, qt, zero), ones.astype(BF16)], axis=0))
        qn.append(jnp.sqrt(jnp.sum(jnp.where(mine, q2, 0.0), axis=0, keepdims=True)))
    key = lax.broadcasted_iota(jnp.int32, (t, t), 0)
    qry = lax.broadcasted_iota(jnp.int32, (t, t), 1)
    causal = key <= qry
    lane = lax.broadcasted_iota(jnp.int32, (t, LANES), 1)

    @pl.when(qi == 0)
    def _():
        def scan(ki, c):
            kk = k_ref[pl.ds(pl.multiple_of(ki * t, t), t), :].astype(F32)
            k2 = kk[:, :LANES] * kk[:, :LANES]
            aug = kk[:, LANES:]
            out_kn, out_nf = [], []
            for j in range(2):
                mine = (lane < HEAD_DIM) if j == 0 else (lane >= HEAD_DIM)
                n2 = jnp.sum(jnp.where(mine, k2, 0.0), axis=1, keepdims=True)
                here = jnp.logical_and(lane >= FOX_AUG * j, lane < FOX_AUG * (j + 1))
                negf = jnp.sum(jnp.where(here, aug, 0.0), axis=1, keepdims=True)
                kn = jnp.maximum(c[j], jnp.sqrt(jnp.max(n2)))
                nf = jnp.maximum(c[2 + j], jnp.max(negf))
                bound_ref[j, ki] = kn
                bound_ref[2 + j, ki] = nf
                out_kn.append(kn)
                out_nf.append(nf)
            return tuple(out_kn + out_nf)

        lax.fori_loop(0, nk, scan, (jnp.float32(0.0),) * 2 + (jnp.float32(-jnp.inf),) * 2)

    def tile(ki, carry, mode):
        m, l, acc = carry
        kk = k_ref[pl.ds(pl.multiple_of(ki * t, t), t), :]
        vt = vt_ref[ki]
        s = [_dot(kk, qa[j]) for j in range(2)]
        if mode == "diag":
            s = [jnp.where(causal, sj, -jnp.inf) for sj in s]
        if mode == "fast":
            new_m = m
        else:
            new_m = tuple(jnp.maximum(m[j], jnp.max(s[j], axis=0, keepdims=True)) for j in range(2))
        e = [jnp.exp(s[j] - new_m[j]) for j in range(2)]
        pv = [_dot(vt, e[j].astype(BF16)) for j in range(2)]
        colsum = [jnp.sum(e[j], axis=0, keepdims=True) for j in range(2)]
        if mode == "fast":
            new_l = tuple(l[j] + colsum[j] for j in range(2))
        else:
            corr = [jnp.exp(m[j] - new_m[j]) for j in range(2)]
            new_l = tuple(l[j] * corr[j] + colsum[j] for j in range(2))
            acc = acc * jnp.where(top, corr[0], corr[1])
        return new_m, new_l, acc + jnp.where(top, pv[0], pv[1])

    def row_gap(m, ki):
        kic = jnp.maximum(ki, 0)
        return tuple(jnp.max(qn[j] * bound_ref[j, kic] - m[j]) for j in range(2))

    def gap(rg, ki):
        kic = jnp.maximum(ki, 0)
        return jnp.maximum(rg[0] + bound_ref[2, kic], rg[1] + bound_ref[3, kic])

    ninf = jnp.full((1, t), -jnp.inf, F32)
    zrow = jnp.zeros((1, t), F32)
    state = tile(qi, ((ninf, ninf), (zrow, zrow), jnp.zeros((LANES, t), F32)), "diag")

    def general(c):
        i, _, st = c
        ki = qi - 1 - i
        st = tile(ki, st, "general")
        return i + 1, gap(row_gap(st[0], ki - 1), ki - 1), st

    i0, _, (m, l, acc) = lax.while_loop(
        lambda c: jnp.logical_and(c[0] < qi, c[1] >= FAST_MARGIN), general,
        (jnp.int32(0), gap(row_gap(state[0], qi - 1), qi - 1), state))
    rg = row_gap(m, qi - 1 - i0)

    def run_fast(n, i_start, l, acc):
        def trip(c):
            i, _, l, acc = c
            ki = qi - 1 - i
            kk = [k_ref[pl.ds(pl.multiple_of((ki - d) * t, t), t), :] for d in range(n)]
            s = [[_dot(kk[d], qa[j]) for j in range(2)] for d in range(n)]
            e = [[jnp.exp(s[d][j] - m[j]) for j in range(2)] for d in range(n)]
            pv = [sum(_dot(vt_ref[ki - d, :, :], e[d][j].astype(BF16)) for d in range(n)) for j in range(2)]
            l = tuple(l[j] + sum(jnp.sum(e[d][j], axis=0, keepdims=True) for d in range(n)) for j in range(2))
            return i + n, gap(rg, ki - 2 * n + 1), l, acc + jnp.where(top, pv[0], pv[1])

        i_end, _, l, acc = lax.while_loop(
            lambda c: jnp.logical_and(c[0] + n - 1 < qi, c[1] > EXP_UNDERFLOW), trip,
            (i_start, gap(rg, qi - n - i_start), l, acc))
        return i_end, l, acc

    i1, l, acc = run_fast(3, i0, l, acc)
    i1, l, acc = run_fast(2, i1, l, acc)

    def fast(c):
        i, _, l, acc = c
        ki = qi - 1 - i
        _, l, acc = tile(ki, (m, l, acc), "fast")
        return i + 1, gap(rg, ki - 1), l, acc

    _, _, l, acc = lax.while_loop(
        lambda c: jnp.logical_and(c[0] < qi, c[1] > EXP_UNDERFLOW), fast,
        (i1, gap(rg, qi - 1 - i1), l, acc))
    o_ref[...] = (acc / jnp.where(top, l[0], l[1])).T


def _stick_breaking(q, kt, v):
    b, s, width = q.shape
    t = SB_TILE
    nk = s // t
    suffix = jnp.asarray(np.tril(np.ones((t, t), np.float32), -1), BF16)
    once = pl.Buffered(1)
    return pl.pallas_call(
        _sb_kernel,
        out_shape=jax.ShapeDtypeStruct((b, s, width), F32),
        grid=(b, nk),
        in_specs=[
            pl.BlockSpec((None, t, width), lambda bi, i: (bi, i, 0)),
            pl.BlockSpec((None, nk, width, t), lambda bi, i: (bi, 0, 0, 0), pipeline_mode=once),
            pl.BlockSpec((None, s, width), lambda bi, i: (bi, 0, 0), pipeline_mode=once),
            pl.BlockSpec((t, t), lambda bi, i: (0, 0), pipeline_mode=once),
        ],
        out_specs=pl.BlockSpec((None, t, width), lambda bi, i: (bi, i, 0)),
        compiler_params=_params("arbitrary", "arbitrary"),
        name="stick_breaking",
    )(q, kt, v, suffix)


def _forgetting(qt, k, vt):
    b, nq, width, t = qt.shape
    s = nq * t
    return pl.pallas_call(
        _fox_kernel,
        out_shape=jax.ShapeDtypeStruct((b, s, width), F32),
        grid=(b, width // LANES, nq),
        in_specs=[
            pl.BlockSpec((None, None, LANES, t), lambda bi, p, i: (bi, i, p, 0)),
            pl.BlockSpec((None, s, 2 * LANES), lambda bi, p, i: (bi, 0, p)),
            pl.BlockSpec((None, nq, LANES, t), lambda bi, p, i: (bi, 0, p, 0)),
        ],
        out_specs=pl.BlockSpec((None, t, LANES), lambda bi, p, i: (bi, i, p)),
        scratch_shapes=[pltpu.SMEM((4, nq), F32)],
        compiler_params=_params("arbitrary", "arbitrary", "arbitrary"),
        name="forgetting",
    )(qt, k, vt)


def _swa_kernel(qt_ref, kc_ref, kp_ref, vtc_ref, vtp_ref, bias_ref, sink_ref, o_ref):
    r = kc_ref.shape[0]
    first = pl.program_id(1) == 0
    sub = lax.broadcasted_iota(jnp.int32, (LANES, WINDOW), 0)
    top = sub < HEAD_DIM
    key = lax.broadcasted_iota(jnp.int32, (2 * WINDOW, WINDOW), 0)
    n_win = r // WINDOW
    ks, vts = [], []
    for w in range(n_win):
        lo = w * WINDOW
        if w == 0:
            ks.append(jnp.concatenate([kp_ref[...], kc_ref[:WINDOW, :]], axis=0))
            vts.append(jnp.concatenate([vtp_ref[0, :, r - WINDOW:], vtc_ref[0, :, :WINDOW]], axis=1))
        else:
            ks.append(kc_ref[lo - WINDOW:lo + WINDOW, :])
            vts.append(vtc_ref[0, :, lo - WINDOW:lo + WINDOW])
    units = [(w, g, kv) for w in range(n_win) for g in range(2) for kv in range(N_KV_SWA)]
    z = []
    for w, g, kv in units:
        qg = qt_ref[0, g * LANES:(g + 1) * LANES, w * WINDOW:(w + 1) * WINDOW]
        zero = jnp.zeros_like(qg)
        qm = jnp.where(top, qg, zero) if kv == 0 else jnp.where(top, zero, qg)
        zu = _dot(ks[w], qm) + bias_ref[2 * kv + g]
        if w == 0:
            zu = jnp.where(jnp.logical_and(first, key < WINDOW), -jnp.inf, zu)
        z.append(zu)
    p = []
    for (w, g, kv), zu in zip(units, z):
        head = 2 * kv + g
        sink = sink_ref[head:head + 1, 0:1]
        m = jnp.maximum(jnp.max(zu, axis=0, keepdims=True), sink)
        e = jnp.exp(zu - m)
        den = jnp.sum(e, axis=0, keepdims=True) + jnp.exp(sink - m)
        p.append((e * (1.0 / den)).astype(BF16))
    pv = {u: _dot(vts[u[0]], pu) for u, pu in zip(units, p)}
    for w in range(n_win):
        for g in range(2):
            o_ref[w * WINDOW:(w + 1) * WINDOW, g * LANES:(g + 1) * LANES] = jnp.where(
                top, pv[(w, g, 0)], pv[(w, g, 1)]).T


def _sliding_window(l, qt, k, vt, bias_t, sinks):
    b, s, kv_w = k.shape
    r = ROW_TILE
    wpt = r // WINDOW
    return pl.pallas_call(
        _swa_kernel,
        out_shape=jax.ShapeDtypeStruct((b, s, GROUP_WIDTH), F32),
        grid=(b, s // r),
        in_specs=[
            pl.BlockSpec((None, 1, GROUP_WIDTH, r), lambda bi, i: (bi, i, 0, 0)),
            pl.BlockSpec((None, r, kv_w), lambda bi, i: (bi, i, 0)),
            pl.BlockSpec((None, WINDOW, kv_w), lambda bi, i: (bi, jnp.maximum(i * wpt - 1, 0), 0)),
            pl.BlockSpec((None, 1, kv_w, r), lambda bi, i: (bi, i, 0, 0)),
            pl.BlockSpec((None, 1, kv_w, r), lambda bi, i: (bi, jnp.maximum(i - 1, 0), 0, 0)),
            pl.BlockSpec(bias_t.shape, lambda bi, i: (0, 0, 0)),
            pl.BlockSpec((None,) + sinks.shape[1:], lambda bi, i: (l, 0, 0)),
        ],
        out_specs=pl.BlockSpec((None, r, GROUP_WIDTH), lambda bi, i: (bi, i, 0)),
        compiler_params=_params("arbitrary", "arbitrary"),
        name="sliding_window",
    )(qt, k, k, vt, vt, bias_t, sinks)


def _swa_bias(rel_bias):
    i = np.arange(WINDOW)[:, None]
    j = np.arange(2 * WINDOW)[None, :]
    dist = WINDOW + i - j
    in_window = (dist >= 0) & (dist < WINDOW)
    d = np.clip(dist, 0, None)
    max_exact = REL_BUCKETS // 2
    safe = np.maximum(d, 1).astype(np.float32)
    large = max_exact + (np.log(safe / max_exact) / math.log(REL_MAX_DIST / max_exact)
                         * (REL_BUCKETS - max_exact)).astype(np.int32)
    large = np.minimum(large, REL_BUCKETS - 1)
    bucket = np.where(d < max_exact, d, large).astype(np.int32)
    onehot = jnp.asarray(np.eye(REL_BUCKETS, dtype=np.float32)[bucket.reshape(-1)])
    bias = jnp.dot(onehot, rel_bias.astype(F32), precision=lax.Precision.HIGHEST)
    bias = bias.T.reshape(-1, WINDOW, 2 * WINDOW)
    return jnp.where(jnp.asarray(in_window)[None], bias, -jnp.inf).transpose(0, 2, 1)


def _ssm_kernel(u0_ref, u1_ref, avec_ref, bmat_ref, cmat_ref, d_ref, wglu_ref, bglu_ref,
                o0_ref, o1_ref, up_ref, xr_ref, xi_ref, er_ref, ei_ref, sr_ref, si_ref, carry_ref):
    n_slab, ts, _ = xr_ref.shape
    n = ts // SSM_CHUNK
    np_ = n_slab * LANES

    @pl.when(pl.program_id(1) == 0)
    def _():
        carry_ref[...] = jnp.zeros_like(carry_ref)

    def rows(i):
        return pl.ds(i * n, n)

    for k, u_ref in enumerate((u0_ref, u1_ref)):
        for i in range(SSM_CHUNK):
            up_ref[rows(i), k * LANES:(k + 1) * LANES] = u_ref[pl.ds(i, n, stride=SSM_CHUNK), :]
    u = up_ref[...]
    bu = _dot(u.astype(BF16), bmat_ref[...])
    for j in range(n_slab):
        xr_ref[j] = bu[:, j * LANES:(j + 1) * LANES]
        xi_ref[j] = bu[:, np_ + j * LANES:np_ + (j + 1) * LANES]

    def coef(k, j):
        return avec_ref[k:k + 1, j * LANES:(j + 1) * LANES]

    for j in range(n_slab):
        ar, ai = coef(0, j), coef(1, j)
        pr, pi_ = xr_ref[j, rows(0), :], xi_ref[j, rows(0), :]
        for i in range(1, SSM_CHUNK):
            nr = ar * pr - ai * pi_ + xr_ref[j, rows(i), :]
            ni = ar * pi_ + ai * pr + xi_ref[j, rows(i), :]
            xr_ref[j, rows(i), :] = nr
            xi_ref[j, rows(i), :] = ni
            pr, pi_ = nr, ni
        er_ref[j] = pr
        ei_ref[j] = pi_

    a16 = [(coef(2, j), coef(3, j)) for j in range(n_slab)]

    def chunk(c, s):
        out = []
        for j in range(n_slab):
            sr, si = s[j]
            sr_ref[j, pl.ds(c, 1), :] = sr
            si_ref[j, pl.ds(c, 1), :] = si
            er = er_ref[j, pl.ds(c, 1), :]
            ei = ei_ref[j, pl.ds(c, 1), :]
            a16r, a16i = a16[j]
            out.append((a16r * sr - a16i * si + er, a16r * si + a16i * sr + ei))
        return tuple(out)

    init = tuple((carry_ref[0:1, j * LANES:(j + 1) * LANES], carry_ref[1:2, j * LANES:(j + 1) * LANES])
                 for j in range(n_slab))
    fin = lax.fori_loop(0, n, chunk, init)
    for j in range(n_slab):
        carry_ref[0:1, j * LANES:(j + 1) * LANES] = fin[j][0]
        carry_ref[1:2, j * LANES:(j + 1) * LANES] = fin[j][1]

    for j in range(n_slab):
        ar, ai = coef(0, j), coef(1, j)
        pr, pi_ = sr_ref[j], si_ref[j]
        for i in range(SSM_CHUNK):
            pr, pi_ = ar * pr - ai * pi_, ar * pi_ + ai * pr
            xr_ref[j, rows(i), :] = xr_ref[j, rows(i), :] + pr
            xi_ref[j, rows(i), :] = xi_ref[j, rows(i), :] + pi_

    xr = jnp.concatenate([xr_ref[j].astype(BF16) for j in range(n_slab)], axis=1)
    xi = jnp.concatenate([xi_ref[j].astype(BF16) for j in range(n_slab)], axis=1)
    y = _dot(xr, cmat_ref[0:np_, :]) + _dot(xi, cmat_ref[np_:, :]) + d_ref[...] * u
    y = 0.5 * y * (1.0 + jnp.tanh(math.sqrt(2.0 / math.pi) * (y + 0.044715 * (y * y * y))))
    gate = jax.nn.sigmoid(_dot(y.astype(BF16), wglu_ref[...]) + bglu_ref[...])
    out = y * gate
    for k, o_ref in enumerate((o0_ref, o1_ref)):
        for i in range(SSM_CHUNK):
            o_ref[pl.ds(i, n, stride=SSM_CHUNK), :] = out[i * n:(i + 1) * n, k * LANES:(k + 1) * LANES]


def _ssm(l, u0, u1, avec, bmat, cmat, d_skip, w_glu, b_glu):
    b, s, hw = u0.shape
    ts = min(SSM_TILE, s)
    np_ = avec.shape[-1]
    n = ts // SSM_CHUNK
    n_slab = np_ // LANES
    lspec = lambda a: pl.BlockSpec((None,) + a.shape[1:], lambda bi, i: (l, 0, 0))
    half = pl.BlockSpec((None, ts, hw), lambda bi, i: (bi, i, 0))
    return pl.pallas_call(
        _ssm_kernel,
        out_shape=(jax.ShapeDtypeStruct((b, s, hw), F32), jax.ShapeDtypeStruct((b, s, hw), F32)),
        grid=(b, s // ts),
        in_specs=[half, half,
                  lspec(avec), lspec(bmat), lspec(cmat), lspec(d_skip), lspec(w_glu), lspec(b_glu)],
        out_specs=(half, half),
        scratch_shapes=[pltpu.VMEM((ts, 2 * hw), F32),
                        pltpu.VMEM((n_slab, ts, LANES), F32), pltpu.VMEM((n_slab, ts, LANES), F32),
                        pltpu.VMEM((n_slab, n, LANES), F32), pltpu.VMEM((n_slab, n, LANES), F32),
                        pltpu.VMEM((n_slab, n, LANES), F32), pltpu.VMEM((n_slab, n, LANES), F32),
                        pltpu.VMEM((8, np_), F32)],
        compiler_params=_params("arbitrary", "arbitrary"),
        name="s5_ssm",
    )(u0, u1, avec, bmat, cmat, d_skip, w_glu, b_glu)


def _post_kernel(osb_ref, osw_ref, ofx_ref, ossm0_ref, ossm1_ref, x_ref, mod_ref, og_ref, n2_ref,
                 wout_ref, w1_ref, w2_ref, fg_ref, o_ref, *, final):
    mod = mod_ref[...]
    g1, sh2, sc2, g2 = mod[2:3], mod[3:4], mod[4:5], mod[5:6]
    ssm = jnp.concatenate([ossm0_ref[...], ossm1_ref[...]], axis=1)
    mo = None
    for k, o in enumerate((osb_ref[...], osw_ref[...], ofx_ref[...], ssm)):
        lo = k * GROUP_WIDTH
        y = (_rms(o) * og_ref[:, lo:lo + GROUP_WIDTH]).astype(BF16)
        part = _dot(y, wout_ref[lo:lo + GROUP_WIDTH, :])
        mo = part if mo is None else mo + part
    x1 = x_ref[...] + g1 * mo
    h = (_rms(x1) * n2_ref[...] * (1.0 + sc2) + sh2).astype(BF16)
    d_ff = w1_ref.shape[1]
    fc = 1024
    acc = None
    for c in range(d_ff // fc):
        a = _dot(h, w1_ref[:, c * fc:(c + 1) * fc])
        a = jnp.square(jnp.maximum(a, 0.0)).astype(BF16)
        part = _dot(a, w2_ref[c * fc:(c + 1) * fc, :])
        acc = part if acc is None else acc + part
    x2 = x1 + g2 * acc
    if final:
        x2 = _rms(x2) * fg_ref[...]
    o_ref[...] = x2


def _post(l, final, o_sb, o_sw, o_fx, o_ssm0, o_ssm1, x, mod, out_gain, norm2_gain, w_out, w1, w2,
          final_gain):
    b, s, d = x.shape
    tm = ROW_TILE
    grp = pl.BlockSpec((None, tm, GROUP_WIDTH), lambda bi, i: (bi, i, 0))
    half = pl.BlockSpec((None, tm, LANES), lambda bi, i: (bi, i, 0))
    xspec = pl.BlockSpec((None, tm, d), lambda bi, i: (bi, i, 0))
    lspec = lambda a: pl.BlockSpec((None,) + a.shape[1:], lambda bi, i: (l, 0, 0),
                                   pipeline_mode=pl.Buffered(1))
    return pl.pallas_call(
        functools.partial(_post_kernel, final=final),
        out_shape=jax.ShapeDtypeStruct((b, s, d), F32),
        grid=(b, s // tm),
        in_specs=[grp, grp, grp, half, half, xspec,
                  pl.BlockSpec((None, None, N_ADA, d), lambda bi, i: (l, bi, 0, 0)),
                  lspec(out_gain), lspec(norm2_gain), lspec(w_out), lspec(w1), lspec(w2),
                  pl.BlockSpec(final_gain.shape, lambda bi, i: (0, 0))],
        out_specs=xspec,
        compiler_params=_params("arbitrary", "arbitrary"),
        name="post_mlp",
    )(o_sb, o_sw, o_fx, o_ssm0, o_ssm1, x, mod, out_gain, norm2_gain, w_out, w1, w2, final_gain)


def _swa_head_perm(a, axis):
    shape = a.shape
    a = a.reshape(shape[:axis] + (4, HEAD_DIM) + shape[axis + 1:])
    a = jnp.take(a, jnp.asarray([0, 2, 1, 3]), axis=axis)
    return a.reshape(shape)


def kernel(x, c, w_ada, b_ada, norm1_gain, norm2_gain, w_in, rel_bias, sinks, forget_bias, lam_re, lam_im, log_dt, ssm_b_re, ssm_b_im, ssm_c_re, ssm_c_im, ssm_d, w_glu, b_glu, out_gain, w_out, w_mlp_in, w_mlp_out, final_gain):
    b, s, d = x.shape
    depth = w_in.shape[0]
    g, p = lam_re.shape[1:]
    h = ssm_b_re.shape[-1]
    n_fox = forget_bias.shape[-1]
    scale = 1.0 / math.sqrt(HEAD_DIM)
    assert s % SSM_TILE == 0 and s % (ROW_TILE * INPROJ_SPLIT) == 0
    assert ROW_TILE % SB_TILE == 0 and ROW_TILE % FOX_TILE == 0

    gw, kvw = GROUP_WIDTH, N_KV_SWA * HEAD_DIM
    sizes = (gw, gw, gw, gw, kvw, kvw, gw, gw, gw, n_fox, gw)
    offs = np.concatenate([[0], np.cumsum(sizes)])
    col = lambda k: w_in[:, :, offs[k]:offs[k + 1]]
    w_sbq = (col(0) * scale).astype(BF16)
    w_sbv = col(2).astype(BF16)
    w_swk = col(4).astype(BF16)
    w_fxk = col(7).astype(BF16)
    w_u = col(10).astype(BF16)
    wt = jnp.concatenate([col(1), col(6) * scale, col(8), _swa_head_perm(col(3), 2) * scale, col(5)],
                         axis=2).transpose(0, 2, 1).astype(BF16)
    wf = jnp.pad(col(9), ((0, 0), (0, 0), (0, LANES - n_fox))).astype(BF16)
    forget_b = jnp.pad(forget_bias.astype(F32), ((0, 0), (0, LANES - n_fox)))[:, None, :]
    in_wts = (w_sbq, w_sbv, w_swk, w_fxk, w_u, wt, wf)

    og = out_gain.astype(F32)
    og = jnp.concatenate([og[:, :gw], _swa_head_perm(og[:, gw:2 * gw], 1), og[:, 2 * gw:]], axis=1)
    wo = jnp.concatenate([w_out[:, :gw], _swa_head_perm(w_out[:, gw:2 * gw], 1), w_out[:, 2 * gw:]],
                         axis=1).astype(BF16)
    og = og[:, None, :]
    n1 = norm1_gain.astype(F32)[:, None, :]
    n2 = norm2_gain.astype(F32)[:, None, :]
    w1 = w_mlp_in.astype(BF16)
    w2 = w_mlp_out.astype(BF16)
    fg = final_gain.astype(F32)[None, :]
    bias = _swa_bias(rel_bias)
    sinks_b = jnp.broadcast_to(sinks.astype(F32)[:, :, None], sinks.shape + (LANES,))

    avec, bbr, bbi = _ssm_prep(lam_re.astype(F32), lam_im.astype(F32), log_dt.astype(F32),
                               ssm_b_re.astype(F32), ssm_b_im.astype(F32))
    eye = jnp.eye(g, dtype=F32)
    blockdiag_b = lambda bb: jnp.einsum('lhgp,gk->lghkp', bb.reshape(depth, h, g, p), eye
                                        ).reshape(depth, g * h, g * p)
    bmat = jnp.concatenate([blockdiag_b(bbr), blockdiag_b(bbi)], axis=2).astype(BF16)
    blockdiag_c = lambda cc: jnp.einsum('lghp,gk->lgpkh', cc.astype(F32), eye
                                        ).reshape(depth, g * p, g * h)
    cmat = jnp.concatenate([blockdiag_c(ssm_c_re), -blockdiag_c(ssm_c_im)], axis=1).astype(BF16)
    d_skip = ssm_d.astype(F32).reshape(depth, 1, g * h)
    wglu = w_glu.astype(BF16)
    bglu = b_glu.astype(F32)[:, None, :]

    c_pad = jnp.pad(c.astype(F32), ((0, 8 - b), (0, 0)))
    mod = _ada(c_pad, w_ada.astype(F32), b_ada.astype(F32))[:, :b].reshape(depth, b, N_ADA, d)

    x = x.astype(F32)
    for l in range(depth):
        (sbq, sbv, swk, fxk, u0, u1, sbkt, fxqt, fxvt, swqt, swvt) = _inproj(l, x, mod, n1, in_wts, forget_b)
        o_sb = _stick_breaking(sbq, sbkt, sbv)
        o_sw = _sliding_window(l, swqt, swk, swvt, bias, sinks_b)
        o_fx = _forgetting(fxqt, fxk, fxvt)
        o_ssm0, o_ssm1 = _ssm(l, u0, u1, avec, bmat, cmat, d_skip, wglu, bglu)
        x = _post(l, l == depth - 1, o_sb, o_sw, o_fx, o_ssm0, o_ssm1, x, mod, og, n2, wo, w1, w2, fg)
    return x
```
